```python
import math
import jax
import jax.numpy as jnp
from jax import lax
import numpy as np

D_MODEL = 1024
BATCH = 2
SEQ = 8192
DEPTH = 4
DEC_BATCH = 128
DEC_SEQ = 1
PAST_LEN = 2048
PAGE_SIZE = 128

D_MIX = D_MODEL
N_MIXERS = 4
W_GROUP = D_MIX // N_MIXERS
N_PROJ_BLOCKS = 10
D_IN = N_PROJ_BLOCKS * W_GROUP

POOL_WINDOWS = (2, 4, 8, 16)
POOL_GROUPS = len(POOL_WINDOWS)
POOL_CH = W_GROUP // POOL_GROUPS
POOL_BUF = max(POOL_WINDOWS) - 1

HG_HEADS = 4
HG_DK = W_GROUP // HG_HEADS
HG_CHUNK = 64

DA_HEADS = 4
DA_DV = W_GROUP // DA_HEADS
DA_HALF = DA_DV // 2
DA_DK = 2 * DA_HALF
Q_BLOCK = 128
REL_BUCKETS = 32
REL_MAX_DIST = 128

SG_HEADS = 4
SG_CH = W_GROUP // SG_HEADS
SG_CHUNK = 128

D_FF = 2816
ALPHA = (2.0 * DEPTH) ** 0.25
BETA = (8.0 * DEPTH) ** -0.25
EPS = 1e-5

kernel_name = 'hybrid_pool_hgrn2_diffattn_sgu_decoder_step'


def layer_norm(x, g, b):
    xf = x.astype(jnp.float32)
    mu = jnp.mean(xf, -1, keepdims=True)
    var = jnp.mean(jnp.square(xf - mu), -1, keepdims=True)
    return ((xf - mu) * lax.rsqrt(var + EPS)).astype(x.dtype) * g + b


def rms_norm(x, g):
    xf = x.astype(jnp.float32)
    return (xf * lax.rsqrt(jnp.mean(xf * xf, -1, keepdims=True) + EPS)).astype(x.dtype) * g


def swiglu(x, w_gu, w_dn):
    h = jnp.einsum('ntd,df->ntf', x, w_gu)
    gate, up = jnp.split(h, 2, axis=-1)
    return jnp.einsum('ntf,fd->ntd', jax.nn.silu(gate) * up, w_dn)


def pool_mix(a_ext, pos, w, scale):
    n, t = a_ext.shape[0], pos.shape[0]
    cs = jnp.cumsum(a_ext.astype(jnp.float32), axis=1)
    cs = jnp.pad(cs, ((0, 0), (1, 0), (0, 0)))
    end = cs[:, POOL_BUF + 1:]
    x_self = a_ext[:, POOL_BUF:].astype(jnp.float32)
    parts = []
    for g, win in enumerate(POOL_WINDOWS):
        ch = slice(g * POOL_CH, (g + 1) * POOL_CH)
        start = cs[:, POOL_BUF + 1 - win:POOL_BUF + 1 - win + t, ch]
        cnt = jnp.minimum(pos + 1, win).astype(jnp.float32)[None, :, None]
        parts.append((end[..., ch] - start) / cnt - x_self[..., ch])
    d = jnp.stack(parts, axis=2).astype(w.dtype)
    y = jnp.einsum('ntgc,gce->ntge', d, w).reshape(n, t, W_GROUP)
    return y * scale


def hgrn_chunked(q, k, v, logf, s0):
    n, t, h, _ = q.shape
    dv = v.shape[-1]
    c = min(HG_CHUNK, t)
    n_chunks = -(-t // c)
    pad = n_chunks * c - t

    def chunks(z):
        z = jnp.pad(z.astype(jnp.float32), ((0, 0), (0, pad), (0, 0), (0, 0)))
        return z.reshape(n, n_chunks, c, h, z.shape[-1]).transpose(1, 0, 3, 2, 4)

    causal = jnp.tril(jnp.ones((c, c), dtype=bool))[:, :, None]

    def step(s, inp):
        qc, kc, vc, lc = inp
        b = jnp.cumsum(lc, axis=2)
        rel = jnp.exp(jnp.where(causal, b[:, :, :, None, :] - b[:, :, None, :, :], -jnp.inf))
        att = jnp.einsum('nhtd,nhsd,nhtsd->nhts', qc, kc, rel)
        o = jnp.einsum('nhts,nhsv->nhtv', att, vc) + jnp.einsum('nhtd,nhdv->nhtv', qc * jnp.exp(b), s)
        b_last = b[:, :, -1:, :]
        s = jnp.exp(b_last[:, :, 0, :])[..., None] * s + jnp.einsum('nhsd,nhsv->nhdv', kc * jnp.exp(b_last - b), vc)
        return s, o

    s_fin, o = lax.scan(step, s0.astype(jnp.float32), (chunks(q), chunks(k), chunks(v), chunks(logf)))
    o = o.transpose(1, 0, 3, 2, 4).reshape(n, n_chunks * c, h, dv)[:, :t]
    return o, s_fin


def rel_bucket(dist):
    n = jnp.maximum(dist, 0)
    max_exact = REL_BUCKETS // 2
    large = max_exact + (jnp.log(jnp.maximum(n, 1).astype(jnp.float32) / max_exact)
                         / math.log(REL_MAX_DIST / max_exact) * (REL_BUCKETS - max_exact)).astype(jnp.int32)
    large = jnp.minimum(large, REL_BUCKETS - 1)
    return jnp.where(n < max_exact, n, large)


def diff_attend(q, k, v, q_pos, k_pos, rel_bias, lam):
    s = jnp.einsum('nqhmd,nkhmd->nhmqk', q, k).astype(jnp.float32) * (DA_HALF ** -0.5)
    dist = q_pos[:, None] - k_pos[None, :]
    bias = jnp.transpose(rel_bias[rel_bucket(dist)], (2, 0, 1)).astype(jnp.float32)
    s = jnp.where((dist >= 0)[None, None, None], s + bias[None, :, None], -jnp.inf)
    p = jax.nn.softmax(s, axis=-1)
    a = p[:, :, 0] - lam * p[:, :, 1]
    return jnp.einsum('nhqk,nkhv->nqhv', a.astype(v.dtype), v)


def diff_attn_prompt(q, k, v, rel_bias, lam):
    n, t = q.shape[:2]
    qb_len = min(Q_BLOCK, t)
    nb = t // qb_len
    qb = q.reshape(n, nb, qb_len, DA_HEADS, 2, DA_HALF).swapaxes(0, 1)
    k_pos = jnp.arange(t, dtype=jnp.int32)

    def one(args):
        q_blk, bi = args
        q_pos = bi * qb_len + jnp.arange(qb_len, dtype=jnp.int32)
        return diff_attend(q_blk, k, v, q_pos, k_pos, rel_bias, lam)

    o = lax.map(one, (qb, jnp.arange(nb, dtype=jnp.int32)))
    return o.swapaxes(0, 1).reshape(n, t, DA_HEADS, DA_DV)


def sgu(u, v, ln_gain, ln_bias, w_s, b_s):
    n, t, c = v.shape
    vn = layer_norm(v, ln_gain, ln_bias)
    L = min(t, SG_CHUNK)
    pad = (-t) % L
    n_chunks = (t + pad) // L
    vc = jnp.pad(vn, ((0, 0), (0, pad), (0, 0))).reshape(n, n_chunks, L, SG_HEADS, SG_CH)
    w = w_s[:, :L, :L] * jnp.tril(jnp.ones((L, L), w_s.dtype))
    s = jnp.einsum('gts,ncsgd->nctgd', w, vc) + b_s[:, :L].T[None, None, :, :, None]
    s = s.reshape(n, n_chunks * L, c)[:, :t]
    return u * s, vn


def mix_sublayer(x, pos, pool_prefix, hg_s0, kv_past, lp):
    n, t, _ = x.shape
    p = jnp.einsum('ntd,de->nte', x, lp['w_in'])
    a, hq, hf, hi, hg, dq, dk, dv, su, sv = jnp.split(p, N_PROJ_BLOCKS, axis=-1)
    a_ext = jnp.concatenate([pool_prefix.astype(a.dtype), a], axis=1)
    y_a = pool_mix(a_ext, pos, lp['pool_w'], lp['pool_scale'])
    new_pool = a_ext[:, -POOL_BUF:]
    lb = lp['lb']
    z = hf.astype(jnp.float32)
    logf = jnp.logaddexp(jnp.log(lb), jnp.log1p(-lb) + jax.nn.log_sigmoid(z))
    kin = (1.0 - lb) * jax.nn.sigmoid(-z)
    heads = lambda z_: z_.reshape(n, t, HG_HEADS, HG_DK)
    o_b, new_s = hgrn_chunked(heads(hq), heads(kin), heads(hi), heads(logf), hg_s0)
    y_b = rms_norm(o_b.astype(x.dtype), lp['hg_g']).reshape(n, t, W_GROUP) * jax.nn.silu(hg)
    q = dq.reshape(n, t, DA_HEADS, 2, DA_HALF)
    k = dk.reshape(n, t, DA_HEADS, 2, DA_HALF)
    v = dv.reshape(n, t, DA_HEADS, DA_DV)
    if kv_past is None:
        o_c = diff_attn_prompt(q, k, v, lp['rel_bias'], lp['lam'])
    else:
        k_past, v_past = kv_past
        n_past = k_past.shape[1]
        k_all = jnp.concatenate([k_past.reshape(n, n_past, DA_HEADS, 2, DA_HALF).astype(k.dtype), k], axis=1)
        v_all = jnp.concatenate([v_past.astype(v.dtype), v], axis=1)
        k_pos = jnp.arange(n_past + t, dtype=jnp.int32)
        o_c = diff_attend(q, k_all, v_all, pos, k_pos, lp['rel_bias'], lp['lam'])
    y_c = (rms_norm(o_c, lp['sub_g']) * (1.0 - lp['lam_init'])).reshape(n, t, W_GROUP)
    y_d, vn = sgu(su, sv, lp['sg_ln_g'], lp['sg_ln_b'], lp['sg_w'], lp['sg_b'])
    n_open = (t - 1) % SG_CHUNK + 1
    y = jnp.einsum('nte,ed->ntd', jnp.concatenate([y_a, y_b, y_c, y_d], axis=-1), lp['w_out'])
    state = (new_pool, new_s.astype(x.dtype), dk.reshape(n, t, DA_HEADS, DA_DK), v, vn[:, t - n_open:])
    return y, state


def trunk_layer(x, pos, pool_prefix, hg_s0, kv_past, lp):
    x = layer_norm(ALPHA * x + 0.5 * swiglu(x, lp['ffn1_w_gu'], lp['ffn1_w_dn']), lp['ln_g'][0], lp['ln_b'][0])
    y, state = mix_sublayer(x, pos, pool_prefix, hg_s0, kv_past, lp)
    x = layer_norm(ALPHA * x + y, lp['ln_g'][1], lp['ln_b'][1])
    x = layer_norm(ALPHA * x + 0.5 * swiglu(x, lp['ffn2_w_gu'], lp['ffn2_w_dn']), lp['ln_g'][2], lp['ln_b'][2])
    return x, state


def setup_inputs(seed: int = 0) -> dict:
    key = jax.random.key(seed)
    ks = jax.random.split(key, 32)
    n_pages = PAST_LEN // PAGE_SIZE
    n_pool = (DEC_BATCH * n_pages * 5) // 4
    f32 = jnp.float32
    nrm = lambda k, shape, s: jax.random.normal(k, shape, f32) * s
    page_table = jax.random.permutation(ks[6], n_pool)[:DEC_BATCH * n_pages].reshape(DEC_BATCH, n_pages).astype(jnp.int32)
    return {
        'x_prompt': nrm(ks[0], (BATCH, SEQ, D_MODEL), 1.0),
        'x_sample': nrm(ks[1], (DEC_BATCH, DEC_SEQ, D_MODEL), 1.0),
        'state_pool': nrm(ks[2], (DEPTH, DEC_BATCH, POOL_BUF, W_GROUP), 1.0),
        'state_hgrn': nrm(ks[3], (DEPTH, DEC_BATCH, HG_HEADS, HG_DK, HG_DK), 0.5),
        'cache_k': nrm(ks[4], (DEPTH, n_pool, PAGE_SIZE, DA_HEADS, DA_DK), 1.0),
        'cache_v': nrm(ks[5], (DEPTH, n_pool, PAGE_SIZE, DA_HEADS, DA_DV), 1.0),
        'page_table': page_table,
        'rel_bias': nrm(ks[7], (REL_BUCKETS, DA_HEADS), 0.5),
        'ln_g': 1.0 + nrm(ks[8], (DEPTH, 3, D_MODEL), 0.02),
        'ln_b': nrm(ks[9], (DEPTH, 3, D_MODEL), 0.02),
        'ffn1_w_gu': nrm(ks[10], (DEPTH, D_MODEL, 2 * D_FF), D_MODEL ** -0.5),
        'ffn1_w_dn': nrm(ks[11], (DEPTH, D_FF, D_MODEL), BETA * D_FF ** -0.5),
        'ffn2_w_gu': nrm(ks[12], (DEPTH, D_MODEL, 2 * D_FF), D_MODEL ** -0.5),
        'ffn2_w_dn': nrm(ks[13], (DEPTH, D_FF, D_MODEL), BETA * D_FF ** -0.5),
        'w_in': nrm(ks[14], (DEPTH, D_MODEL, D_IN), D_MODEL ** -0.5),
        'w_out': nrm(ks[15], (DEPTH, D_MIX, D_MODEL), BETA * D_MIX ** -0.5),
        'pool_w': nrm(ks[16], (DEPTH, POOL_GROUPS, POOL_CH, POOL_CH), POOL_CH ** -0.5),
        'pool_scale': 1.0 + nrm(ks[17], (DEPTH, W_GROUP), 0.1),
        'hgrn_lb': nrm(ks[18], (DEPTH, W_GROUP), 0.1),
        'hgrn_norm_g': 1.0 + nrm(ks[19], (DEPTH, HG_DK), 0.02),
        'diff_lam_q1': nrm(ks[20], (DEPTH, DA_HALF), 0.1),
        'diff_lam_k1': nrm(ks[21], (DEPTH, DA_HALF), 0.1),
        'diff_lam_q2': nrm(ks[22], (DEPTH, DA_HALF), 0.1),
        'diff_lam_k2': nrm(ks[23], (DEPTH, DA_HALF), 0.1),
        'diff_subln_g': 1.0 + nrm(ks[24], (DEPTH, DA_DV), 0.02),
        'sgu_ln_g': 1.0 + nrm(ks[25], (DEPTH, W_GROUP), 0.02),
        'sgu_ln_b': nrm(ks[26], (DEPTH, W_GROUP), 0.02),
        'sgu_w': nrm(ks[27], (DEPTH, SG_HEADS, SG_CHUNK, SG_CHUNK), SG_CHUNK ** -0.5),
        'sgu_b': 1.0 + nrm(ks[28], (DEPTH, SG_HEADS, SG_CHUNK), 0.1),
    }


def reference(x_prompt, x_sample, state_pool, state_hgrn, cache_k, cache_v, page_table,
              rel_bias, ln_g, ln_b, ffn1_w_gu, ffn1_w_dn, ffn2_w_gu, ffn2_w_dn, w_in, w_out,
              pool_w, pool_scale, hgrn_lb, hgrn_norm_g, diff_lam_q1, diff_lam_k1,
              diff_lam_q2, diff_lam_k2, diff_subln_g, sgu_ln_g, sgu_ln_b, sgu_w, sgu_b):
    f32 = jnp.float32
    bp, tp, _ = x_prompt.shape
    bs, ts, _ = x_sample.shape
    pos_p = jnp.arange(tp, dtype=jnp.int32)
    pos_s = PAST_LEN + jnp.arange(ts, dtype=jnp.int32)
    lb_cum = jnp.cumsum(jax.nn.softmax(hgrn_lb.astype(f32), axis=0), axis=0)
    lb_all = jnp.maximum(lb_cum - lb_cum[:1], 0.0)
    xp, xs = x_prompt, x_sample
    st_p = [[], [], [], [], []]
    st_s = [[], [], [], [], []]
    for l in range(DEPTH):
        lam_init = 0.8 - 0.6 * math.exp(-0.3 * l)
        lam = (jnp.exp(jnp.sum(diff_lam_q1[l].astype(f32) * diff_lam_k1[l].astype(f32)))
               - jnp.exp(jnp.sum(diff_lam_q2[l].astype(f32) * diff_lam_k2[l].astype(f32))) + lam_init)
        lp = {
            'w_in': w_in[l], 'w_out': w_out[l], 'ln_g': ln_g[l], 'ln_b': ln_b[l],
            'ffn1_w_gu': ffn1_w_gu[l], 'ffn1_w_dn': ffn1_w_dn[l],
            'ffn2_w_gu': ffn2_w_gu[l], 'ffn2_w_dn': ffn2_w_dn[l],
            'pool_w': pool_w[l], 'pool_scale': pool_scale[l],
            'lb': lb_all[l], 'hg_g': hgrn_norm_g[l],
            'rel_bias': rel_bias, 'lam': lam, 'lam_init': lam_init, 'sub_g': diff_subln_g[l],
            'sg_ln_g': sgu_ln_g[l], 'sg_ln_b': sgu_ln_b[l], 'sg_w': sgu_w[l], 'sg_b': sgu_b[l],
        }
        xp, sp = trunk_layer(xp, pos_p, jnp.zeros((bp, POOL_BUF, W_GROUP), xp.dtype),
                             jnp.zeros((bp, HG_HEADS, HG_DK, HG_DK), f32), None, lp)
        k_past = cache_k[l][page_table].reshape(bs, -1, DA_HEADS, DA_DK)
        v_past = cache_v[l][page_table].reshape(bs, -1, DA_HEADS, DA_DV)
        xs, ss = trunk_layer(xs, pos_s, state_pool[l], state_hgrn[l].astype(f32), (k_past, v_past), lp)
        for i in range(5):
            st_p[i].append(sp[i])
            st_s[i].append(ss[i])
    pool_p, hgrn_p, k_p, v_p, sgv_p = [jnp.stack(z, axis=0) for z in st_p]
    pool_s, hgrn_s, k_s, v_s, sgv_s = [jnp.stack(z, axis=0) for z in st_s]
    return (xp, xs, pool_p, pool_s, hgrn_p, hgrn_s, k_p, k_s, v_p, v_s, sgv_p, sgv_s)
```

```python
import functools
import math

import jax
import jax.numpy as jnp
import jax.scipy.linalg
from jax import lax
from jax.experimental import pallas as pl
from jax.experimental.pallas import tpu as pltpu

F32 = jnp.float32
BF16 = jnp.bfloat16

N_MIX = 4
W_GROUP = 256
N_HEADS = 4
D_HEAD = 64
DA_HALF = 32
POOL_WINDOWS = (2, 4, 8, 16)
POOL_BUF = 15
PAGE = 128
SG_CHUNK = 128
HG_SUB = 16
REL_BUCKETS = 32
REL_MAX_DIST = 128
EPS = 1e-5
NEG = -1e30

VMEM_LIMIT = 56 * 1024 * 1024
ROW_TILE = 512
ATT_TILE = 256

NT_DIMS = (((1,), (1,)), ((), ()))


def _params(*sem):
    return pltpu.CompilerParams(dimension_semantics=sem, vmem_limit_bytes=VMEM_LIMIT)


def _const_spec(shape, single=False):
    nd = len(shape)
    kw = {"pipeline_mode": pl.Buffered(1)} if single else {}
    return pl.BlockSpec(shape, lambda *_: (0,) * nd, **kw)


def _layer_norm(y, g, b):
    mu = jnp.mean(y, axis=-1, keepdims=True)
    yc = y - mu
    var = jnp.mean(yc * yc, axis=-1, keepdims=True)
    return yc * lax.rsqrt(var + EPS) * g + b


def _silu(x):
    return x * jax.nn.sigmoid(x)


def _log_forget(z, log_lb, log1m_lb):
    log_sig = -(jnp.maximum(-z, 0.0) + jnp.log1p(jnp.exp(-jnp.abs(z))))
    b = log1m_lb + log_sig
    return jnp.maximum(log_lb, b) + jnp.log1p(jnp.exp(-jnp.abs(log_lb - b)))


def _ffn_kernel(x_ref, wgu_ref, wdn_ref, g_ref, b_ref, o_ref, *, d_ff, n_chunks, alpha):
    x = x_ref[...]
    xb = x.astype(BF16)
    tf = d_ff // n_chunks
    acc = None
    for c in range(n_chunks):
        gate = jnp.dot(xb, wgu_ref[:, c * tf:(c + 1) * tf], preferred_element_type=F32)
        up = jnp.dot(xb, wgu_ref[:, d_ff + c * tf:d_ff + (c + 1) * tf], preferred_element_type=F32)
        h = (_silu(gate) * up).astype(BF16)
        part = jnp.dot(h, wdn_ref[c * tf:(c + 1) * tf, :], preferred_element_type=F32)
        acc = part if acc is None else acc + part
    o_ref[...] = _layer_norm(alpha * x + 0.5 * acc, g_ref[...], b_ref[...])


def _ffn(x, w_gu, w_dn, g, b, alpha):
    r, d = x.shape
    d_ff = w_dn.shape[0]
    tm = min(ROW_TILE, r)
    return pl.pallas_call(
        functools.partial(_ffn_kernel, d_ff=d_ff, n_chunks=2, alpha=alpha),
        grid=(r // tm,),
        in_specs=[
            pl.BlockSpec((tm, d), lambda i: (i, 0)),
            _const_spec(w_gu.shape, single=True),
            _const_spec(w_dn.shape, single=True),
            _const_spec((1, d)),
            _const_spec((1, d)),
        ],
        out_specs=pl.BlockSpec((tm, d), lambda i: (i, 0)),
        out_shape=jax.ShapeDtypeStruct((r, d), F32),
        compiler_params=_params("arbitrary"),
        name="ffn",
    )(x, w_gu, w_dn, g, b)


def _inproj_p_kernel(x_ref, w_ref, wkvt_ref, p_ref, q_ref, v_ref, kvt_ref, *, q_scale):
    xb = x_ref[...].astype(BF16)
    p = jnp.dot(xb, w_ref[...], preferred_element_type=F32)
    p_ref[...] = p
    for h in range(N_HEADS):
        q_ref[h] = (p[:, 5 * W_GROUP + h * D_HEAD:5 * W_GROUP + (h + 1) * D_HEAD] * q_scale).astype(BF16)
        v_ref[h] = p[:, 7 * W_GROUP + h * D_HEAD:7 * W_GROUP + (h + 1) * D_HEAD].astype(BF16)
    kvt_ref[0] = lax.dot_general(wkvt_ref[...], xb, NT_DIMS, preferred_element_type=F32)


def _inproj_prompt(x, w_in, w_kvt, n_batch):
    r, d = x.shape
    t = r // n_batch
    d_in = w_in.shape[1]
    tm = min(ROW_TILE, t)
    tpb = t // tm
    return pl.pallas_call(
        functools.partial(_inproj_p_kernel, q_scale=DA_HALF ** -0.5),
        grid=(r // tm,),
        in_specs=[
            pl.BlockSpec((tm, d), lambda i: (i, 0)),
            _const_spec(w_in.shape),
            _const_spec(w_kvt.shape),
        ],
        out_specs=[
            pl.BlockSpec((tm, d_in), lambda i: (i, 0)),
            pl.BlockSpec((N_HEADS, tm, D_HEAD), lambda i: (0, i, 0)),
            pl.BlockSpec((N_HEADS, tm, D_HEAD), lambda i: (0, i, 0)),
            pl.BlockSpec((1, 2 * W_GROUP, tm), lambda i: (i // tpb, 0, i % tpb)),
        ],
        out_shape=[
            jax.ShapeDtypeStruct((r, d_in), F32),
            jax.ShapeDtypeStruct((N_HEADS, r, D_HEAD), BF16),
            jax.ShapeDtypeStruct((N_HEADS, r, D_HEAD), BF16),
            jax.ShapeDtypeStruct((n_batch, 2 * W_GROUP, t), F32),
        ],
        compiler_params=_params("arbitrary"),
        name="inproj_prompt",
    )(x, w_in, w_kvt)


def _inproj_s_kernel(x_ref, w_ref, wt_ref, p_ref, pt_ref):
    xb = x_ref[...].astype(BF16)
    p_ref[...] = jnp.dot(xb, w_ref[...], preferred_element_type=F32)
    pt_ref[...] = lax.dot_general(wt_ref[...], xb, NT_DIMS, preferred_element_type=F32)


def _inproj_sample(x, w_in, w_int):
    n, d = x.shape
    d_in = w_in.shape[1]
    return pl.pallas_call(
        _inproj_s_kernel,
        grid=(1,),
        in_specs=[_const_spec((n, d)), _const_spec(w_in.shape), _const_spec(w_int.shape)],
        out_specs=[_const_spec((n, d_in)), _const_spec((d_in, n))],
        out_shape=[jax.ShapeDtypeStruct((n, d_in), F32), jax.ShapeDtypeStruct((d_in, n), F32)],
        compiler_params=_params("arbitrary"),
        name="inproj_sample",
    )(x, w_in, w_int)


def _group_select(grp, parts):
    out = parts[N_HEADS - 1]
    for g in range(N_HEADS - 2, -1, -1):
        out = jnp.where(grp == g, parts[g], out)
    return out


def _mix_p_kernel(pa_ref, pd_ref, poolw_ref, pscale_ref, loglb_ref, log1mlb_ref, omlb_ref, hgg_ref,
                  sgg_ref, sgb_ref, sgw_ref, sgbias_ref, eones_ref,
                  y_ref, pool_ref, st_out_ref, sgv_ref,
                  prev_ref, st_ref, bl_ref, q_ref, k_ref, v_ref, o_ref):
    c = SG_CHUNK
    w = W_GROUP
    t = pl.program_id(1)

    @pl.when(t == 0)
    def _():
        prev_ref[...] = jnp.zeros_like(prev_ref)
        st_ref[...] = jnp.zeros_like(st_ref)

    a = pa_ref[:, 0:w]
    hq = pa_ref[:, w:2 * w]
    hf = pa_ref[:, 2 * w:3 * w]
    hi = pa_ref[:, 3 * w:4 * w]
    hg = pa_ref[:, 4 * w:5 * w]
    su = pd_ref[:, 0:w]
    sv = pd_ref[:, w:2 * w]
    lane = lax.broadcasted_iota(jnp.int32, (c, w), 1)
    row = lax.broadcasted_iota(jnp.int32, (c, w), 0)
    grp = lane >> 6

    e = jnp.concatenate([prev_ref[...], a], axis=0)
    s2 = e + pltpu.roll(e, 1, 0)
    s4 = s2 + pltpu.roll(s2, 2, 0)
    s8 = s4 + pltpu.roll(s4, 4, 0)
    s16 = s8 + pltpu.roll(s8, 8, 0)
    wsum = _group_select(grp, [s2[16:], s4[16:], s8[16:], s16[16:]])
    win = _group_select(grp, [jnp.full((c, w), v, jnp.int32) for v in POOL_WINDOWS])
    cnt = jnp.minimum(t * c + row + 1, win).astype(F32)
    dpool = wsum / cnt - a
    ya = jnp.dot(dpool.astype(BF16), poolw_ref[...], preferred_element_type=F32) * pscale_ref[...]
    prev_ref[...] = a[c - 16:]
    pool_ref[0] = a[c - 16:]

    vn = _layer_norm(sv, sgg_ref[...], sgb_ref[...])
    sg = jnp.dot(sgw_ref[...], vn.astype(BF16), preferred_element_type=F32)
    s_gate = _group_select(grp, [sg[g * c:(g + 1) * c] for g in range(N_HEADS)]) + sgbias_ref[...]
    yd = su * s_gate
    sgv_ref[0] = vn

    logf = _log_forget(hf, loglb_ref[...], log1mlb_ref[...])
    kin = omlb_ref[...] * jax.nn.sigmoid(-hf)
    r16 = row & (HG_SUB - 1)
    bl = logf
    rv = logf
    for sh in (1, 2, 4, 8):
        bl = bl + jnp.where(r16 >= sh, pltpu.roll(bl, sh, 0), 0.0)
        rv = rv + jnp.where(r16 + sh < HG_SUB, pltpu.roll(rv, c - sh, 0), 0.0)
    qt = (hq * jnp.exp(bl)).astype(BF16)
    kt = kin * jnp.exp(rv - logf)
    bl_ref[...] = bl
    q_ref[...] = hq
    k_ref[...] = kin
    v_ref[...] = hi
    vtb = hi.T.astype(BF16)
    bi0 = lax.broadcasted_iota(jnp.int32, (w, w), 0) >> 6
    bi1 = lax.broadcasted_iota(jnp.int32, (w, w), 1) >> 6
    blockmask = bi0 == bi1
    grp_st = lax.broadcasted_iota(jnp.int32, (D_HEAD, w), 1) >> 6
    st = st_ref[...]
    o_inter = []
    for i in range(c // HG_SUB):
        sb = jnp.where(blockmask, jnp.concatenate([st] * N_HEADS, axis=0), 0.0).astype(BF16)
        o_inter.append(lax.dot_general(qt[i * HG_SUB:(i + 1) * HG_SUB], sb, NT_DIMS,
                                       preferred_element_type=F32))
        km = jnp.where((row >> 4) == i, kt, 0.0).astype(BF16)
        u = jnp.dot(vtb, km, preferred_element_type=F32)
        ut = _group_select(grp_st, [u[g * D_HEAD:(g + 1) * D_HEAD] for g in range(N_HEADS)])
        st = jnp.exp(rv[i * HG_SUB:i * HG_SUB + 1]) * st + ut
    st_ref[...] = st
    st_out_ref[0] = st
    o_ref[...] = jnp.concatenate(o_inter, axis=0)

    def intra(i, carry):
        r0 = pl.multiple_of(i * HG_SUB, HG_SUB)
        bli = bl_ref[pl.ds(r0, HG_SUB), :]
        qi = q_ref[pl.ds(r0, HG_SUB), :]
        ki = k_ref[pl.ds(r0, HG_SUB), :]
        vi = v_ref[pl.ds(r0, HG_SUB), :]
        rt = lax.broadcasted_iota(jnp.int32, (HG_SUB, w), 0)
        xs = []
        for s in range(HG_SUB):
            ratio = jnp.exp(jnp.minimum(bli - bli[s:s + 1], 0.0))
            xs.append(jnp.where(rt >= s, ratio * qi * ki[s:s + 1], 0.0).astype(BF16))
        x = jnp.concatenate(xs, axis=0)
        r = jnp.dot(x, eones_ref[...], preferred_element_type=F32)
        oi = r[0:HG_SUB] * vi[0:1]
        for s in range(1, HG_SUB):
            oi = oi + r[s * HG_SUB:(s + 1) * HG_SUB] * vi[s:s + 1]
        o_ref[pl.ds(r0, HG_SUB), :] = o_ref[pl.ds(r0, HG_SUB), :] + oi
        return carry

    lax.fori_loop(0, c // HG_SUB, intra, 0)

    o = o_ref[...]
    ms = [jnp.mean(jnp.square(o[:, g * D_HEAD:(g + 1) * D_HEAD]), axis=-1, keepdims=True)
          for g in range(N_HEADS)]
    yb = o * lax.rsqrt(_group_select(grp, ms) + EPS) * hgg_ref[...] * _silu(hg)

    y_ref[:, 0:w] = ya.astype(BF16)
    y_ref[:, w:2 * w] = yb.astype(BF16)
    y_ref[:, 2 * w:3 * w] = yd.astype(BF16)


def _mix_prompt(p, n_batch, lp):
    r = p.shape[0]
    t = r // n_batch
    c = SG_CHUNK
    nt = t // c
    w = W_GROUP
    row = lambda: _const_spec((1, w))
    return pl.pallas_call(
        _mix_p_kernel,
        grid=(n_batch, nt),
        in_specs=[
            pl.BlockSpec((c, 5 * w), lambda b, i: (b * nt + i, 0)),
            pl.BlockSpec((c, 2 * w), lambda b, i: (b * nt + i, 4)),
            _const_spec((w, w)), row(), row(), row(), row(), row(), row(), row(),
            _const_spec((N_HEADS * c, c)), _const_spec((c, w)), _const_spec((w, w)),
        ],
        out_specs=[
            pl.BlockSpec((c, 3 * w), lambda b, i: (b * nt + i, 0)),
            pl.BlockSpec((1, 16, w), lambda b, i: (b, 0, 0)),
            pl.BlockSpec((1, D_HEAD, w), lambda b, i: (b, 0, 0)),
            pl.BlockSpec((1, c, w), lambda b, i: (b, 0, 0)),
        ],
        out_shape=[
            jax.ShapeDtypeStruct((r, 3 * w), BF16),
            jax.ShapeDtypeStruct((n_batch, 16, w), F32),
            jax.ShapeDtypeStruct((n_batch, D_HEAD, w), F32),
            jax.ShapeDtypeStruct((n_batch, c, w), F32),
        ],
        scratch_shapes=[
            pltpu.VMEM((16, w), F32), pltpu.VMEM((D_HEAD, w), F32),
            pltpu.VMEM((c, w), F32), pltpu.VMEM((c, w), F32), pltpu.VMEM((c, w), F32),
            pltpu.VMEM((c, w), F32), pltpu.VMEM((c, w), F32),
        ],
        compiler_params=_params("arbitrary", "arbitrary"),
        name="mix_prompt",
    )(p, p, lp["pool_w"], lp["pool_scale"], lp["log_lb"], lp["log1m_lb"], lp["om_lb"], lp["hg_g"],
      lp["sg_g"], lp["sg_b"], lp["sg_w"], lp["sg_bias"], lp["eones"])


def _attn_p_kernel(cfar_ref, lam_ref, q_ref, kt_ref, v_ref, d0_ref, d1_ref, subg_ref, o_ref,
                   kb_ref, m_ref, l_ref, acc_ref, *, tq):
    h = pl.program_id(1)
    qi = pl.program_id(2)

    @pl.when(qi == 0)
    def _():
        kb_ref[...] = kt_ref[0].astype(BF16)

    q = q_ref[0]
    lane = lax.broadcasted_iota(jnp.int32, q.shape, 1)
    zero = jnp.zeros_like(q)
    qm = (jnp.where(lane < DA_HALF, q, zero), jnp.where(lane >= DA_HALF, q, zero))
    m_ref[...] = jnp.full_like(m_ref, NEG)
    l_ref[...] = jnp.zeros_like(l_ref)
    acc_ref[...] = jnp.zeros_like(acc_ref)

    def process(ki, bias):
        k0 = pl.multiple_of(ki * tq, tq)
        kblk = kb_ref[:, pl.ds(k0, tq)]
        vblk = v_ref[0, pl.ds(k0, tq), :]
        for mi in range(2):
            s = jnp.dot(qm[mi], kblk, preferred_element_type=F32) + bias
            m_old = m_ref[mi]
            m_new = jnp.maximum(m_old, jnp.max(s, axis=-1, keepdims=True))
            alpha = jnp.exp(m_old - m_new)
            p = jnp.exp(s - m_new)
            l_ref[mi] = alpha * l_ref[mi] + jnp.sum(p, axis=-1, keepdims=True)
            acc_ref[mi] = alpha * acc_ref[mi] + jnp.dot(p.astype(BF16), vblk, preferred_element_type=F32)
            m_ref[mi] = m_new

    cfar = cfar_ref[h]

    def far_body(ki, carry):
        process(ki, cfar)
        return carry

    lax.fori_loop(0, jnp.maximum(qi - 1, 0), far_body, 0)

    @pl.when(qi >= 1)
    def _():
        process(qi - 1, d1_ref[0])

    process(qi, d0_ref[0])

    o = acc_ref[0] / l_ref[0] - lam_ref[0] * (acc_ref[1] / l_ref[1])
    ms = jnp.mean(o * o, axis=-1, keepdims=True)
    o_ref[0] = (o * lax.rsqrt(ms + EPS) * subg_ref[...]).astype(BF16)


def _attn_prompt(q, kvt, v, lp, n_batch):
    r = q.shape[1]
    t = r // n_batch
    tq = min(ATT_TILE, t)
    nq = t // tq
    smem = pl.BlockSpec(memory_space=pltpu.SMEM)
    return pl.pallas_call(
        functools.partial(_attn_p_kernel, tq=tq),
        grid=(n_batch, N_HEADS, nq),
        in_specs=[
            smem, smem,
            pl.BlockSpec((1, tq, D_HEAD), lambda b, h, i: (h, b * nq + i, 0)),
            pl.BlockSpec((1, D_HEAD, t), lambda b, h, i: (b, h, 0)),
            pl.BlockSpec((1, t, D_HEAD), lambda b, h, i: (h, b, 0)),
            pl.BlockSpec((1, tq, tq), lambda b, h, i: (h, 0, 0)),
            pl.BlockSpec((1, tq, tq), lambda b, h, i: (h, 0, 0)),
            _const_spec((1, D_HEAD)),
        ],
        out_specs=pl.BlockSpec((1, tq, D_HEAD), lambda b, h, i: (h, b * nq + i, 0)),
        out_shape=jax.ShapeDtypeStruct((N_HEADS, r, D_HEAD), BF16),
        scratch_shapes=[
            pltpu.VMEM((D_HEAD, t), BF16),
            pltpu.VMEM((2, tq, 1), F32), pltpu.VMEM((2, tq, 1), F32), pltpu.VMEM((2, tq, D_HEAD), F32),
        ],
        compiler_params=_params("arbitrary", "arbitrary", "arbitrary"),
        name="attn_prompt",
    )(lp["c_far"], lp["lam"], q, kvt, v, lp["bias_d0"], lp["bias_d1"], lp["sub_g"])


def _outproj_p_kernel(x_ref, y_ref, yc_ref, wabd_ref, wc_ref, g_ref, b_ref, o_ref, *, alpha):
    acc = jnp.dot(y_ref[...], wabd_ref[...], preferred_element_type=F32)
    for h in range(N_HEADS):
        acc = acc + jnp.dot(yc_ref[h], wc_ref[h], preferred_element_type=F32)
    o_ref[...] = _layer_norm(alpha * x_ref[...] + acc, g_ref[...], b_ref[...])


def _outproj_prompt(x, y_abd, y_c, w_abd, w_c, g, b, alpha):
    r, d = x.shape
    tm = min(ROW_TILE, r)
    return pl.pallas_call(
        functools.partial(_outproj_p_kernel, alpha=alpha),
        grid=(r // tm,),
        in_specs=[
            pl.BlockSpec((tm, d), lambda i: (i, 0)),
            pl.BlockSpec((tm, 3 * W_GROUP), lambda i: (i, 0)),
            pl.BlockSpec((N_HEADS, tm, D_HEAD), lambda i: (0, i, 0)),
            _const_spec(w_abd.shape), _const_spec(w_c.shape), _const_spec((1, d)), _const_spec((1, d)),
        ],
        out_specs=pl.BlockSpec((tm, d), lambda i: (i, 0)),
        out_shape=jax.ShapeDtypeStruct((r, d), F32),
        compiler_params=_params("arbitrary"),
        name="outproj_prompt",
    )(x, y_abd, y_c, w_abd, w_c, g, b)


def _outproj_s_kernel(x_ref, ya_ref, ybt_ref, yct_ref, yd_ref, w_ref, g_ref, b_ref, o_ref, *, alpha):
    w = W_GROUP
    parts = (ya_ref[...], ybt_ref[...].T, yct_ref[...].T, yd_ref[...])
    acc = None
    for j, part in enumerate(parts):
        d = jnp.dot(part.astype(BF16), w_ref[j * w:(j + 1) * w, :], preferred_element_type=F32)
        acc = d if acc is None else acc + d
    o_ref[...] = _layer_norm(alpha * x_ref[...] + acc, g_ref[...], b_ref[...])


def _outproj_sample(x, y_a, y_bt, y_ct, y_d, w_out, g, b, alpha):
    n, d = x.shape
    w = W_GROUP
    return pl.pallas_call(
        functools.partial(_outproj_s_kernel, alpha=alpha),
        grid=(1,),
        in_specs=[
            _const_spec((n, d)), _const_spec((n, w)), _const_spec((w, n)), _const_spec((w, n)),
            _const_spec((n, w)), _const_spec(w_out.shape), _const_spec((1, d)), _const_spec((1, d)),
        ],
        out_specs=_const_spec((n, d)),
        out_shape=jax.ShapeDtypeStruct((n, d), F32),
        compiler_params=_params("arbitrary"),
        name="outproj_sample",
    )(x, y_a, y_bt, y_ct, y_d, w_out, g, b)


def _mix_s_kernel(pa_ref, pd_ref, pool_ref, poolw_ref, pscale_ref, sgg_ref, sgb_ref, w00_ref, b0_ref,
                  ya_ref, yd_ref, newpool_ref, vn_ref, *, cnts):
    a = pa_ref[...]
    n, w = a.shape
    grp = lax.broadcasted_iota(jnp.int32, (n, w), 1) >> 6
    acc = a
    means = []
    for j in range(1, POOL_BUF + 1):
        acc = acc + pool_ref[POOL_BUF - j]
        if j + 1 in POOL_WINDOWS:
            means.append(acc / cnts[POOL_WINDOWS.index(j + 1)])
    dpool = _group_select(grp, means) - a
    ya_ref[...] = jnp.dot(dpool.astype(BF16), poolw_ref[...], preferred_element_type=F32) * pscale_ref[...]
    for j in range(POOL_BUF - 1):
        newpool_ref[j] = pool_ref[j + 1]
    newpool_ref[POOL_BUF - 1] = a

    vn = _layer_norm(pd_ref[:, w:2 * w], sgg_ref[...], sgb_ref[...])
    vn_ref[...] = vn
    yd_ref[...] = pd_ref[:, 0:w] * (w00_ref[...] * vn + b0_ref[...])


def _mix_sample(p, pool, lp, past_len):
    n = p.shape[0]
    w = W_GROUP
    cnts = tuple(float(min(past_len + 1, win)) for win in POOL_WINDOWS)
    row = lambda: _const_spec((1, w))
    return pl.pallas_call(
        functools.partial(_mix_s_kernel, cnts=cnts),
        grid=(1,),
        in_specs=[
            pl.BlockSpec((n, w), lambda i: (0, 0)),
            pl.BlockSpec((n, 2 * w), lambda i: (0, 4)),
            _const_spec(pool.shape), _const_spec((w, w)), row(), row(), row(), row(), row(),
        ],
        out_specs=[_const_spec((n, w)), _const_spec((n, w)), _const_spec(pool.shape), _const_spec((n, w))],
        out_shape=[
            jax.ShapeDtypeStruct((n, w), F32), jax.ShapeDtypeStruct((n, w), F32),
            jax.ShapeDtypeStruct(pool.shape, F32), jax.ShapeDtypeStruct((n, w), F32),
        ],
        compiler_params=_params("arbitrary"),
        name="mix_sample",
    )(p, p, pool, lp["pool_w"], lp["pool_scale"], lp["sg_g"], lp["sg_b"], lp["sg_w00"], lp["sg_b0"])


def _hgrn_s_kernel(q_ref, f_ref, i_ref, g_ref, s_ref, loglb_ref, log1mlb_ref, omlb_ref, hgg_ref,
                   so_ref, yb_ref):
    z = f_ref[...]
    f = jnp.exp(_log_forget(z, loglb_ref[...], log1mlb_ref[...]))
    kin = omlb_ref[...] * jax.nn.sigmoid(-z)
    q = q_ref[...]
    v = i_ref[...]
    o = jnp.zeros_like(v)
    for d in range(D_HEAD):
        sn = f[d:d + 1] * s_ref[0, d] + kin[d:d + 1] * v
        so_ref[0, d] = sn
        o = o + q[d:d + 1] * sn
    ms = jnp.mean(o * o, axis=0, keepdims=True)
    yb_ref[...] = o * lax.rsqrt(ms + EPS) * hgg_ref[...] * _silu(g_ref[...])


def _hgrn_sample(pt, state, lp):
    n = pt.shape[1]
    dh = D_HEAD
    blk = lambda off: pl.BlockSpec((dh, n), lambda h: (off * N_HEADS + h, 0))
    col = lambda: pl.BlockSpec((dh, 1), lambda h: (h, 0))
    return pl.pallas_call(
        _hgrn_s_kernel,
        grid=(N_HEADS,),
        in_specs=[
            blk(1), blk(2), blk(3), blk(4),
            pl.BlockSpec((1, dh, dh, n), lambda h: (h, 0, 0, 0)),
            col(), col(), col(), _const_spec((dh, 1)),
        ],
        out_specs=[
            pl.BlockSpec((1, dh, dh, n), lambda h: (h, 0, 0, 0)),
            pl.BlockSpec((dh, n), lambda h: (h, 0)),
        ],
        out_shape=[
            jax.ShapeDtypeStruct(state.shape, F32),
            jax.ShapeDtypeStruct((W_GROUP, n), F32),
        ],
        compiler_params=_params("arbitrary"),
        name="hgrn_sample",
    )(pt, pt, pt, pt, state, lp["log_lb_col"], lp["log1m_lb_col"], lp["om_lb_col"], lp["hg_g_col"])


def _attn_s_kernel(pt_ref, lam_ref, qt_ref, knt_ref, vnt_ref, bias_ref, bself_ref, subg_ref, *rest,
                   n_pages, q_scale):
    k_pages = rest[:n_pages]
    v_pages = rest[n_pages:2 * n_pages]
    o_ref = rest[2 * n_pages]
    ot_ref = rest[2 * n_pages + 1]
    del pt_ref
    n = pl.program_id(0)
    w, ns = qt_ref.shape
    dh = D_HEAD
    sel = lax.broadcasted_iota(jnp.int32, (w, ns), 1) == n

    def column(ref):
        return jnp.sum(jnp.where(sel, ref[...], 0.0), axis=1, keepdims=True)

    qcol = column(qt_ref) * q_scale
    kcol = column(knt_ref)
    vcol = column(vnt_ref)
    lam = lam_ref[0]

    def full_max(x):
        return jnp.max(jnp.max(x, axis=1, keepdims=True), axis=0, keepdims=True)

    def full_sum(x):
        return jnp.sum(jnp.sum(x, axis=1, keepdims=True), axis=0, keepdims=True)

    cols = []
    for h in range(N_HEADS):
        qh = qcol[h * dh:(h + 1) * dh]
        kh = kcol[h * dh:(h + 1) * dh]
        rows = ([], [])
        for j in range(n_pages):
            prod = k_pages[j][h] * qh
            rows[0].append(jnp.sum(prod[0:DA_HALF], axis=0, keepdims=True))
            rows[1].append(jnp.sum(prod[DA_HALF:dh], axis=0, keepdims=True))
        self_prod = qh * kh
        probs, probs_self = [], []
        for mi in range(2):
            s = jnp.concatenate(rows[mi], axis=0) + bias_ref[h]
            s_self = (jnp.sum(self_prod[mi * DA_HALF:(mi + 1) * DA_HALF], axis=0, keepdims=True)
                      + bself_ref[h])
            m = jnp.maximum(full_max(s), s_self)
            p = jnp.exp(s - m)
            p_self = jnp.exp(s_self - m)
            l = full_sum(p) + p_self
            probs.append(p / l)
            probs_self.append(p_self / l)
        a = probs[0] - lam * probs[1]
        a_self = probs_self[0] - lam * probs_self[1]
        oh = v_pages[0][h] * a[0:1]
        for j in range(1, n_pages):
            oh = oh + v_pages[j][h] * a[j:j + 1]
        cols.append(jnp.sum(oh, axis=1, keepdims=True) + a_self * vcol[h * dh:(h + 1) * dh])
    ocol = jnp.concatenate(cols, axis=0)
    ot_ref[...] = jnp.where(sel, ocol, ot_ref[...])

    @pl.when(n == pl.num_programs(0) - 1)
    def _():
        for h in range(N_HEADS):
            oh = ot_ref[h * dh:(h + 1) * dh, :]
            ms = jnp.mean(oh * oh, axis=0, keepdims=True)
            o_ref[h * dh:(h + 1) * dh, :] = oh * lax.rsqrt(ms + EPS) * subg_ref[...]


def _attn_sample(pt, cache_kt, cache_vt, page_table, layer, lp):
    n = pt.shape[1]
    n_pages = page_table.shape[1]
    w = W_GROUP
    dh = D_HEAD
    blk = lambda off: pl.BlockSpec((w, n), lambda i, tbl: (off, 0))
    page_specs = [
        pl.BlockSpec((None, None, N_HEADS, dh, PAGE),
                     functools.partial(lambda i, tbl, j: (layer, tbl[i, j], 0, 0, 0), j=j))
        for j in range(n_pages)
    ]
    grid_spec = pltpu.PrefetchScalarGridSpec(
        num_scalar_prefetch=1,
        grid=(n,),
        in_specs=[
            pl.BlockSpec(memory_space=pltpu.SMEM),
            blk(5), blk(6), blk(7),
            pl.BlockSpec((N_HEADS, n_pages, PAGE), lambda i, tbl: (0, 0, 0)),
            pl.BlockSpec((N_HEADS, 1, 1), lambda i, tbl: (0, 0, 0)),
            pl.BlockSpec((dh, 1), lambda i, tbl: (0, 0)),
        ] + page_specs + page_specs,
        out_specs=pl.BlockSpec((w, n), lambda i, tbl: (0, 0)),
        scratch_shapes=[pltpu.VMEM((w, n), F32)],
    )
    return pl.pallas_call(
        functools.partial(_attn_s_kernel, n_pages=n_pages, q_scale=DA_HALF ** -0.5),
        grid_spec=grid_spec,
        out_shape=jax.ShapeDtypeStruct((w, n), F32),
        compiler_params=_params("arbitrary"),
        name="attn_sample",
    )(page_table, lp["lam"], pt, pt, pt, lp["bias_past"], lp["bias_self"], lp["sub_g_col"],
      *([cache_kt] * n_pages), *([cache_vt] * n_pages))


def _rel_bucket(dist):
    n = jnp.maximum(dist, 0)
    max_exact = REL_BUCKETS // 2
    large = max_exact + (jnp.log(jnp.maximum(n, 1).astype(F32) / max_exact)
                         / math.log(REL_MAX_DIST / max_exact) * (REL_BUCKETS - max_exact)).astype(jnp.int32)
    large = jnp.minimum(large, REL_BUCKETS - 1)
    return jnp.where(n < max_exact, n, large)


def _layer_params(l, depth, tq, past_len, n_pages, prm, lb_all):
    w = W_GROUP
    lam_init = 0.8 - 0.6 * math.exp(-0.3 * l)
    lam = (jnp.exp(jnp.sum(prm["diff_lam_q1"][l] * prm["diff_lam_k1"][l]))
           - jnp.exp(jnp.sum(prm["diff_lam_q2"][l] * prm["diff_lam_k2"][l])) + lam_init)
    rel_bias = prm["rel_bias"]
    ii = jnp.arange(tq, dtype=jnp.int32)[:, None]
    jj = jnp.arange(tq, dtype=jnp.int32)[None, :]
    d0 = ii - jj
    bias_d0 = jnp.where((d0 >= 0)[None], jnp.transpose(rel_bias[_rel_bucket(d0)], (2, 0, 1)), NEG)
    bias_d1 = jnp.transpose(rel_bias[_rel_bucket(d0 + tq)], (2, 0, 1))
    kpos = jnp.arange(n_pages * PAGE, dtype=jnp.int32)
    bias_past = jnp.transpose(rel_bias[_rel_bucket(past_len - kpos)], (1, 0)).reshape(N_HEADS, n_pages, PAGE)
    lb = lb_all[l]
    w_in = prm["w_in"][l]
    w_int = w_in.T.astype(BF16)
    w_out = prm["w_out"][l].astype(BF16)
    tril = jnp.tril(jnp.ones((SG_CHUNK, SG_CHUNK), F32))
    sub_g = prm["diff_subln_g"][l] * (1.0 - lam_init)
    return {
        "ffn1_w_gu": prm["ffn1_w_gu"][l].astype(BF16), "ffn1_w_dn": prm["ffn1_w_dn"][l].astype(BF16),
        "ffn2_w_gu": prm["ffn2_w_gu"][l].astype(BF16), "ffn2_w_dn": prm["ffn2_w_dn"][l].astype(BF16),
        "ln_g": prm["ln_g"][l], "ln_b": prm["ln_b"][l],
        "w_in": w_in.astype(BF16), "w_int": w_int, "w_kvt": w_int[6 * w:8 * w],
        "w_out": w_out,
        "w_abd": jnp.concatenate([w_out[0:2 * w], w_out[3 * w:4 * w]], axis=0),
        "w_c": w_out[2 * w:3 * w].reshape(N_HEADS, D_HEAD, -1),
        "pool_w": jax.scipy.linalg.block_diag(*prm["pool_w"][l]).astype(BF16),
        "pool_scale": prm["pool_scale"][l][None],
        "log_lb": jnp.log(lb)[None], "log1m_lb": jnp.log1p(-lb)[None], "om_lb": (1.0 - lb)[None],
        "log_lb_col": jnp.log(lb)[:, None], "log1m_lb_col": jnp.log1p(-lb)[:, None],
        "om_lb_col": (1.0 - lb)[:, None],
        "hg_g": jnp.tile(prm["hgrn_norm_g"][l], N_HEADS)[None], "hg_g_col": prm["hgrn_norm_g"][l][:, None],
        "sg_g": prm["sgu_ln_g"][l][None], "sg_b": prm["sgu_ln_b"][l][None],
        "sg_w": (prm["sgu_w"][l] * tril).reshape(N_HEADS * SG_CHUNK, SG_CHUNK).astype(BF16),
        "sg_bias": jnp.repeat(prm["sgu_b"][l].T, D_HEAD, axis=1),
        "sg_w00": jnp.repeat(prm["sgu_w"][l][:, 0, 0], D_HEAD)[None],
        "sg_b0": jnp.repeat(prm["sgu_b"][l][:, 0], D_HEAD)[None],
        "eones": jnp.kron(jnp.eye(N_HEADS, dtype=F32), jnp.ones((D_HEAD, D_HEAD), F32)).astype(BF16),
        "lam": lam.reshape(1), "c_far": rel_bias[REL_BUCKETS - 1],
        "bias_d0": bias_d0, "bias_d1": bias_d1,
        "bias_past": bias_past, "bias_self": rel_bias[0].reshape(N_HEADS, 1, 1),
        "sub_g": sub_g[None], "sub_g_col": sub_g[:, None],
    }


def kernel(x_prompt, x_sample, state_pool, state_hgrn, cache_k, cache_v, page_table, rel_bias, ln_g, ln_b,
           ffn1_w_gu, ffn1_w_dn, ffn2_w_gu, ffn2_w_dn, w_in, w_out, pool_w, pool_scale, hgrn_lb,
           hgrn_norm_g, diff_lam_q1, diff_lam_k1, diff_lam_q2, diff_lam_k2, diff_subln_g, sgu_ln_g,
           sgu_ln_b, sgu_w, sgu_b):
    prm = dict(rel_bias=rel_bias, ln_g=ln_g, ln_b=ln_b, ffn1_w_gu=ffn1_w_gu, ffn1_w_dn=ffn1_w_dn,
               ffn2_w_gu=ffn2_w_gu, ffn2_w_dn=ffn2_w_dn, w_in=w_in, w_out=w_out, pool_w=pool_w,
               pool_scale=pool_scale, hgrn_norm_g=hgrn_norm_g, diff_lam_q1=diff_lam_q1,
               diff_lam_k1=diff_lam_k1, diff_lam_q2=diff_lam_q2, diff_lam_k2=diff_lam_k2,
               diff_subln_g=diff_subln_g, sgu_ln_g=sgu_ln_g, sgu_ln_b=sgu_ln_b, sgu_w=sgu_w, sgu_b=sgu_b)
    depth = w_in.shape[0]
    nb, t, d = x_prompt.shape
    ns = x_sample.shape[0]
    n_pages = page_table.shape[1]
    past_len = n_pages * PAGE
    alpha = (2.0 * depth) ** 0.25
    w = W_GROUP
    tq = min(ATT_TILE, t)
    assert x_sample.shape[1] == 1 and t % SG_CHUNK == 0 and t % tq == 0

    lb_cum = jnp.cumsum(jax.nn.softmax(hgrn_lb.astype(F32), axis=0), axis=0)
    lb_all = jnp.maximum(lb_cum - lb_cum[:1], 0.0)

    cache_kt = jnp.transpose(cache_k, (0, 1, 3, 4, 2))
    cache_vt = jnp.transpose(cache_v, (0, 1, 3, 4, 2))
    hgrn_t = jnp.transpose(state_hgrn, (0, 2, 3, 4, 1))
    pool_t = jnp.transpose(state_pool, (0, 2, 1, 3))

    xp = x_prompt.reshape(nb * t, d)
    xs = x_sample.reshape(ns, d)
    outs = {k: [] for k in ("pool_p", "pool_s", "hgrn_p", "hgrn_s", "k_p", "k_s", "v_p", "v_s", "sgv_p", "sgv_s")}
    for l in range(depth):
        lp = _layer_params(l, depth, tq, past_len, n_pages, prm, lb_all)
        g = [lp["ln_g"][i][None] for i in range(3)]
        b = [lp["ln_b"][i][None] for i in range(3)]

        xp = _ffn(xp, lp["ffn1_w_gu"], lp["ffn1_w_dn"], g[0], b[0], alpha)
        p, q_h, v_h, kvt = _inproj_prompt(xp, lp["w_in"], lp["w_kvt"], nb)
        y_abd, pool16, st, sgv = _mix_prompt(p, nb, lp)
        y_c = _attn_prompt(q_h, kvt, v_h, lp, nb)
        xp = _outproj_prompt(xp, y_abd, y_c, lp["w_abd"], lp["w_c"], g[1], b[1], alpha)
        xp = _ffn(xp, lp["ffn2_w_gu"], lp["ffn2_w_dn"], g[2], b[2], alpha)
        kvt5 = kvt.reshape(nb, 2, N_HEADS, D_HEAD, t)
        outs["k_p"].append(jnp.transpose(kvt5[:, 0], (0, 3, 1, 2)))
        outs["v_p"].append(jnp.transpose(kvt5[:, 1], (0, 3, 1, 2)))
        outs["pool_p"].append(pool16[:, 1:])
        outs["hgrn_p"].append(jnp.transpose(st.reshape(nb, D_HEAD, N_HEADS, D_HEAD), (0, 2, 3, 1)))
        outs["sgv_p"].append(sgv)

        xs = _ffn(xs, lp["ffn1_w_gu"], lp["ffn1_w_dn"], g[0], b[0], alpha)
        ps, pst = _inproj_sample(xs, lp["w_in"], lp["w_int"])
        y_a, y_d, new_pool, vn = _mix_sample(ps, pool_t[l], lp, past_len)
        new_state, y_bt = _hgrn_sample(pst, hgrn_t[l], lp)
        y_ct = _attn_sample(pst, cache_kt, cache_vt, page_table, l, lp)
        xs = _outproj_sample(xs, y_a, y_bt, y_ct, y_d, lp["w_out"], g[1], b[1], alpha)
        xs = _ffn(xs, lp["ffn2_w_gu"], lp["ffn2_w_dn"], g[2], b[2], alpha)
        outs["k_s"].append(jnp.transpose(pst[6 * w:7 * w].reshape(N_HEADS, D_HEAD, ns), (2, 0, 1))[:, None])
        outs["v_s"].append(jnp.transpose(pst[7 * w:8 * w].reshape(N_HEADS, D_HEAD, ns), (2, 0, 1))[:, None])
        outs["pool_s"].append(jnp.transpose(new_pool, (1, 0, 2)))
        outs["hgrn_s"].append(jnp.transpose(new_state, (3, 0, 1, 2)))
        outs["sgv_s"].append(vn[:, None])

    st = {k: jnp.stack(v, axis=0) for k, v in outs.items()}
    return (xp.reshape(nb, t, d), xs.reshape(ns, 1, d), st["pool_p"], st["pool_s"], st["hgrn_p"], st["hgrn_s"],
            st["k_p"], st["k_s"], st["v_p"], st["v_s"], st["sgv_p"], st["sgv_s"])
```

```python
import functools
import math

import jax
import jax.numpy as jnp
import jax.scipy.linalg
from jax import lax
from jax.experimental import pallas as pl
from jax.experimental.pallas import tpu as pltpu

F32 = jnp.float32
BF16 = jnp.bfloat16

N_MIX = 4
W_GROUP = 256
N_HEADS = 4
D_HEAD = 64
DA_HALF = 32
POOL_WINDOWS = (2, 4, 8, 16)
POOL_BUF = 15
PAGE = 128
SG_CHUNK = 128
HG_SUB = 16
V_AUG = 80
REL_BUCKETS = 32
REL_MAX_DIST = 128
EPS = 1e-5
NEG = -1e30

VMEM_LIMIT = 56 * 1024 * 1024
ROW_TILE = 512
ATT_TQ = 1024
ATT_TK = 512

NT_DIMS = (((1,), (1,)), ((), ()))


def _params(*sem):
    return pltpu.CompilerParams(dimension_semantics=sem, vmem_limit_bytes=VMEM_LIMIT)


def _const_spec(shape, single=False):
    nd = len(shape)
    kw = {"pipeline_mode": pl.Buffered(1)} if single else {}
    return pl.BlockSpec(shape, lambda *_: (0,) * nd, **kw)


def _layer_norm(y, g, b):
    mu = jnp.mean(y, axis=-1, keepdims=True)
    yc = y - mu
    var = jnp.mean(yc * yc, axis=-1, keepdims=True)
    return yc * lax.rsqrt(var + EPS) * g + b


def _silu(x):
    return x * jax.nn.sigmoid(x)


def _log_forget(z, log_lb, log1m_lb):
    log_sig = -(jnp.maximum(-z, 0.0) + jnp.log1p(jnp.exp(-jnp.abs(z))))
    b = log1m_lb + log_sig
    return jnp.maximum(log_lb, b) + jnp.log1p(jnp.exp(-jnp.abs(log_lb - b)))


def _ffn_kernel(x_ref, wgu_ref, wdn_ref, g_ref, b_ref, o_ref, *, d_ff, n_chunks, alpha):
    x = x_ref[...]
    xb = x.astype(BF16)
    tf = d_ff // n_chunks
    acc = None
    for c in range(n_chunks):
        gate = jnp.dot(xb, wgu_ref[:, c * tf:(c + 1) * tf], preferred_element_type=F32)
        up = jnp.dot(xb, wgu_ref[:, d_ff + c * tf:d_ff + (c + 1) * tf], preferred_element_type=F32)
        h = (_silu(gate) * up).astype(BF16)
        part = jnp.dot(h, wdn_ref[c * tf:(c + 1) * tf, :], preferred_element_type=F32)
        acc = part if acc is None else acc + part
    o_ref[...] = _layer_norm(alpha * x + 0.5 * acc, g_ref[...], b_ref[...])


def _ffn(x, w_gu, w_dn, g, b, alpha):
    r, d = x.shape
    d_ff = w_dn.shape[0]
    tm = min(ROW_TILE, r)
    return pl.pallas_call(
        functools.partial(_ffn_kernel, d_ff=d_ff, n_chunks=2, alpha=alpha),
        grid=(r // tm,),
        in_specs=[
            pl.BlockSpec((tm, d), lambda i: (i, 0)),
            _const_spec(w_gu.shape, single=True),
            _const_spec(w_dn.shape, single=True),
            _const_spec((1, d)),
            _const_spec((1, d)),
        ],
        out_specs=pl.BlockSpec((tm, d), lambda i: (i, 0)),
        out_shape=jax.ShapeDtypeStruct((r, d), F32),
        compiler_params=_params("arbitrary"),
        name="ffn",
    )(x, w_gu, w_dn, g, b)


def _inproj_p_kernel(x_ref, w_ref, wqkvt_ref, p_ref, k_ref, qt_ref, kvt_ref, *, q_scale):
    xb = x_ref[...].astype(BF16)
    p = jnp.dot(xb, w_ref[...], preferred_element_type=F32)
    p_ref[...] = p
    for h in range(N_HEADS):
        k_ref[h] = p[:, 6 * W_GROUP + h * D_HEAD:6 * W_GROUP + (h + 1) * D_HEAD].astype(BF16)
    qkvt = lax.dot_general(wqkvt_ref[...], xb, NT_DIMS, preferred_element_type=F32)
    qt_ref[0] = (qkvt[0:W_GROUP] * q_scale).astype(BF16)
    kvt_ref[0] = qkvt[W_GROUP:3 * W_GROUP]


def _inproj_prompt(x, w_in, w_qkvt, n_batch):
    r, d = x.shape
    t = r // n_batch
    d_in = w_in.shape[1]
    tm = min(ROW_TILE, t)
    tpb = t // tm
    return pl.pallas_call(
        functools.partial(_inproj_p_kernel, q_scale=DA_HALF ** -0.5),
        grid=(r // tm,),
        in_specs=[
            pl.BlockSpec((tm, d), lambda i: (i, 0)),
            _const_spec(w_in.shape),
            _const_spec(w_qkvt.shape),
        ],
        out_specs=[
            pl.BlockSpec((tm, d_in), lambda i: (i, 0)),
            pl.BlockSpec((N_HEADS, tm, D_HEAD), lambda i: (0, i, 0)),
            pl.BlockSpec((1, W_GROUP, tm), lambda i: (i // tpb, 0, i % tpb)),
            pl.BlockSpec((1, 2 * W_GROUP, tm), lambda i: (i // tpb, 0, i % tpb)),
        ],
        out_shape=[
            jax.ShapeDtypeStruct((r, d_in), F32),
            jax.ShapeDtypeStruct((N_HEADS, r, D_HEAD), BF16),
            jax.ShapeDtypeStruct((n_batch, W_GROUP, t), BF16),
            jax.ShapeDtypeStruct((n_batch, 2 * W_GROUP, t), F32),
        ],
        compiler_params=_params("arbitrary"),
        name="inproj_prompt",
    )(x, w_in, w_qkvt)


def _inproj_s_kernel(x_ref, w_ref, wt_ref, p_ref, pt_ref):
    xb = x_ref[...].astype(BF16)
    p_ref[...] = jnp.dot(xb, w_ref[...], preferred_element_type=F32)
    pt_ref[...] = lax.dot_general(wt_ref[...], xb, NT_DIMS, preferred_element_type=F32)


def _inproj_sample(x, w_in, w_int):
    n, d = x.shape
    d_in = w_in.shape[1]
    return pl.pallas_call(
        _inproj_s_kernel,
        grid=(1,),
        in_specs=[_const_spec((n, d)), _const_spec(w_in.shape), _const_spec(w_int.shape)],
        out_specs=[_const_spec((n, d_in)), _const_spec((d_in, n))],
        out_shape=[jax.ShapeDtypeStruct((n, d_in), F32), jax.ShapeDtypeStruct((d_in, n), F32)],
        compiler_params=_params("arbitrary"),
        name="inproj_sample",
    )(x, w_in, w_int)


def _group_select(grp, parts):
    out = parts[N_HEADS - 1]
    for g in range(N_HEADS - 2, -1, -1):
        out = jnp.where(grp == g, parts[g], out)
    return out


def _mix_p_kernel(pa_ref, pd_ref, poolw_ref, pscale_ref, loglb_ref, log1mlb_ref, omlb_ref, hgg_ref,
                  sgg_ref, sgb_ref, sgw_ref, sgbias_ref, eones_ref,
                  y_ref, pool_ref, st_out_ref, sgv_ref,
                  prev_ref, st_ref, bl_ref, q_ref, k_ref, v_ref, o_ref):
    c = SG_CHUNK
    w = W_GROUP
    t = pl.program_id(1)

    @pl.when(t == 0)
    def _():
        prev_ref[...] = jnp.zeros_like(prev_ref)
        st_ref[...] = jnp.zeros_like(st_ref)

    a = pa_ref[:, 0:w]
    hq = pa_ref[:, w:2 * w]
    hf = pa_ref[:, 2 * w:3 * w]
    hi = pa_ref[:, 3 * w:4 * w]
    hg = pa_ref[:, 4 * w:5 * w]
    su = pd_ref[:, 0:w]
    sv = pd_ref[:, w:2 * w]
    lane = lax.broadcasted_iota(jnp.int32, (c, w), 1)
    row = lax.broadcasted_iota(jnp.int32, (c, w), 0)
    grp = lane >> 6

    e = jnp.concatenate([prev_ref[...], a], axis=0)
    s2 = e + pltpu.roll(e, 1, 0)
    s4 = s2 + pltpu.roll(s2, 2, 0)
    s8 = s4 + pltpu.roll(s4, 4, 0)
    s16 = s8 + pltpu.roll(s8, 8, 0)
    wsum = _group_select(grp, [s2[16:], s4[16:], s8[16:], s16[16:]])
    win = _group_select(grp, [jnp.full((c, w), v, jnp.int32) for v in POOL_WINDOWS])
    cnt = jnp.minimum(t * c + row + 1, win).astype(F32)
    dpool = wsum / cnt - a
    ya = jnp.dot(dpool.astype(BF16), poolw_ref[...], preferred_element_type=F32) * pscale_ref[...]
    prev_ref[...] = a[c - 16:]
    pool_ref[0] = a[c - 16:]

    vn = _layer_norm(sv, sgg_ref[...], sgb_ref[...])
    sg = jnp.dot(sgw_ref[...], vn.astype(BF16), preferred_element_type=F32)
    s_gate = _group_select(grp, [sg[g * c:(g + 1) * c] for g in range(N_HEADS)]) + sgbias_ref[...]
    yd = su * s_gate
    sgv_ref[0] = vn

    logf = _log_forget(hf, loglb_ref[...], log1mlb_ref[...])
    kin = omlb_ref[...] * jax.nn.sigmoid(-hf)
    r16 = row & (HG_SUB - 1)
    bl = logf
    rv = logf
    for sh in (1, 2, 4, 8):
        bl = bl + jnp.where(r16 >= sh, pltpu.roll(bl, sh, 0), 0.0)
        rv = rv + jnp.where(r16 + sh < HG_SUB, pltpu.roll(rv, c - sh, 0), 0.0)
    qt = (hq * jnp.exp(bl)).astype(BF16)
    kt = kin * jnp.exp(rv - logf)
    bl_ref[...] = bl
    q_ref[...] = hq
    k_ref[...] = kin
    v_ref[...] = hi
    vtb = hi.T.astype(BF16)
    bi0 = lax.broadcasted_iota(jnp.int32, (w, w), 0) >> 6
    bi1 = lax.broadcasted_iota(jnp.int32, (w, w), 1) >> 6
    blockmask = bi0 == bi1
    grp_st = lax.broadcasted_iota(jnp.int32, (D_HEAD, w), 1) >> 6
    st = st_ref[...]
    o_inter = []
    for i in range(c // HG_SUB):
        sb = jnp.where(blockmask, jnp.concatenate([st] * N_HEADS, axis=0), 0.0).astype(BF16)
        o_inter.append(lax.dot_general(qt[i * HG_SUB:(i + 1) * HG_SUB], sb, NT_DIMS,
                                       preferred_element_type=F32))
        km = jnp.where((row >> 4) == i, kt, 0.0).astype(BF16)
        u = jnp.dot(vtb, km, preferred_element_type=F32)
        ut = _group_select(grp_st, [u[g * D_HEAD:(g + 1) * D_HEAD] for g in range(N_HEADS)])
        st = jnp.exp(rv[i * HG_SUB:i * HG_SUB + 1]) * st + ut
    st_ref[...] = st
    st_out_ref[0] = st
    o_ref[...] = jnp.concatenate(o_inter, axis=0)

    def intra(i, carry):
        r0 = pl.multiple_of(i * HG_SUB, HG_SUB)
        bli = bl_ref[pl.ds(r0, HG_SUB), :]
        qi = q_ref[pl.ds(r0, HG_SUB), :]
        ki = k_ref[pl.ds(r0, HG_SUB), :]
        vi = v_ref[pl.ds(r0, HG_SUB), :]
        rt = lax.broadcasted_iota(jnp.int32, (HG_SUB, w), 0)
        xs = []
        for s in range(HG_SUB):
            ratio = jnp.exp(jnp.minimum(bli - bli[s:s + 1], 0.0))
            xs.append(jnp.where(rt >= s, ratio * qi * ki[s:s + 1], 0.0).astype(BF16))
        x = jnp.concatenate(xs, axis=0)
        r = jnp.dot(x, eones_ref[...], preferred_element_type=F32)
        oi = r[0:HG_SUB] * vi[0:1]
        for s in range(1, HG_SUB):
            oi = oi + r[s * HG_SUB:(s + 1) * HG_SUB] * vi[s:s + 1]
        o_ref[pl.ds(r0, HG_SUB), :] = o_ref[pl.ds(r0, HG_SUB), :] + oi
        return carry

    lax.fori_loop(0, c // HG_SUB, intra, 0)

    o = o_ref[...]
    ms = [jnp.mean(jnp.square(o[:, g * D_HEAD:(g + 1) * D_HEAD]), axis=-1, keepdims=True)
          for g in range(N_HEADS)]
    yb = o * lax.rsqrt(_group_select(grp, ms) + EPS) * hgg_ref[...] * _silu(hg)

    y_ref[:, 0:w] = ya.astype(BF16)
    y_ref[:, w:2 * w] = yb.astype(BF16)
    y_ref[:, 2 * w:3 * w] = yd.astype(BF16)


def _mix_prompt(p, n_batch, lp):
    r = p.shape[0]
    t = r // n_batch
    c = SG_CHUNK
    nt = t // c
    w = W_GROUP
    row = lambda: _const_spec((1, w))
    return pl.pallas_call(
        _mix_p_kernel,
        grid=(n_batch, nt),
        in_specs=[
            pl.BlockSpec((c, 5 * w), lambda b, i: (b * nt + i, 0)),
            pl.BlockSpec((c, 2 * w), lambda b, i: (b * nt + i, 4)),
            _const_spec((w, w)), row(), row(), row(), row(), row(), row(), row(),
            _const_spec((N_HEADS * c, c)), _const_spec((c, w)), _const_spec((w, w)),
        ],
        out_specs=[
            pl.BlockSpec((c, 3 * w), lambda b, i: (b * nt + i, 0)),
            pl.BlockSpec((1, 16, w), lambda b, i: (b, 0, 0)),
            pl.BlockSpec((1, D_HEAD, w), lambda b, i: (b, 0, 0)),
            pl.BlockSpec((1, c, w), lambda b, i: (b, 0, 0)),
        ],
        out_shape=[
            jax.ShapeDtypeStruct((r, 3 * w), BF16),
            jax.ShapeDtypeStruct((n_batch, 16, w), F32),
            jax.ShapeDtypeStruct((n_batch, D_HEAD, w), F32),
            jax.ShapeDtypeStruct((n_batch, c, w), F32),
        ],
        scratch_shapes=[
            pltpu.VMEM((16, w), F32), pltpu.VMEM((D_HEAD, w), F32),
            pltpu.VMEM((c, w), F32), pltpu.VMEM((c, w), F32), pltpu.VMEM((c, w), F32),
            pltpu.VMEM((c, w), F32), pltpu.VMEM((c, w), F32),
        ],
        compiler_params=_params("arbitrary", "arbitrary"),
        name="mix_prompt",
    )(p, p, lp["pool_w"], lp["pool_scale"], lp["log_lb"], lp["log1m_lb"], lp["om_lb"], lp["hg_g"],
      lp["sg_g"], lp["sg_b"], lp["sg_w"], lp["sg_bias"], lp["eones"])


def _attn_p_kernel(cfar_ref, lam_ref, qt_ref, k_ref, vt_ref, near_ref, subg_ref, o_ref,
                   va_ref, m_ref, acc_ref, *, tq, tk):
    h = pl.program_id(1)
    qi = pl.program_id(2)
    dh = D_HEAD

    @pl.when(qi == 0)
    def _():
        va_ref[0:dh, :] = vt_ref[0].astype(BF16)
        rows = lax.broadcasted_iota(jnp.int32, (V_AUG - dh, va_ref.shape[1]), 0)
        va_ref[dh:V_AUG, :] = jnp.where(rows == 0, 1.0, 0.0).astype(BF16)

    qt = qt_ref[0]
    sub = lax.broadcasted_iota(jnp.int32, qt.shape, 0)
    zero = jnp.zeros_like(qt)
    qm = (jnp.where(sub < DA_HALF, qt, zero), jnp.where(sub >= DA_HALF, qt, zero))
    m_ref[...] = jnp.full_like(m_ref, NEG)
    acc_ref[...] = jnp.zeros_like(acc_ref)

    def process(ki, near, shift):
        k0 = pl.multiple_of(ki * tk, tk)
        kblk = k_ref[0, pl.ds(k0, tk), :]
        vblk = va_ref[:, pl.ds(k0, tk)]
        for mi in range(2):
            s = jnp.dot(kblk, qm[mi], preferred_element_type=F32)
            if near is not None:
                s = s + near_ref[0, near]
            m_blk = jnp.max(s, axis=0, keepdims=True)
            if shift is not None:
                m_blk = m_blk + shift
            m_old = m_ref[mi]
            m_new = jnp.maximum(m_old, m_blk)
            alpha = jnp.exp(m_old - m_new)
            p = jnp.exp(s - (m_new if shift is None else m_new - shift)).astype(BF16)
            acc_ref[mi] = alpha * acc_ref[mi] + jnp.dot(vblk, p, preferred_element_type=F32)
            m_ref[mi] = m_new

    cfar = cfar_ref[h]

    def far_body(ki, carry):
        process(ki, None, cfar)
        return carry

    kb0 = qi * (tq // tk)
    lax.fori_loop(0, jnp.maximum(kb0 - 1, 0), far_body, 0)

    @pl.when(qi >= 1)
    def _():
        process(kb0 - 1, 0, None)

    for r in range(tq // tk):
        process(kb0 + r, r + 1, None)

    a0 = acc_ref[0]
    a1 = acc_ref[1]
    o = a0[0:dh] / a0[dh:dh + 1] - lam_ref[0] * (a1[0:dh] / a1[dh:dh + 1])
    ms = jnp.mean(o * o, axis=0, keepdims=True)
    o_ref[0] = (o * lax.rsqrt(ms + EPS) * subg_ref[...]).astype(BF16)


def _attn_tiles(t):
    tq = min(ATT_TQ, t)
    return tq, min(ATT_TK, tq)


def _attn_prompt(qt, k_rows, kvt, lp, n_batch):
    t = qt.shape[2]
    tq, tk = _attn_tiles(t)
    nq = t // tq
    n_near = tq // tk + 1
    smem = pl.BlockSpec(memory_space=pltpu.SMEM)
    return pl.pallas_call(
        functools.partial(_attn_p_kernel, tq=tq, tk=tk),
        grid=(n_batch, N_HEADS, nq),
        in_specs=[
            smem, smem,
            pl.BlockSpec((1, D_HEAD, tq), lambda b, h, i: (b, h, i)),
            pl.BlockSpec((1, t, D_HEAD), lambda b, h, i: (h, b, 0)),
            pl.BlockSpec((1, D_HEAD, t), lambda b, h, i: (b, N_HEADS + h, 0)),
            pl.BlockSpec((1, n_near, tk, tq), lambda b, h, i: (h, 0, 0, 0)),
            _const_spec((D_HEAD, 1)),
        ],
        out_specs=pl.BlockSpec((1, D_HEAD, tq), lambda b, h, i: (b, h, i)),
        out_shape=jax.ShapeDtypeStruct((n_batch, W_GROUP, t), BF16),
        scratch_shapes=[
            pltpu.VMEM((V_AUG, t), BF16),
            pltpu.VMEM((2, 1, tq), F32), pltpu.VMEM((2, V_AUG, tq), F32),
        ],
        compiler_params=_params("arbitrary", "arbitrary", "arbitrary"),
        name="attn_prompt",
    )(lp["c_far"], lp["lam"], qt, k_rows, kvt, lp["bias_near"], lp["sub_g_col"])


def _outproj_p_kernel(x_ref, y_ref, yct_ref, wabd_ref, wc_ref, g_ref, b_ref, o_ref, *, alpha):
    acc = jnp.dot(y_ref[...], wabd_ref[...], preferred_element_type=F32)
    yc = yct_ref[0].astype(F32).T.astype(BF16)
    acc = acc + jnp.dot(yc, wc_ref[...], preferred_element_type=F32)
    o_ref[...] = _layer_norm(alpha * x_ref[...] + acc, g_ref[...], b_ref[...])


def _outproj_prompt(x, y_abd, y_ct, w_abd, w_c, g, b, alpha):
    r, d = x.shape
    t = y_ct.shape[2]
    tm = min(ROW_TILE, t)
    tpb = t // tm
    return pl.pallas_call(
        functools.partial(_outproj_p_kernel, alpha=alpha),
        grid=(r // tm,),
        in_specs=[
            pl.BlockSpec((tm, d), lambda i: (i, 0)),
            pl.BlockSpec((tm, 3 * W_GROUP), lambda i: (i, 0)),
            pl.BlockSpec((1, W_GROUP, tm), lambda i: (i // tpb, 0, i % tpb)),
            _const_spec(w_abd.shape), _const_spec(w_c.shape), _const_spec((1, d)), _const_spec((1, d)),
        ],
        out_specs=pl.BlockSpec((tm, d), lambda i: (i, 0)),
        out_shape=jax.ShapeDtypeStruct((r, d), F32),
        compiler_params=_params("arbitrary"),
        name="outproj_prompt",
    )(x, y_abd, y_ct, w_abd, w_c, g, b)


def _outproj_s_kernel(x_ref, ya_ref, ybt_ref, yct_ref, yd_ref, w_ref, g_ref, b_ref, o_ref, *, alpha):
    w = W_GROUP
    parts = (ya_ref[...], ybt_ref[...].T, yct_ref[...].T, yd_ref[...])
    acc = None
    for j, part in enumerate(parts):
        d = jnp.dot(part.astype(BF16), w_ref[j * w:(j + 1) * w, :], preferred_element_type=F32)
        acc = d if acc is None else acc + d
    o_ref[...] = _layer_norm(alpha * x_ref[...] + acc, g_ref[...], b_ref[...])


def _outproj_sample(x, y_a, y_bt, y_ct, y_d, w_out, g, b, alpha):
    n, d = x.shape
    w = W_GROUP
    return pl.pallas_call(
        functools.partial(_outproj_s_kernel, alpha=alpha),
        grid=(1,),
        in_specs=[
            _const_spec((n, d)), _const_spec((n, w)), _const_spec((w, n)), _const_spec((w, n)),
            _const_spec((n, w)), _const_spec(w_out.shape), _const_spec((1, d)), _const_spec((1, d)),
        ],
        out_specs=_const_spec((n, d)),
        out_shape=jax.ShapeDtypeStruct((n, d), F32),
        compiler_params=_params("arbitrary"),
        name="outproj_sample",
    )(x, y_a, y_bt, y_ct, y_d, w_out, g, b)


def _mix_s_kernel(pa_ref, pd_ref, pool_ref, poolw_ref, pscale_ref, sgg_ref, sgb_ref, w00_ref, b0_ref,
                  ya_ref, yd_ref, newpool_ref, vn_ref, *, cnts):
    a = pa_ref[...]
    n, w = a.shape
    grp = lax.broadcasted_iota(jnp.int32, (n, w), 1) >> 6
    acc = a
    means = []
    for j in range(1, POOL_BUF + 1):
        acc = acc + pool_ref[POOL_BUF - j]
        if j + 1 in POOL_WINDOWS:
            means.append(acc / cnts[POOL_WINDOWS.index(j + 1)])
    dpool = _group_select(grp, means) - a
    ya_ref[...] = jnp.dot(dpool.astype(BF16), poolw_ref[...], preferred_element_type=F32) * pscale_ref[...]
    for j in range(POOL_BUF - 1):
        newpool_ref[j] = pool_ref[j + 1]
    newpool_ref[POOL_BUF - 1] = a

    vn = _layer_norm(pd_ref[:, w:2 * w], sgg_ref[...], sgb_ref[...])
    vn_ref[...] = vn
    yd_ref[...] = pd_ref[:, 0:w] * (w00_ref[...] * vn + b0_ref[...])


def _mix_sample(p, pool, lp, past_len):
    n = p.shape[0]
    w = W_GROUP
    cnts = tuple(float(min(past_len + 1, win)) for win in POOL_WINDOWS)
    row = lambda: _const_spec((1, w))
    return pl.pallas_call(
        functools.partial(_mix_s_kernel, cnts=cnts),
        grid=(1,),
        in_specs=[
            pl.BlockSpec((n, w), lambda i: (0, 0)),
            pl.BlockSpec((n, 2 * w), lambda i: (0, 4)),
            _const_spec(pool.shape), _const_spec((w, w)), row(), row(), row(), row(), row(),
        ],
        out_specs=[_const_spec((n, w)), _const_spec((n, w)), _const_spec(pool.shape), _const_spec((n, w))],
        out_shape=[
            jax.ShapeDtypeStruct((n, w), F32), jax.ShapeDtypeStruct((n, w), F32),
            jax.ShapeDtypeStruct(pool.shape, F32), jax.ShapeDtypeStruct((n, w), F32),
        ],
        compiler_params=_params("arbitrary"),
        name="mix_sample",
    )(p, p, pool, lp["pool_w"], lp["pool_scale"], lp["sg_g"], lp["sg_b"], lp["sg_w00"], lp["sg_b0"])


def _hgrn_s_kernel(q_ref, f_ref, i_ref, g_ref, s_ref, loglb_ref, log1mlb_ref, omlb_ref, hgg_ref,
                   so_ref, yb_ref):
    z = f_ref[...]
    f = jnp.exp(_log_forget(z, loglb_ref[...], log1mlb_ref[...]))
    kin = omlb_ref[...] * jax.nn.sigmoid(-z)
    q = q_ref[...]
    v = i_ref[...]
    o = jnp.zeros_like(v)
    for d in range(D_HEAD):
        sn = f[d:d + 1] * s_ref[0, d] + kin[d:d + 1] * v
        so_ref[0, d] = sn
        o = o + q[d:d + 1] * sn
    ms = jnp.mean(o * o, axis=0, keepdims=True)
    yb_ref[...] = o * lax.rsqrt(ms + EPS) * hgg_ref[...] * _silu(g_ref[...])


def _hgrn_sample(pt, state, lp):
    n = pt.shape[1]
    dh = D_HEAD
    blk = lambda off: pl.BlockSpec((dh, n), lambda h: (off * N_HEADS + h, 0))
    col = lambda: pl.BlockSpec((dh, 1), lambda h: (h, 0))
    return pl.pallas_call(
        _hgrn_s_kernel,
        grid=(N_HEADS,),
        in_specs=[
            blk(1), blk(2), blk(3), blk(4),
            pl.BlockSpec((1, dh, dh, n), lambda h: (h, 0, 0, 0)),
            col(), col(), col(), _const_spec((dh, 1)),
        ],
        out_specs=[
            pl.BlockSpec((1, dh, dh, n), lambda h: (h, 0, 0, 0)),
            pl.BlockSpec((dh, n), lambda h: (h, 0)),
        ],
        out_shape=[
            jax.ShapeDtypeStruct(state.shape, F32),
            jax.ShapeDtypeStruct((W_GROUP, n), F32),
        ],
        compiler_params=_params("arbitrary"),
        name="hgrn_sample",
    )(pt, pt, pt, pt, state, lp["log_lb_col"], lp["log1m_lb_col"], lp["om_lb_col"], lp["hg_g_col"])


def _attn_s_kernel(pt_ref, lam_ref, qt_ref, knt_ref, vnt_ref, bias_ref, bself_ref, subg_ref, *rest,
                   n_pages, q_scale):
    k_pages = rest[:n_pages]
    v_pages = rest[n_pages:2 * n_pages]
    o_ref = rest[2 * n_pages]
    ot_ref = rest[2 * n_pages + 1]
    del pt_ref
    n = pl.program_id(0)
    w, ns = qt_ref.shape
    dh = D_HEAD
    sel = lax.broadcasted_iota(jnp.int32, (w, ns), 1) == n

    def column(ref):
        return jnp.sum(jnp.where(sel, ref[...], 0.0), axis=1, keepdims=True)

    qcol = column(qt_ref) * q_scale
    kcol = column(knt_ref)
    vcol = column(vnt_ref)
    lam = lam_ref[0]

    def full_max(x):
        return jnp.max(jnp.max(x, axis=1, keepdims=True), axis=0, keepdims=True)

    def full_sum(x):
        return jnp.sum(jnp.sum(x, axis=1, keepdims=True), axis=0, keepdims=True)

    cols = []
    for h in range(N_HEADS):
        qh = qcol[h * dh:(h + 1) * dh]
        kh = kcol[h * dh:(h + 1) * dh]
        rows = ([], [])
        for j in range(n_pages):
            prod = k_pages[j][h] * qh
            rows[0].append(jnp.sum(prod[0:DA_HALF], axis=0, keepdims=True))
            rows[1].append(jnp.sum(prod[DA_HALF:dh], axis=0, keepdims=True))
        self_prod = qh * kh
        probs, probs_self = [], []
        for mi in range(2):
            s = jnp.concatenate(rows[mi], axis=0) + bias_ref[h]
            s_self = (jnp.sum(self_prod[mi * DA_HALF:(mi + 1) * DA_HALF], axis=0, keepdims=True)
                      + bself_ref[h])
            m = jnp.maximum(full_max(s), s_self)
            p = jnp.exp(s - m)
            p_self = jnp.exp(s_self - m)
            l = full_sum(p) + p_self
            probs.append(p / l)
            probs_self.append(p_self / l)
        a = probs[0] - lam * probs[1]
        a_self = probs_self[0] - lam * probs_self[1]
        oh = v_pages[0][h] * a[0:1]
        for j in range(1, n_pages):
            oh = oh + v_pages[j][h] * a[j:j + 1]
        cols.append(jnp.sum(oh, axis=1, keepdims=True) + a_self * vcol[h * dh:(h + 1) * dh])
    ocol = jnp.concatenate(cols, axis=0)
    ot_ref[...] = jnp.where(sel, ocol, ot_ref[...])

    @pl.when(n == pl.num_programs(0) - 1)
    def _():
        for h in range(N_HEADS):
            oh = ot_ref[h * dh:(h + 1) * dh, :]
            ms = jnp.mean(oh * oh, axis=0, keepdims=True)
            o_ref[h * dh:(h + 1) * dh, :] = oh * lax.rsqrt(ms + EPS) * subg_ref[...]


def _attn_sample(pt, cache_kt, cache_vt, page_table, layer, lp):
    n = pt.shape[1]
    n_pages = page_table.shape[1]
    w = W_GROUP
    dh = D_HEAD
    blk = lambda off: pl.BlockSpec((w, n), lambda i, tbl: (off, 0))
    page_specs = [
        pl.BlockSpec((None, None, N_HEADS, dh, PAGE),
                     functools.partial(lambda i, tbl, j: (layer, tbl[i, j], 0, 0, 0), j=j))
        for j in range(n_pages)
    ]
    grid_spec = pltpu.PrefetchScalarGridSpec(
        num_scalar_prefetch=1,
        grid=(n,),
        in_specs=[
            pl.BlockSpec(memory_space=pltpu.SMEM),
            blk(5), blk(6), blk(7),
            pl.BlockSpec((N_HEADS, n_pages, PAGE), lambda i, tbl: (0, 0, 0)),
            pl.BlockSpec((N_HEADS, 1, 1), lambda i, tbl: (0, 0, 0)),
            pl.BlockSpec((dh, 1), lambda i, tbl: (0, 0)),
        ] + page_specs + page_specs,
        out_specs=pl.BlockSpec((w, n), lambda i, tbl: (0, 0)),
        scratch_shapes=[pltpu.VMEM((w, n), F32)],
    )
    return pl.pallas_call(
        functools.partial(_attn_s_kernel, n_pages=n_pages, q_scale=DA_HALF ** -0.5),
        grid_spec=grid_spec,
        out_shape=jax.ShapeDtypeStruct((w, n), F32),
        compiler_params=_params("arbitrary"),
        name="attn_sample",
    )(page_table, lp["lam"], pt, pt, pt, lp["bias_past"], lp["bias_self"], lp["sub_g_col"],
      *([cache_kt] * n_pages), *([cache_vt] * n_pages))


def _rel_bucket(dist):
    n = jnp.maximum(dist, 0)
    max_exact = REL_BUCKETS // 2
    large = max_exact + (jnp.log(jnp.maximum(n, 1).astype(F32) / max_exact)
                         / math.log(REL_MAX_DIST / max_exact) * (REL_BUCKETS - max_exact)).astype(jnp.int32)
    large = jnp.minimum(large, REL_BUCKETS - 1)
    return jnp.where(n < max_exact, n, large)


def _layer_params(l, depth, tq, tk, past_len, n_pages, prm, lb_all):
    w = W_GROUP
    lam_init = 0.8 - 0.6 * math.exp(-0.3 * l)
    lam = (jnp.exp(jnp.sum(prm["diff_lam_q1"][l] * prm["diff_lam_k1"][l]))
           - jnp.exp(jnp.sum(prm["diff_lam_q2"][l] * prm["diff_lam_k2"][l])) + lam_init)
    rel_bias = prm["rel_bias"]

    def bias_of(dist):
        onehot = _rel_bucket(dist)[..., None] == jnp.arange(REL_BUCKETS, dtype=jnp.int32)
        table = rel_bias.T.reshape((N_HEADS,) + (1,) * dist.ndim + (REL_BUCKETS,))
        return jnp.sum(jnp.where(onehot[None], table, 0.0), axis=-1)

    rr = jnp.arange(-1, tq // tk, dtype=jnp.int32)[:, None, None]
    kk = jnp.arange(tk, dtype=jnp.int32)[None, :, None]
    qq = jnp.arange(tq, dtype=jnp.int32)[None, None, :]
    dnear = qq - kk - rr * tk
    bias_near = jnp.where((dnear >= 0)[None], bias_of(dnear), NEG)
    kpos = jnp.arange(n_pages * PAGE, dtype=jnp.int32)
    bias_past = bias_of(past_len - kpos).reshape(N_HEADS, n_pages, PAGE)
    lb = lb_all[l]
    w_in = prm["w_in"][l]
    w_int = w_in.T.astype(BF16)
    w_out = prm["w_out"][l].astype(BF16)
    tril = jnp.tril(jnp.ones((SG_CHUNK, SG_CHUNK), F32))
    sub_g = prm["diff_subln_g"][l] * (1.0 - lam_init)
    return {
        "ffn1_w_gu": prm["ffn1_w_gu"][l].astype(BF16), "ffn1_w_dn": prm["ffn1_w_dn"][l].astype(BF16),
        "ffn2_w_gu": prm["ffn2_w_gu"][l].astype(BF16), "ffn2_w_dn": prm["ffn2_w_dn"][l].astype(BF16),
        "ln_g": prm["ln_g"][l], "ln_b": prm["ln_b"][l],
        "w_in": w_in.astype(BF16), "w_int": w_int, "w_qkvt": w_int[5 * w:8 * w],
        "w_out": w_out,
        "w_abd": jnp.concatenate([w_out[0:2 * w], w_out[3 * w:4 * w]], axis=0),
        "w_c": w_out[2 * w:3 * w],
        "pool_w": jax.scipy.linalg.block_diag(*prm["pool_w"][l]).astype(BF16),
        "pool_scale": prm["pool_scale"][l][None],
        "log_lb": jnp.log(lb)[None], "log1m_lb": jnp.log1p(-lb)[None], "om_lb": (1.0 - lb)[None],
        "log_lb_col": jnp.log(lb)[:, None], "log1m_lb_col": jnp.log1p(-lb)[:, None],
        "om_lb_col": (1.0 - lb)[:, None],
        "hg_g": jnp.tile(prm["hgrn_norm_g"][l], N_HEADS)[None], "hg_g_col": prm["hgrn_norm_g"][l][:, None],
        "sg_g": prm["sgu_ln_g"][l][None], "sg_b": prm["sgu_ln_b"][l][None],
        "sg_w": (prm["sgu_w"][l] * tril).reshape(N_HEADS * SG_CHUNK, SG_CHUNK).astype(BF16),
        "sg_bias": jnp.repeat(prm["sgu_b"][l].T, D_HEAD, axis=1),
        "sg_w00": jnp.repeat(prm["sgu_w"][l][:, 0, 0], D_HEAD)[None],
        "sg_b0": jnp.repeat(prm["sgu_b"][l][:, 0], D_HEAD)[None],
        "eones": jnp.kron(jnp.eye(N_HEADS, dtype=F32), jnp.ones((D_HEAD, D_HEAD), F32)).astype(BF16),
        "lam": lam.reshape(1), "c_far": rel_bias[REL_BUCKETS - 1],
        "bias_near": bias_near,
        "bias_past": bias_past, "bias_self": rel_bias[0].reshape(N_HEADS, 1, 1),
        "sub_g_col": sub_g[:, None],
    }


def kernel(x_prompt, x_sample, state_pool, state_hgrn, cache_k, cache_v, page_table, rel_bias, ln_g, ln_b,
           ffn1_w_gu, ffn1_w_dn, ffn2_w_gu, ffn2_w_dn, w_in, w_out, pool_w, pool_scale, hgrn_lb,
           hgrn_norm_g, diff_lam_q1, diff_lam_k1, diff_lam_q2, diff_lam_k2, diff_subln_g, sgu_ln_g,
           sgu_ln_b, sgu_w, sgu_b):
    prm = dict(rel_bias=rel_bias, ln_g=ln_g, ln_b=ln_b, ffn1_w_gu=ffn1_w_gu, ffn1_w_dn=ffn1_w_dn,
               ffn2_w_gu=ffn2_w_gu, ffn2_w_dn=ffn2_w_dn, w_in=w_in, w_out=w_out, pool_w=pool_w,
               pool_scale=pool_scale, hgrn_norm_g=hgrn_norm_g, diff_lam_q1=diff_lam_q1,
               diff_lam_k1=diff_lam_k1, diff_lam_q2=diff_lam_q2, diff_lam_k2=diff_lam_k2,
               diff_subln_g=diff_subln_g, sgu_ln_g=sgu_ln_g, sgu_ln_b=sgu_ln_b, sgu_w=sgu_w, sgu_b=sgu_b)
    depth = w_in.shape[0]
    nb, t, d = x_prompt.shape
    ns = x_sample.shape[0]
    n_pages = page_table.shape[1]
    past_len = n_pages * PAGE
    alpha = (2.0 * depth) ** 0.25
    w = W_GROUP
    tq, tk = _attn_tiles(t)
    assert x_sample.shape[1] == 1 and t % SG_CHUNK == 0 and t % tq == 0 and tk >= REL_MAX_DIST

    lb_cum = jnp.cumsum(jax.nn.softmax(hgrn_lb.astype(F32), axis=0), axis=0)
    lb_all = jnp.maximum(lb_cum - lb_cum[:1], 0.0)

    cache_kt = jnp.transpose(cache_k, (0, 1, 3, 4, 2))
    cache_vt = jnp.transpose(cache_v, (0, 1, 3, 4, 2))
    hgrn_t = jnp.transpose(state_hgrn, (0, 2, 3, 4, 1))
    pool_t = jnp.transpose(state_pool, (0, 2, 1, 3))

    xp = x_prompt.reshape(nb * t, d)
    xs = x_sample.reshape(ns, d)
    outs = {k: [] for k in ("pool_p", "pool_s", "hgrn_p", "hgrn_s", "k_p", "k_s", "v_p", "v_s", "sgv_p", "sgv_s")}
    for l in range(depth):
        lp = _layer_params(l, depth, tq, tk, past_len, n_pages, prm, lb_all)
        g = [lp["ln_g"][i][None] for i in range(3)]
        b = [lp["ln_b"][i][None] for i in range(3)]

        xp = _ffn(xp, lp["ffn1_w_gu"], lp["ffn1_w_dn"], g[0], b[0], alpha)
        p, k_rows, q_t, kvt = _inproj_prompt(xp, lp["w_in"], lp["w_qkvt"], nb)
        y_abd, pool16, st, sgv = _mix_prompt(p, nb, lp)
        y_ct = _attn_prompt(q_t, k_rows, kvt, lp, nb)
        xp = _outproj_prompt(xp, y_abd, y_ct, lp["w_abd"], lp["w_c"], g[1], b[1], alpha)
        xp = _ffn(xp, lp["ffn2_w_gu"], lp["ffn2_w_dn"], g[2], b[2], alpha)
        kvt5 = kvt.reshape(nb, 2, N_HEADS, D_HEAD, t)
        outs["k_p"].append(jnp.transpose(kvt5[:, 0], (0, 3, 1, 2)))
        outs["v_p"].append(jnp.transpose(kvt5[:, 1], (0, 3, 1, 2)))
        outs["pool_p"].append(pool16[:, 1:])
        outs["hgrn_p"].append(jnp.transpose(st.reshape(nb, D_HEAD, N_HEADS, D_HEAD), (0, 2, 3, 1)))
        outs["sgv_p"].append(sgv)

        xs = _ffn(xs, lp["ffn1_w_gu"], lp["ffn1_w_dn"], g[0], b[0], alpha)
        ps, pst = _inproj_sample(xs, lp["w_in"], lp["w_int"])
        y_a, y_d, new_pool, vn = _mix_sample(ps, pool_t[l], lp, past_len)
        new_state, y_bt = _hgrn_sample(pst, hgrn_t[l], lp)
        y_ct = _attn_sample(pst, cache_kt, cache_vt, page_table, l, lp)
        xs = _outproj_sample(xs, y_a, y_bt, y_ct, y_d, lp["w_out"], g[1], b[1], alpha)
        xs = _ffn(xs, lp["ffn2_w_gu"], lp["ffn2_w_dn"], g[2], b[2], alpha)
        outs["k_s"].append(jnp.transpose(pst[6 * w:7 * w].reshape(N_HEADS, D_HEAD, ns), (2, 0, 1))[:, None])
        outs["v_s"].append(jnp.transpose(pst[7 * w:8 * w].reshape(N_HEADS, D_HEAD, ns), (2, 0, 1))[:, None])
        outs["pool_s"].append(jnp.transpose(new_pool, (1, 0, 2)))
        outs["hgrn_s"].append(jnp.transpose(new_state, (3, 0, 1, 2)))
        outs["sgv_s"].append(vn[:, None])

    st = {k: jnp.stack(v, axis=0) for k, v in outs.items()}
    return (xp.reshape(nb, t, d), xs.reshape(ns, 1, d), st["pool_p"], st["pool_s"], st["hgrn_p"], st["hgrn_s"],
            st["k_p"], st["k_s"], st["v_p"], st["v_s"], st["sgv_p"], st["sgv_s"])
```

```python
import functools
import math

import jax
import jax.numpy as jnp
import jax.scipy.linalg
from jax import lax
from jax.experimental import pallas as pl
from jax.experimental.pallas import tpu as pltpu

F32 = jnp.float32
BF16 = jnp.bfloat16

N_MIX = 4
W_GROUP = 256
N_HEADS = 4
D_HEAD = 64
DA_HALF = 32
POOL_WINDOWS = (2, 4, 8, 16)
POOL_BUF = 15
PAGE = 128
SG_CHUNK = 128
HG_SUB = 16
V_AUG = 80
REL_BUCKETS = 32
REL_MAX_DIST = 128
EPS = 1e-5
NEG = -1e30

VMEM_LIMIT = 56 * 1024 * 1024
MXU_DIM = 256
ROW_TILE = 512
ATT_TQ = 1024
ATT_TK = 512

NT_DIMS = (((1,), (1,)), ((), ()))


def _params(*sem):
    return pltpu.CompilerParams(dimension_semantics=sem, vmem_limit_bytes=VMEM_LIMIT)


def _const_spec(shape, single=False):
    nd = len(shape)
    kw = {"pipeline_mode": pl.Buffered(1)} if single else {}
    return pl.BlockSpec(shape, lambda *_: (0,) * nd, **kw)


def _layer_norm(y, g, b):
    mu = jnp.mean(y, axis=-1, keepdims=True)
    yc = y - mu
    var = jnp.mean(yc * yc, axis=-1, keepdims=True)
    return yc * lax.rsqrt(var + EPS) * g + b


def _silu(x):
    return x * jax.nn.sigmoid(x)


def _log_forget(z, log_lb, log1m_lb):
    log_sig = -(jnp.maximum(-z, 0.0) + jnp.log1p(jnp.exp(-jnp.abs(z))))
    b = log1m_lb + log_sig
    return jnp.maximum(log_lb, b) + jnp.log1p(jnp.exp(-jnp.abs(log_lb - b)))


def _ffn_kernel(x_ref, wgu_ref, wdn_ref, g_ref, b_ref, o_ref, *, d_ff, bounds, alpha):
    x = x_ref[...]
    xb = x.astype(BF16)
    acc = None
    for lo, hi in zip(bounds[:-1], bounds[1:]):
        gate = jnp.dot(xb, wgu_ref[:, lo:hi], preferred_element_type=F32)
        up = jnp.dot(xb, wgu_ref[:, d_ff + lo:d_ff + hi], preferred_element_type=F32)
        h = (_silu(gate) * up).astype(BF16)
        part = jnp.dot(h, wdn_ref[lo:hi, :], preferred_element_type=F32)
        acc = part if acc is None else acc + part
    o_ref[...] = _layer_norm(alpha * x + 0.5 * acc, g_ref[...], b_ref[...])


def _ffn(x, w_gu, w_dn, g, b, alpha):
    r, d = x.shape
    d_ff = w_dn.shape[0]
    tm = min(ROW_TILE, r)
    n_tiles = d_ff // MXU_DIM
    bounds = (0, (n_tiles + 1) // 2 * MXU_DIM, d_ff)
    return pl.pallas_call(
        functools.partial(_ffn_kernel, d_ff=d_ff, bounds=bounds, alpha=alpha),
        grid=(r // tm,),
        in_specs=[
            pl.BlockSpec((tm, d), lambda i: (i, 0)),
            _const_spec(w_gu.shape, single=True),
            _const_spec(w_dn.shape, single=True),
            _const_spec((1, d)),
            _const_spec((1, d)),
        ],
        out_specs=pl.BlockSpec((tm, d), lambda i: (i, 0)),
        out_shape=jax.ShapeDtypeStruct((r, d), F32),
        compiler_params=_params("arbitrary"),
        name="ffn",
    )(x, w_gu, w_dn, g, b)


def _inproj_p_kernel(x_ref, w_ref, wqkvt_ref, p_ref, k_ref, qt_ref, kvt_ref, *, q_scale):
    xb = x_ref[...].astype(BF16)
    p = jnp.dot(xb, w_ref[...], preferred_element_type=F32)
    p_ref[...] = p
    for h in range(N_HEADS):
        k_ref[h] = p[:, 6 * W_GROUP + h * D_HEAD:6 * W_GROUP + (h + 1) * D_HEAD].astype(BF16)
    qkvt = lax.dot_general(wqkvt_ref[...], xb, NT_DIMS, preferred_element_type=F32)
    qt_ref[0] = (qkvt[0:W_GROUP] * q_scale).astype(BF16)
    kvt_ref[0] = qkvt[W_GROUP:3 * W_GROUP]


def _inproj_prompt(x, w_in, w_qkvt, n_batch):
    r, d = x.shape
    t = r // n_batch
    d_in = w_in.shape[1]
    tm = min(ROW_TILE, t)
    tpb = t // tm
    return pl.pallas_call(
        functools.partial(_inproj_p_kernel, q_scale=DA_HALF ** -0.5),
        grid=(r // tm,),
        in_specs=[
            pl.BlockSpec((tm, d), lambda i: (i, 0)),
            _const_spec(w_in.shape),
            _const_spec(w_qkvt.shape),
        ],
        out_specs=[
            pl.BlockSpec((tm, d_in), lambda i: (i, 0)),
            pl.BlockSpec((N_HEADS, tm, D_HEAD), lambda i: (0, i, 0)),
            pl.BlockSpec((1, W_GROUP, tm), lambda i: (i // tpb, 0, i % tpb)),
            pl.BlockSpec((1, 2 * W_GROUP, tm), lambda i: (i // tpb, 0, i % tpb)),
        ],
        out_shape=[
            jax.ShapeDtypeStruct((r, d_in), F32),
            jax.ShapeDtypeStruct((N_HEADS, r, D_HEAD), BF16),
            jax.ShapeDtypeStruct((n_batch, W_GROUP, t), BF16),
            jax.ShapeDtypeStruct((n_batch, 2 * W_GROUP, t), F32),
        ],
        compiler_params=_params("arbitrary"),
        name="inproj_prompt",
    )(x, w_in, w_qkvt)


def _inproj_s_kernel(x_ref, w_ref, wt_ref, p_ref, pt_ref):
    xb = x_ref[...].astype(BF16)
    p_ref[...] = jnp.dot(xb, w_ref[...], preferred_element_type=F32)
    pt_ref[...] = lax.dot_general(wt_ref[...], xb, NT_DIMS, preferred_element_type=F32)


def _inproj_sample(x, w_in, w_int):
    n, d = x.shape
    d_in = w_in.shape[1]
    return pl.pallas_call(
        _inproj_s_kernel,
        grid=(1,),
        in_specs=[_const_spec((n, d)), _const_spec(w_in.shape), _const_spec(w_int.shape)],
        out_specs=[_const_spec((n, d_in)), _const_spec((d_in, n))],
        out_shape=[jax.ShapeDtypeStruct((n, d_in), F32), jax.ShapeDtypeStruct((d_in, n), F32)],
        compiler_params=_params("arbitrary"),
        name="inproj_sample",
    )(x, w_in, w_int)


def _group_select(grp, parts):
    out = parts[N_HEADS - 1]
    for g in range(N_HEADS - 2, -1, -1):
        out = jnp.where(grp == g, parts[g], out)
    return out


def _mix_p_kernel(pa_ref, pd_ref, poolw_ref, pscale_ref, loglb_ref, log1mlb_ref, omlb_ref, hgg_ref,
                  sgg_ref, sgb_ref, sgw_ref, sgbias_ref, eones_ref,
                  y_ref, pool_ref, st_out_ref, sgv_ref,
                  prev_ref, st_ref):
    c = SG_CHUNK
    w = W_GROUP
    t = pl.program_id(1)

    @pl.when(t == 0)
    def _():
        prev_ref[...] = jnp.zeros_like(prev_ref)
        st_ref[...] = jnp.zeros_like(st_ref)

    a = pa_ref[:, 0:w]
    hq = pa_ref[:, w:2 * w]
    hf = pa_ref[:, 2 * w:3 * w]
    hi = pa_ref[:, 3 * w:4 * w]
    hg = pa_ref[:, 4 * w:5 * w]
    su = pd_ref[:, 0:w]
    sv = pd_ref[:, w:2 * w]
    lane = lax.broadcasted_iota(jnp.int32, (c, w), 1)
    row = lax.broadcasted_iota(jnp.int32, (c, w), 0)
    grp = lane >> 6

    e = jnp.concatenate([prev_ref[...], a], axis=0)
    s2 = e + pltpu.roll(e, 1, 0)
    s4 = s2 + pltpu.roll(s2, 2, 0)
    s8 = s4 + pltpu.roll(s4, 4, 0)
    s16 = s8 + pltpu.roll(s8, 8, 0)
    wsum = _group_select(grp, [s2[16:], s4[16:], s8[16:], s16[16:]])
    win = _group_select(grp, [jnp.full((c, w), v, jnp.int32) for v in POOL_WINDOWS])
    cnt = jnp.minimum(t * c + row + 1, win).astype(F32)
    dpool = wsum / cnt - a
    ya = jnp.dot(dpool.astype(BF16), poolw_ref[...], preferred_element_type=F32) * pscale_ref[...]
    prev_ref[...] = a[c - 16:]
    pool_ref[0] = a[c - 16:]

    vn = _layer_norm(sv, sgg_ref[...], sgb_ref[...])
    sg = jnp.dot(sgw_ref[...], vn.astype(BF16), preferred_element_type=F32)
    s_gate = _group_select(grp, [sg[g * c:(g + 1) * c] for g in range(N_HEADS)]) + sgbias_ref[...]
    yd = su * s_gate
    sgv_ref[0] = vn

    n_sub = c // HG_SUB
    logf = _log_forget(hf, loglb_ref[...], log1mlb_ref[...])
    kin = omlb_ref[...] * jax.nn.sigmoid(-hf)
    r16 = row & (HG_SUB - 1)
    bl = logf
    rv = logf
    for sh in (1, 2, 4, 8):
        bl = bl + jnp.where(r16 >= sh, pltpu.roll(bl, sh, 0), 0.0)
        rv = rv + jnp.where(r16 + sh < HG_SUB, pltpu.roll(rv, c - sh, 0), 0.0)
    sub_row = row >> 4
    qtb = (hq * jnp.exp(bl)).astype(BF16)
    kt = kin * jnp.exp(rv - logf)
    dec = jnp.exp(rv)
    vtb = hi.T.astype(BF16)
    bi0 = lax.broadcasted_iota(jnp.int32, (w, w), 0) >> 6
    bi1 = lax.broadcasted_iota(jnp.int32, (w, w), 1) >> 6
    blockmask = bi0 == bi1
    st = st_ref[...]
    seen = []
    for i in range(n_sub):
        seen.append(st.astype(BF16))
        km = jnp.where(sub_row == i, kt, 0.0).astype(BF16)
        u = jnp.dot(vtb, km, preferred_element_type=F32)
        st = dec[i * HG_SUB:i * HG_SUB + 1] * st + jnp.where(blockmask, u, 0.0)
    st_ref[...] = st
    st_out_ref[0] = st
    qx = jnp.concatenate([jnp.where(sub_row == i, qtb, jnp.zeros_like(qtb)) for i in range(n_sub)], axis=1)
    o = lax.dot_general(qx, jnp.concatenate(seen, axis=1), NT_DIMS,
                        preferred_element_type=F32)

    rt = lax.broadcasted_iota(jnp.int32, (HG_SUB, w), 0)
    xs = []
    for i in range(n_sub):
        rows = slice(i * HG_SUB, (i + 1) * HG_SUB)
        bli, qi, ki = bl[rows], hq[rows], kin[rows]
        for s in range(HG_SUB):
            ratio = jnp.exp(jnp.minimum(bli - bli[s:s + 1], 0.0))
            xs.append(jnp.where(rt >= s, ratio * qi * ki[s:s + 1], 0.0).astype(BF16))
    x = jnp.concatenate(xs, axis=0)
    r = jnp.dot(x, eones_ref[...], preferred_element_type=F32)
    o_intra = []
    for i in range(n_sub):
        vi = hi[i * HG_SUB:(i + 1) * HG_SUB]
        base = i * HG_SUB * HG_SUB
        oi = r[base:base + HG_SUB] * vi[0:1]
        for s in range(1, HG_SUB):
            oi = oi + r[base + s * HG_SUB:base + (s + 1) * HG_SUB] * vi[s:s + 1]
        o_intra.append(oi)
    o = o + jnp.concatenate(o_intra, axis=0)
    ms = [jnp.mean(jnp.square(o[:, g * D_HEAD:(g + 1) * D_HEAD]), axis=-1, keepdims=True)
          for g in range(N_HEADS)]
    yb = o * lax.rsqrt(_group_select(grp, ms) + EPS) * hgg_ref[...] * _silu(hg)

    y_ref[:, 0:w] = ya.astype(BF16)
    y_ref[:, w:2 * w] = yb.astype(BF16)
    y_ref[:, 2 * w:3 * w] = yd.astype(BF16)


def _mix_prompt(p, n_batch, lp):
    r = p.shape[0]
    t = r // n_batch
    c = SG_CHUNK
    nt = t // c
    w = W_GROUP
    row = lambda: _const_spec((1, w))
    return pl.pallas_call(
        _mix_p_kernel,
        grid=(n_batch, nt),
        in_specs=[
            pl.BlockSpec((c, 5 * w), lambda b, i: (b * nt + i, 0)),
            pl.BlockSpec((c, 2 * w), lambda b, i: (b * nt + i, 4)),
            _const_spec((w, w)), row(), row(), row(), row(), row(), row(), row(),
            _const_spec((N_HEADS * c, c)), _const_spec((c, w)), _const_spec((w, w)),
        ],
        out_specs=[
            pl.BlockSpec((c, 3 * w), lambda b, i: (b * nt + i, 0)),
            pl.BlockSpec((1, 16, w), lambda b, i: (b, 0, 0)),
            pl.BlockSpec((1, w, w), lambda b, i: (b, 0, 0)),
            pl.BlockSpec((1, c, w), lambda b, i: (b, 0, 0)),
        ],
        out_shape=[
            jax.ShapeDtypeStruct((r, 3 * w), BF16),
            jax.ShapeDtypeStruct((n_batch, 16, w), F32),
            jax.ShapeDtypeStruct((n_batch, w, w), F32),
            jax.ShapeDtypeStruct((n_batch, c, w), F32),
        ],
        scratch_shapes=[pltpu.VMEM((16, w), F32), pltpu.VMEM((w, w), F32)],
        compiler_params=_params("arbitrary", "arbitrary"),
        name="mix_prompt",
    )(p, p, lp["pool_w"], lp["pool_scale"], lp["log_lb"], lp["log1m_lb"], lp["om_lb"], lp["hg_g"],
      lp["sg_g"], lp["sg_b"], lp["sg_w"], lp["sg_bias"], lp["eones"])


def _attn_p_kernel(cfar_ref, lam_ref, qt_ref, k_ref, vt_ref, near_ref, subg_ref, o_ref,
                   va_ref, m_ref, acc_ref, *, tq, tk):
    h = pl.program_id(1)
    qi = pl.program_id(2)
    dh = D_HEAD

    @pl.when(qi == 0)
    def _():
        va_ref[0:dh, :] = vt_ref[0].astype(BF16)
        rows = lax.broadcasted_iota(jnp.int32, (V_AUG - dh, va_ref.shape[1]), 0)
        va_ref[dh:V_AUG, :] = jnp.where(rows == 0, 1.0, 0.0).astype(BF16)

    qt = qt_ref[0]
    sub = lax.broadcasted_iota(jnp.int32, qt.shape, 0)
    zero = jnp.zeros_like(qt)
    qm = (jnp.where(sub < DA_HALF, qt, zero), jnp.where(sub >= DA_HALF, qt, zero))
    m_ref[...] = jnp.full_like(m_ref, NEG)
    acc_ref[...] = jnp.zeros_like(acc_ref)

    def process(ki, near, shift):
        k0 = pl.multiple_of(ki * tk, tk)
        kblk = k_ref[0, pl.ds(k0, tk), :]
        vblk = va_ref[:, pl.ds(k0, tk)]
        for mi in range(2):
            s = jnp.dot(kblk, qm[mi], preferred_element_type=F32)
            if near is not None:
                s = s + near_ref[0, near]
            m_blk = jnp.max(s, axis=0, keepdims=True)
            if shift is not None:
                m_blk = m_blk + shift
            m_old = m_ref[mi]
            m_new = jnp.maximum(m_old, m_blk)
            alpha = jnp.exp(m_old - m_new)
            p = jnp.exp(s - (m_new if shift is None else m_new - shift)).astype(BF16)
            acc_ref[mi] = alpha * acc_ref[mi] + jnp.dot(vblk, p, preferred_element_type=F32)
            m_ref[mi] = m_new

    cfar = cfar_ref[h]

    def far_body(ki, carry):
        process(ki, None, cfar)
        return carry

    kb0 = qi * (tq // tk)
    lax.fori_loop(0, jnp.maximum(kb0 - 1, 0), far_body, 0)

    @pl.when(qi >= 1)
    def _():
        process(kb0 - 1, 0, None)

    for r in range(tq // tk):
        process(kb0 + r, r + 1, None)

    a0 = acc_ref[0]
    a1 = acc_ref[1]
    o = a0[0:dh] / a0[dh:dh + 1] - lam_ref[0] * (a1[0:dh] / a1[dh:dh + 1])
    ms = jnp.mean(o * o, axis=0, keepdims=True)
    o_ref[0] = (o * lax.rsqrt(ms + EPS) * subg_ref[...]).astype(BF16)


def _attn_tiles(t):
    tq = min(ATT_TQ, t)
    return tq, min(ATT_TK, tq)


def _attn_prompt(qt, k_rows, kvt, lp, n_batch):
    t = qt.shape[2]
    tq, tk = _attn_tiles(t)
    nq = t // tq
    n_near = tq // tk + 1
    smem = pl.BlockSpec(memory_space=pltpu.SMEM)
    return pl.pallas_call(
        functools.partial(_attn_p_kernel, tq=tq, tk=tk),
        grid=(n_batch, N_HEADS, nq),
        in_specs=[
            smem, smem,
            pl.BlockSpec((1, D_HEAD, tq), lambda b, h, i: (b, h, i)),
            pl.BlockSpec((1, t, D_HEAD), lambda b, h, i: (h, b, 0)),
            pl.BlockSpec((1, D_HEAD, t), lambda b, h, i: (b, N_HEADS + h, 0)),
            pl.BlockSpec((1, n_near, tk, tq), lambda b, h, i: (h, 0, 0, 0)),
            _const_spec((D_HEAD, 1)),
        ],
        out_specs=pl.BlockSpec((1, D_HEAD, tq), lambda b, h, i: (b, h, i)),
        out_shape=jax.ShapeDtypeStruct((n_batch, W_GROUP, t), BF16),
        scratch_shapes=[
            pltpu.VMEM((V_AUG, t), BF16),
            pltpu.VMEM((2, 1, tq), F32), pltpu.VMEM((2, V_AUG, tq), F32),
        ],
        compiler_params=_params("arbitrary", "arbitrary", "arbitrary"),
        name="attn_prompt",
    )(lp["c_far"], lp["lam"], qt, k_rows, kvt, lp["bias_near"], lp["sub_g_col"])


def _outproj_p_kernel(x_ref, y_ref, yct_ref, wabd_ref, wc_ref, g_ref, b_ref, o_ref, *, alpha):
    acc = jnp.dot(y_ref[...], wabd_ref[...], preferred_element_type=F32)
    yc = yct_ref[0].astype(F32).T.astype(BF16)
    acc = acc + jnp.dot(yc, wc_ref[...], preferred_element_type=F32)
    o_ref[...] = _layer_norm(alpha * x_ref[...] + acc, g_ref[...], b_ref[...])


def _outproj_prompt(x, y_abd, y_ct, w_abd, w_c, g, b, alpha):
    r, d = x.shape
    t = y_ct.shape[2]
    tm = min(ROW_TILE, t)
    tpb = t // tm
    return pl.pallas_call(
        functools.partial(_outproj_p_kernel, alpha=alpha),
        grid=(r // tm,),
        in_specs=[
            pl.BlockSpec((tm, d), lambda i: (i, 0)),
            pl.BlockSpec((tm, 3 * W_GROUP), lambda i: (i, 0)),
            pl.BlockSpec((1, W_GROUP, tm), lambda i: (i // tpb, 0, i % tpb)),
            _const_spec(w_abd.shape), _const_spec(w_c.shape), _const_spec((1, d)), _const_spec((1, d)),
        ],
        out_specs=pl.BlockSpec((tm, d), lambda i: (i, 0)),
        out_shape=jax.ShapeDtypeStruct((r, d), F32),
        compiler_params=_params("arbitrary"),
        name="outproj_prompt",
    )(x, y_abd, y_ct, w_abd, w_c, g, b)


def _outproj_s_kernel(x_ref, ya_ref, ybt_ref, yct_ref, yd_ref, w_ref, g_ref, b_ref, o_ref, *, alpha):
    w = W_GROUP
    parts = (ya_ref[...], ybt_ref[...].T, yct_ref[...].T, yd_ref[...])
    acc = None
    for j, part in enumerate(parts):
        d = jnp.dot(part.astype(BF16), w_ref[j * w:(j + 1) * w, :], preferred_element_type=F32)
        acc = d if acc is None else acc + d
    o_ref[...] = _layer_norm(alpha * x_ref[...] + acc, g_ref[...], b_ref[...])


def _outproj_sample(x, y_a, y_bt, y_ct, y_d, w_out, g, b, alpha):
    n, d = x.shape
    w = W_GROUP
    return pl.pallas_call(
        functools.partial(_outproj_s_kernel, alpha=alpha),
        grid=(1,),
        in_specs=[
            _const_spec((n, d)), _const_spec((n, w)), _const_spec((w, n)), _const_spec((w, n)),
            _const_spec((n, w)), _const_spec(w_out.shape), _const_spec((1, d)), _const_spec((1, d)),
        ],
        out_specs=_const_spec((n, d)),
        out_shape=jax.ShapeDtypeStruct((n, d), F32),
        compiler_params=_params("arbitrary"),
        name="outproj_sample",
    )(x, y_a, y_bt, y_ct, y_d, w_out, g, b)


def _mix_s_kernel(pa_ref, pd_ref, pool_ref, poolw_ref, pscale_ref, sgg_ref, sgb_ref, w00_ref, b0_ref,
                  ya_ref, yd_ref, newpool_ref, vn_ref, *, cnts):
    a = pa_ref[...]
    n, w = a.shape
    grp = lax.broadcasted_iota(jnp.int32, (n, w), 1) >> 6
    acc = a
    means = []
    for j in range(1, POOL_BUF + 1):
        acc = acc + pool_ref[POOL_BUF - j]
        if j + 1 in POOL_WINDOWS:
            means.append(acc / cnts[POOL_WINDOWS.index(j + 1)])
    dpool = _group_select(grp, means) - a
    ya_ref[...] = jnp.dot(dpool.astype(BF16), poolw_ref[...], preferred_element_type=F32) * pscale_ref[...]
    for j in range(POOL_BUF - 1):
        newpool_ref[j] = pool_ref[j + 1]
    newpool_ref[POOL_BUF - 1] = a

    vn = _layer_norm(pd_ref[:, w:2 * w], sgg_ref[...], sgb_ref[...])
    vn_ref[...] = vn
    yd_ref[...] = pd_ref[:, 0:w] * (w00_ref[...] * vn + b0_ref[...])


def _mix_sample(p, pool, lp, past_len):
    n = p.shape[0]
    w = W_GROUP
    cnts = tuple(float(min(past_len + 1, win)) for win in POOL_WINDOWS)
    row = lambda: _const_spec((1, w))
    return pl.pallas_call(
        functools.partial(_mix_s_kernel, cnts=cnts),
        grid=(1,),
        in_specs=[
            pl.BlockSpec((n, w), lambda i: (0, 0)),
            pl.BlockSpec((n, 2 * w), lambda i: (0, 4)),
            _const_spec(pool.shape), _const_spec((w, w)), row(), row(), row(), row(), row(),
        ],
        out_specs=[_const_spec((n, w)), _const_spec((n, w)), _const_spec(pool.shape), _const_spec((n, w))],
        out_shape=[
            jax.ShapeDtypeStruct((n, w), F32), jax.ShapeDtypeStruct((n, w), F32),
            jax.ShapeDtypeStruct(pool.shape, F32), jax.ShapeDtypeStruct((n, w), F32),
        ],
        compiler_params=_params("arbitrary"),
        name="mix_sample",
    )(p, p, pool, lp["pool_w"], lp["pool_scale"], lp["sg_g"], lp["sg_b"], lp["sg_w00"], lp["sg_b0"])


def _hgrn_s_kernel(q_ref, f_ref, i_ref, g_ref, s_ref, loglb_ref, log1mlb_ref, omlb_ref, hgg_ref,
                   so_ref, yb_ref):
    z = f_ref[...]
    f = jnp.exp(_log_forget(z, loglb_ref[...], log1mlb_ref[...]))
    kin = omlb_ref[...] * jax.nn.sigmoid(-z)
    q = q_ref[...]
    v = i_ref[...]
    o = jnp.zeros_like(v)
    for d in range(D_HEAD):
        sn = f[d:d + 1] * s_ref[0, d] + kin[d:d + 1] * v
        so_ref[0, d] = sn
        o = o + q[d:d + 1] * sn
    ms = jnp.mean(o * o, axis=0, keepdims=True)
    yb_ref[...] = o * lax.rsqrt(ms + EPS) * hgg_ref[...] * _silu(g_ref[...])


def _hgrn_sample(pt, state, lp):
    n = pt.shape[1]
    dh = D_HEAD
    blk = lambda off: pl.BlockSpec((dh, n), lambda h: (off * N_HEADS + h, 0))
    col = lambda: pl.BlockSpec((dh, 1), lambda h: (h, 0))
    return pl.pallas_call(
        _hgrn_s_kernel,
        grid=(N_HEADS,),
        in_specs=[
            blk(1), blk(2), blk(3), blk(4),
            pl.BlockSpec((1, dh, dh, n), lambda h: (h, 0, 0, 0)),
            col(), col(), col(), _const_spec((dh, 1)),
        ],
        out_specs=[
            pl.BlockSpec((1, dh, dh, n), lambda h: (h, 0, 0, 0)),
            pl.BlockSpec((dh, n), lambda h: (h, 0)),
        ],
        out_shape=[
            jax.ShapeDtypeStruct(state.shape, F32),
            jax.ShapeDtypeStruct((W_GROUP, n), F32),
        ],
        compiler_params=_params("arbitrary"),
        name="hgrn_sample",
    )(pt, pt, pt, pt, state, lp["log_lb_col"], lp["log1m_lb_col"], lp["om_lb_col"], lp["hg_g_col"])


def _attn_s_kernel(pt_ref, lam_ref, qt_ref, knt_ref, vnt_ref, bias_ref, bself_ref, subg_ref, *rest,
                   n_pages, q_scale):
    k_pages = rest[:n_pages]
    v_pages = rest[n_pages:2 * n_pages]
    o_ref = rest[2 * n_pages]
    ot_ref = rest[2 * n_pages + 1]
    del pt_ref
    n = pl.program_id(0)
    w, ns = qt_ref.shape
    dh = D_HEAD
    sel = lax.broadcasted_iota(jnp.int32, (w, ns), 1) == n

    def column(ref):
        return jnp.sum(jnp.where(sel, ref[...], 0.0), axis=1, keepdims=True)

    qcol = column(qt_ref) * q_scale
    kcol = column(knt_ref)
    vcol = column(vnt_ref)
    lam = lam_ref[0]

    def full_max(x):
        return jnp.max(jnp.max(x, axis=1, keepdims=True), axis=0, keepdims=True)

    def full_sum(x):
        return jnp.sum(jnp.sum(x, axis=1, keepdims=True), axis=0, keepdims=True)

    cols = []
    for h in range(N_HEADS):
        qh = qcol[h * dh:(h + 1) * dh]
        kh = kcol[h * dh:(h + 1) * dh]
        rows = ([], [])
        for j in range(n_pages):
            prod = k_pages[j][h] * qh
            rows[0].append(jnp.sum(prod[0:DA_HALF], axis=0, keepdims=True))
            rows[1].append(jnp.sum(prod[DA_HALF:dh], axis=0, keepdims=True))
        self_prod = qh * kh
        probs, probs_self = [], []
        for mi in range(2):
            s = jnp.concatenate(rows[mi], axis=0) + bias_ref[h]
            s_self = (jnp.sum(self_prod[mi * DA_HALF:(mi + 1) * DA_HALF], axis=0, keepdims=True)
                      + bself_ref[h])
            m = jnp.maximum(full_max(s), s_self)
            p = jnp.exp(s - m)
            p_self = jnp.exp(s_self - m)
            l = full_sum(p) + p_self
            probs.append(p / l)
            probs_self.append(p_self / l)
        a = probs[0] - lam * probs[1]
        a_self = probs_self[0] - lam * probs_self[1]
        oh = v_pages[0][h] * a[0:1]
        for j in range(1, n_pages):
            oh = oh + v_pages[j][h] * a[j:j + 1]
        cols.append(jnp.sum(oh, axis=1, keepdims=True) + a_self * vcol[h * dh:(h + 1) * dh])
    ocol = jnp.concatenate(cols, axis=0)
    ot_ref[...] = jnp.where(sel, ocol, ot_ref[...])

    @pl.when(n == pl.num_programs(0) - 1)
    def _():
        for h in range(N_HEADS):
            oh = ot_ref[h * dh:(h + 1) * dh, :]
            ms = jnp.mean(oh * oh, axis=0, keepdims=True)
            o_ref[h * dh:(h + 1) * dh, :] = oh * lax.rsqrt(ms + EPS) * subg_ref[...]


def _attn_sample(pt, cache_kt, cache_vt, page_table, layer, lp):
    n = pt.shape[1]
    n_pages = page_table.shape[1]
    w = W_GROUP
    dh = D_HEAD
    blk = lambda off: pl.BlockSpec((w, n), lambda i, tbl: (off, 0))
    page_specs = [
        pl.BlockSpec((None, None, N_HEADS, dh, PAGE),
                     functools.partial(lambda i, tbl, j: (layer, tbl[i, j], 0, 0, 0), j=j))
        for j in range(n_pages)
    ]
    grid_spec = pltpu.PrefetchScalarGridSpec(
        num_scalar_prefetch=1,
        grid=(n,),
        in_specs=[
            pl.BlockSpec(memory_space=pltpu.SMEM),
            blk(5), blk(6), blk(7),
            pl.BlockSpec((N_HEADS, n_pages, PAGE), lambda i, tbl: (0, 0, 0)),
            pl.BlockSpec((N_HEADS, 1, 1), lambda i, tbl: (0, 0, 0)),
            pl.BlockSpec((dh, 1), lambda i, tbl: (0, 0)),
        ] + page_specs + page_specs,
        out_specs=pl.BlockSpec((w, n), lambda i, tbl: (0, 0)),
        scratch_shapes=[pltpu.VMEM((w, n), F32)],
    )
    return pl.pallas_call(
        functools.partial(_attn_s_kernel, n_pages=n_pages, q_scale=DA_HALF ** -0.5),
        grid_spec=grid_spec,
        out_shape=jax.ShapeDtypeStruct((w, n), F32),
        compiler_params=_params("arbitrary"),
        name="attn_sample",
    )(page_table, lp["lam"], pt, pt, pt, lp["bias_past"], lp["bias_self"], lp["sub_g_col"],
      *([cache_kt] * n_pages), *([cache_vt] * n_pages))


def _rel_bucket(dist):
    n = jnp.maximum(dist, 0)
    max_exact = REL_BUCKETS // 2
    large = max_exact + (jnp.log(jnp.maximum(n, 1).astype(F32) / max_exact)
                         / math.log(REL_MAX_DIST / max_exact) * (REL_BUCKETS - max_exact)).astype(jnp.int32)
    large = jnp.minimum(large, REL_BUCKETS - 1)
    return jnp.where(n < max_exact, n, large)


def _layer_params(l, depth, tq, tk, past_len, n_pages, prm, lb_all):
    w = W_GROUP
    lam_init = 0.8 - 0.6 * math.exp(-0.3 * l)
    lam = (jnp.exp(jnp.sum(prm["diff_lam_q1"][l] * prm["diff_lam_k1"][l]))
           - jnp.exp(jnp.sum(prm["diff_lam_q2"][l] * prm["diff_lam_k2"][l])) + lam_init)
    rel_bias = prm["rel_bias"]

    def bias_of(dist):
        onehot = _rel_bucket(dist)[..., None] == jnp.arange(REL_BUCKETS, dtype=jnp.int32)
        table = rel_bias.T.reshape((N_HEADS,) + (1,) * dist.ndim + (REL_BUCKETS,))
        return jnp.sum(jnp.where(onehot[None], table, 0.0), axis=-1)

    span = tq + tk - 1
    jj = jnp.arange(span + 1, dtype=jnp.int32)
    tiles = []
    for r in range(-1, tq // tk):
        dist = jj - (tk - 1) - r * tk
        f = jnp.where((dist >= 0)[None], bias_of(dist), NEG)
        sheared = jnp.tile(f, (1, tk))[:, :tk * span].reshape(N_HEADS, tk, span)
        tiles.append(sheared[:, :, tk - 1:tk - 1 + tq])
    bias_near = jnp.stack(tiles, axis=1)
    kpos = jnp.arange(n_pages * PAGE, dtype=jnp.int32)
    bias_past = bias_of(past_len - kpos).reshape(N_HEADS, n_pages, PAGE)
    lb = lb_all[l]
    w_in = prm["w_in"][l]
    w_int = w_in.T.astype(BF16)
    w_out = prm["w_out"][l].astype(BF16)
    tril = jnp.tril(jnp.ones((SG_CHUNK, SG_CHUNK), F32))
    sub_g = prm["diff_subln_g"][l] * (1.0 - lam_init)
    return {
        "ffn1_w_gu": prm["ffn1_w_gu"][l].astype(BF16), "ffn1_w_dn": prm["ffn1_w_dn"][l].astype(BF16),
        "ffn2_w_gu": prm["ffn2_w_gu"][l].astype(BF16), "ffn2_w_dn": prm["ffn2_w_dn"][l].astype(BF16),
        "ln_g": prm["ln_g"][l], "ln_b": prm["ln_b"][l],
        "w_in": w_in.astype(BF16), "w_int": w_int, "w_qkvt": w_int[5 * w:8 * w],
        "w_out": w_out,
        "w_abd": jnp.concatenate([w_out[0:2 * w], w_out[3 * w:4 * w]], axis=0),
        "w_c": w_out[2 * w:3 * w],
        "pool_w": jax.scipy.linalg.block_diag(*prm["pool_w"][l]).astype(BF16),
        "pool_scale": prm["pool_scale"][l][None],
        "log_lb": jnp.log(lb)[None], "log1m_lb": jnp.log1p(-lb)[None], "om_lb": (1.0 - lb)[None],
        "log_lb_col": jnp.log(lb)[:, None], "log1m_lb_col": jnp.log1p(-lb)[:, None],
        "om_lb_col": (1.0 - lb)[:, None],
        "hg_g": jnp.tile(prm["hgrn_norm_g"][l], N_HEADS)[None], "hg_g_col": prm["hgrn_norm_g"][l][:, None],
        "sg_g": prm["sgu_ln_g"][l][None], "sg_b": prm["sgu_ln_b"][l][None],
        "sg_w": (prm["sgu_w"][l] * tril).reshape(N_HEADS * SG_CHUNK, SG_CHUNK).astype(BF16),
        "sg_bias": jnp.repeat(prm["sgu_b"][l].T, D_HEAD, axis=1),
        "sg_w00": jnp.repeat(prm["sgu_w"][l][:, 0, 0], D_HEAD)[None],
        "sg_b0": jnp.repeat(prm["sgu_b"][l][:, 0], D_HEAD)[None],
        "eones": jnp.kron(jnp.eye(N_HEADS, dtype=F32), jnp.ones((D_HEAD, D_HEAD), F32)).astype(BF16),
        "lam": lam.reshape(1), "c_far": rel_bias[REL_BUCKETS - 1],
        "bias_near": bias_near,
        "bias_past": bias_past, "bias_self": rel_bias[0].reshape(N_HEADS, 1, 1),
        "sub_g_col": sub_g[:, None],
    }


def kernel(x_prompt, x_sample, state_pool, state_hgrn, cache_k, cache_v, page_table, rel_bias, ln_g, ln_b,
           ffn1_w_gu, ffn1_w_dn, ffn2_w_gu, ffn2_w_dn, w_in, w_out, pool_w, pool_scale, hgrn_lb,
           hgrn_norm_g, diff_lam_q1, diff_lam_k1, diff_lam_q2, diff_lam_k2, diff_subln_g, sgu_ln_g,
           sgu_ln_b, sgu_w, sgu_b):
    prm = dict(rel_bias=rel_bias, ln_g=ln_g, ln_b=ln_b, ffn1_w_gu=ffn1_w_gu, ffn1_w_dn=ffn1_w_dn,
               ffn2_w_gu=ffn2_w_gu, ffn2_w_dn=ffn2_w_dn, w_in=w_in, w_out=w_out, pool_w=pool_w,
               pool_scale=pool_scale, hgrn_norm_g=hgrn_norm_g, diff_lam_q1=diff_lam_q1,
               diff_lam_k1=diff_lam_k1, diff_lam_q2=diff_lam_q2, diff_lam_k2=diff_lam_k2,
               diff_subln_g=diff_subln_g, sgu_ln_g=sgu_ln_g, sgu_ln_b=sgu_ln_b, sgu_w=sgu_w, sgu_b=sgu_b)
    depth = w_in.shape[0]
    nb, t, d = x_prompt.shape
    ns = x_sample.shape[0]
    n_pages = page_table.shape[1]
    past_len = n_pages * PAGE
    alpha = (2.0 * depth) ** 0.25
    w = W_GROUP
    tq, tk = _attn_tiles(t)
    assert x_sample.shape[1] == 1 and t % SG_CHUNK == 0 and t % tq == 0 and tk >= REL_MAX_DIST

    lb_cum = jnp.cumsum(jax.nn.softmax(hgrn_lb.astype(F32), axis=0), axis=0)
    lb_all = jnp.maximum(lb_cum - lb_cum[:1], 0.0)

    cache_kt = jnp.transpose(cache_k, (0, 1, 3, 4, 2))
    cache_vt = jnp.transpose(cache_v, (0, 1, 3, 4, 2))
    hgrn_t = jnp.transpose(state_hgrn, (0, 2, 3, 4, 1))
    pool_t = jnp.transpose(state_pool, (0, 2, 1, 3))

    xp = x_prompt.reshape(nb * t, d)
    xs = x_sample.reshape(ns, d)
    outs = {k: [] for k in ("pool_p", "pool_s", "hgrn_p", "hgrn_s", "k_p", "k_s", "v_p", "v_s", "sgv_p", "sgv_s")}
    for l in range(depth):
        lp = _layer_params(l, depth, tq, tk, past_len, n_pages, prm, lb_all)
        g = [lp["ln_g"][i][None] for i in range(3)]
        b = [lp["ln_b"][i][None] for i in range(3)]

        xp = _ffn(xp, lp["ffn1_w_gu"], lp["ffn1_w_dn"], g[0], b[0], alpha)
        p, k_rows, q_t, kvt = _inproj_prompt(xp, lp["w_in"], lp["w_qkvt"], nb)
        y_abd, pool16, st, sgv = _mix_prompt(p, nb, lp)
        y_ct = _attn_prompt(q_t, k_rows, kvt, lp, nb)
        xp = _outproj_prompt(xp, y_abd, y_ct, lp["w_abd"], lp["w_c"], g[1], b[1], alpha)
        xp = _ffn(xp, lp["ffn2_w_gu"], lp["ffn2_w_dn"], g[2], b[2], alpha)
        kvt5 = kvt.reshape(nb, 2, N_HEADS, D_HEAD, t)
        outs["k_p"].append(jnp.transpose(kvt5[:, 0], (0, 3, 1, 2)))
        outs["v_p"].append(jnp.transpose(kvt5[:, 1], (0, 3, 1, 2)))
        outs["pool_p"].append(pool16[:, 1:])
        outs["hgrn_p"].append(jnp.stack(
            [jnp.swapaxes(st[:, h * D_HEAD:(h + 1) * D_HEAD, h * D_HEAD:(h + 1) * D_HEAD], 1, 2)
             for h in range(N_HEADS)], axis=1))
        outs["sgv_p"].append(sgv)

        xs = _ffn(xs, lp["ffn1_w_gu"], lp["ffn1_w_dn"], g[0], b[0], alpha)
        ps, pst = _inproj_sample(xs, lp["w_in"], lp["w_int"])
        y_a, y_d, new_pool, vn = _mix_sample(ps, pool_t[l], lp, past_len)
        new_state, y_bt = _hgrn_sample(pst, hgrn_t[l], lp)
        y_ct = _attn_sample(pst, cache_kt, cache_vt, page_table, l, lp)
        xs = _outproj_sample(xs, y_a, y_bt, y_ct, y_d, lp["w_out"], g[1], b[1], alpha)
        xs = _ffn(xs, lp["ffn2_w_gu"], lp["ffn2_w_dn"], g[2], b[2], alpha)
        outs["k_s"].append(jnp.transpose(pst[6 * w:7 * w].reshape(N_HEADS, D_HEAD, ns), (2, 0, 1))[:, None])
        outs["v_s"].append(jnp.transpose(pst[7 * w:8 * w].reshape(N_HEADS, D_HEAD, ns), (2, 0, 1))[:, None])
        outs["pool_s"].append(jnp.transpose(new_pool, (1, 0, 2)))
        outs["hgrn_s"].append(jnp.transpose(new_state, (3, 0, 1, 2)))
        outs["sgv_s"].append(vn[:, None])

    st = {k: jnp.stack(v, axis=0) for k, v in outs.items()}
    return (xp.reshape(nb, t, d), xs.reshape(ns, 1, d), st["pool_p"], st["pool_s"], st["hgrn_p"], st["hgrn_s"],
            st["k_p"], st["k_s"], st["v_p"], st["v_s"], st["sgv_p"], st["sgv_s"])
```

```python
import functools
import math

import jax
import jax.numpy as jnp
import jax.scipy.linalg
from jax import lax
from jax.experimental import pallas as pl
from jax.experimental.pallas import tpu as pltpu

F32 = jnp.float32
BF16 = jnp.bfloat16

N_MIX = 4
W_GROUP = 256
N_HEADS = 4
D_HEAD = 64
DA_HALF = 32
POOL_WINDOWS = (2, 4, 8, 16)
POOL_BUF = 15
PAGE = 128
SG_CHUNK = 128
HG_SUB = 16
V_AUG = 80
REL_BUCKETS = 32
REL_MAX_DIST = 128
EPS = 1e-5
NEG = -1e30

VMEM_LIMIT = 56 * 1024 * 1024
MXU_DIM = 256
ROW_TILE = 512
ATT_TQ = 1024
ATT_TK = 512
DEC_SAMPLES_PER_STEP = 2

NT_DIMS = (((1,), (1,)), ((), ()))


def _params(*sem):
    return pltpu.CompilerParams(dimension_semantics=sem, vmem_limit_bytes=VMEM_LIMIT)


def _const_spec(shape, single=False):
    nd = len(shape)
    kw = {"pipeline_mode": pl.Buffered(1)} if single else {}
    return pl.BlockSpec(shape, lambda *_: (0,) * nd, **kw)


def _layer_norm(y, g, b):
    mu = jnp.mean(y, axis=-1, keepdims=True)
    yc = y - mu
    var = jnp.mean(yc * yc, axis=-1, keepdims=True)
    return yc * lax.rsqrt(var + EPS) * g + b


def _silu(x):
    return x * jax.nn.sigmoid(x)


def _log_forget(z, log_lb, log1m_lb):
    log_sig = -(jnp.maximum(-z, 0.0) + jnp.log1p(jnp.exp(-jnp.abs(z))))
    b = log1m_lb + log_sig
    return jnp.maximum(log_lb, b) + jnp.log1p(jnp.exp(-jnp.abs(log_lb - b)))


def _ffn_kernel(x_ref, wgu_ref, wdn_ref, g_ref, b_ref, o_ref, *, d_ff, bounds, alpha):
    x = x_ref[...]
    xb = x.astype(BF16)
    acc = None
    for lo, hi in zip(bounds[:-1], bounds[1:]):
        gate = jnp.dot(xb, wgu_ref[:, lo:hi], preferred_element_type=F32)
        up = jnp.dot(xb, wgu_ref[:, d_ff + lo:d_ff + hi], preferred_element_type=F32)
        h = (_silu(gate) * up).astype(BF16)
        part = jnp.dot(h, wdn_ref[lo:hi, :], preferred_element_type=F32)
        acc = part if acc is None else acc + part
    o_ref[...] = _layer_norm(alpha * x + 0.5 * acc, g_ref[...], b_ref[...])


def _ffn(x, w_gu, w_dn, g, b, alpha):
    r, d = x.shape
    d_ff = w_dn.shape[0]
    tm = min(ROW_TILE, r)
    n_tiles = d_ff // MXU_DIM
    bounds = (0, (n_tiles + 1) // 2 * MXU_DIM, d_ff)
    return pl.pallas_call(
        functools.partial(_ffn_kernel, d_ff=d_ff, bounds=bounds, alpha=alpha),
        grid=(r // tm,),
        in_specs=[
            pl.BlockSpec((tm, d), lambda i: (i, 0)),
            _const_spec(w_gu.shape, single=True),
            _const_spec(w_dn.shape, single=True),
            _const_spec((1, d)),
            _const_spec((1, d)),
        ],
        out_specs=pl.BlockSpec((tm, d), lambda i: (i, 0)),
        out_shape=jax.ShapeDtypeStruct((r, d), F32),
        compiler_params=_params("arbitrary"),
        name="ffn",
    )(x, w_gu, w_dn, g, b)


def _inproj_p_kernel(x_ref, w_ref, wqkvt_ref, p_ref, k_ref, qt_ref, kvt_ref, *, q_scale):
    xb = x_ref[...].astype(BF16)
    p = jnp.dot(xb, w_ref[...], preferred_element_type=F32)
    p_ref[...] = p
    for h in range(N_HEADS):
        k_ref[h] = p[:, 6 * W_GROUP + h * D_HEAD:6 * W_GROUP + (h + 1) * D_HEAD].astype(BF16)
    qkvt = lax.dot_general(wqkvt_ref[...], xb, NT_DIMS, preferred_element_type=F32)
    qt_ref[0] = (qkvt[0:W_GROUP] * q_scale).astype(BF16)
    kvt_ref[0] = qkvt[W_GROUP:3 * W_GROUP]


def _inproj_prompt(x, w_in, w_qkvt, n_batch):
    r, d = x.shape
    t = r // n_batch
    d_in = w_in.shape[1]
    tm = min(ROW_TILE, t)
    tpb = t // tm
    return pl.pallas_call(
        functools.partial(_inproj_p_kernel, q_scale=DA_HALF ** -0.5),
        grid=(r // tm,),
        in_specs=[
            pl.BlockSpec((tm, d), lambda i: (i, 0)),
            _const_spec(w_in.shape),
            _const_spec(w_qkvt.shape),
        ],
        out_specs=[
            pl.BlockSpec((tm, d_in), lambda i: (i, 0)),
            pl.BlockSpec((N_HEADS, tm, D_HEAD), lambda i: (0, i, 0)),
            pl.BlockSpec((1, W_GROUP, tm), lambda i: (i // tpb, 0, i % tpb)),
            pl.BlockSpec((1, 2 * W_GROUP, tm), lambda i: (i // tpb, 0, i % tpb)),
        ],
        out_shape=[
            jax.ShapeDtypeStruct((r, d_in), F32),
            jax.ShapeDtypeStruct((N_HEADS, r, D_HEAD), BF16),
            jax.ShapeDtypeStruct((n_batch, W_GROUP, t), BF16),
            jax.ShapeDtypeStruct((n_batch, 2 * W_GROUP, t), F32),
        ],
        compiler_params=_params("arbitrary"),
        name="inproj_prompt",
    )(x, w_in, w_qkvt)


def _inproj_s_kernel(x_ref, w_ref, wt_ref, p_ref, pt_ref):
    xb = x_ref[...].astype(BF16)
    p_ref[...] = jnp.dot(xb, w_ref[...], preferred_element_type=F32)
    pt_ref[...] = lax.dot_general(wt_ref[...], xb, NT_DIMS, preferred_element_type=F32)


def _inproj_sample(x, w_in, w_int):
    n, d = x.shape
    d_in = w_in.shape[1]
    return pl.pallas_call(
        _inproj_s_kernel,
        grid=(1,),
        in_specs=[_const_spec((n, d)), _const_spec(w_in.shape), _const_spec(w_int.shape)],
        out_specs=[_const_spec((n, d_in)), _const_spec((d_in, n))],
        out_shape=[jax.ShapeDtypeStruct((n, d_in), F32), jax.ShapeDtypeStruct((d_in, n), F32)],
        compiler_params=_params("arbitrary"),
        name="inproj_sample",
    )(x, w_in, w_int)


def _group_select(grp, parts):
    out = parts[N_HEADS - 1]
    for g in range(N_HEADS - 2, -1, -1):
        out = jnp.where(grp == g, parts[g], out)
    return out


def _mix_p_kernel(pa_ref, pd_ref, poolw_ref, pscale_ref, loglb_ref, log1mlb_ref, omlb_ref, hgg_ref,
                  sgg_ref, sgb_ref, sgw_ref, sgbias_ref, eones_ref,
                  y_ref, pool_ref, st_out_ref, sgv_ref,
                  prev_ref, st_ref):
    c = SG_CHUNK
    w = W_GROUP
    t = pl.program_id(1)

    @pl.when(t == 0)
    def _():
        prev_ref[...] = jnp.zeros_like(prev_ref)
        st_ref[...] = jnp.zeros_like(st_ref)

    a = pa_ref[:, 0:w]
    hq = pa_ref[:, w:2 * w]
    hf = pa_ref[:, 2 * w:3 * w]
    hi = pa_ref[:, 3 * w:4 * w]
    hg = pa_ref[:, 4 * w:5 * w]
    su = pd_ref[:, 0:w]
    sv = pd_ref[:, w:2 * w]
    lane = lax.broadcasted_iota(jnp.int32, (c, w), 1)
    row = lax.broadcasted_iota(jnp.int32, (c, w), 0)
    grp = lane >> 6

    e = jnp.concatenate([prev_ref[...], a], axis=0)
    s2 = e + pltpu.roll(e, 1, 0)
    s4 = s2 + pltpu.roll(s2, 2, 0)
    s8 = s4 + pltpu.roll(s4, 4, 0)
    s16 = s8 + pltpu.roll(s8, 8, 0)
    wsum = _group_select(grp, [s2[16:], s4[16:], s8[16:], s16[16:]])
    win = _group_select(grp, [jnp.full((c, w), v, jnp.int32) for v in POOL_WINDOWS])
    cnt = jnp.minimum(t * c + row + 1, win).astype(F32)
    dpool = wsum / cnt - a
    ya = jnp.dot(dpool.astype(BF16), poolw_ref[...], preferred_element_type=F32) * pscale_ref[...]
    prev_ref[...] = a[c - 16:]
    pool_ref[0] = a[c - 16:]

    vn = _layer_norm(sv, sgg_ref[...], sgb_ref[...])
    sg = jnp.dot(sgw_ref[...], vn.astype(BF16), preferred_element_type=F32)
    s_gate = _group_select(grp, [sg[g * c:(g + 1) * c] for g in range(N_HEADS)]) + sgbias_ref[...]
    yd = su * s_gate
    sgv_ref[0] = vn

    n_sub = c // HG_SUB
    logf = _log_forget(hf, loglb_ref[...], log1mlb_ref[...])
    kin = omlb_ref[...] * jax.nn.sigmoid(-hf)
    r16 = row & (HG_SUB - 1)
    bl = logf
    rv = logf
    for sh in (1, 2, 4, 8):
        bl = bl + jnp.where(r16 >= sh, pltpu.roll(bl, sh, 0), 0.0)
        rv = rv + jnp.where(r16 + sh < HG_SUB, pltpu.roll(rv, c - sh, 0), 0.0)
    sub_row = row >> 4
    qtb = (hq * jnp.exp(bl)).astype(BF16)
    kt = kin * jnp.exp(rv - logf)
    dec = jnp.exp(rv)
    vtb = hi.T.astype(BF16)
    bi0 = lax.broadcasted_iota(jnp.int32, (w, w), 0) >> 6
    bi1 = lax.broadcasted_iota(jnp.int32, (w, w), 1) >> 6
    blockmask = bi0 == bi1
    st = st_ref[...]
    seen = []
    for i in range(n_sub):
        seen.append(st.astype(BF16))
        km = jnp.where(sub_row == i, kt, 0.0).astype(BF16)
        u = jnp.dot(vtb, km, preferred_element_type=F32)
        st = dec[i * HG_SUB:i * HG_SUB + 1] * st + jnp.where(blockmask, u, 0.0)
    st_ref[...] = st
    st_out_ref[0] = st
    qx = jnp.concatenate([jnp.where(sub_row == i, qtb, jnp.zeros_like(qtb)) for i in range(n_sub)], axis=1)
    o = lax.dot_general(qx, jnp.concatenate(seen, axis=1), NT_DIMS,
                        preferred_element_type=F32)

    rt = lax.broadcasted_iota(jnp.int32, (HG_SUB, w), 0)
    xs = []
    for i in range(n_sub):
        rows = slice(i * HG_SUB, (i + 1) * HG_SUB)
        bli, qi, ki = bl[rows], hq[rows], kin[rows]
        for s in range(HG_SUB):
            ratio = jnp.exp(jnp.minimum(bli - bli[s:s + 1], 0.0))
            xs.append(jnp.where(rt >= s, ratio * qi * ki[s:s + 1], 0.0).astype(BF16))
    x = jnp.concatenate(xs, axis=0)
    r = jnp.dot(x, eones_ref[...], preferred_element_type=F32)
    o_intra = []
    for i in range(n_sub):
        vi = hi[i * HG_SUB:(i + 1) * HG_SUB]
        base = i * HG_SUB * HG_SUB
        oi = r[base:base + HG_SUB] * vi[0:1]
        for s in range(1, HG_SUB):
            oi = oi + r[base + s * HG_SUB:base + (s + 1) * HG_SUB] * vi[s:s + 1]
        o_intra.append(oi)
    o = o + jnp.concatenate(o_intra, axis=0)
    ms = [jnp.mean(jnp.square(o[:, g * D_HEAD:(g + 1) * D_HEAD]), axis=-1, keepdims=True)
          for g in range(N_HEADS)]
    yb = o * lax.rsqrt(_group_select(grp, ms) + EPS) * hgg_ref[...] * _silu(hg)

    y_ref[:, 0:w] = ya.astype(BF16)
    y_ref[:, w:2 * w] = yb.astype(BF16)
    y_ref[:, 2 * w:3 * w] = yd.astype(BF16)


def _mix_prompt(p, n_batch, lp):
    r = p.shape[0]
    t = r // n_batch
    c = SG_CHUNK
    nt = t // c
    w = W_GROUP
    row = lambda: _const_spec((1, w))
    return pl.pallas_call(
        _mix_p_kernel,
        grid=(n_batch, nt),
        in_specs=[
            pl.BlockSpec((c, 5 * w), lambda b, i: (b * nt + i, 0)),
            pl.BlockSpec((c, 2 * w), lambda b, i: (b * nt + i, 4)),
            _const_spec((w, w)), row(), row(), row(), row(), row(), row(), row(),
            _const_spec((N_HEADS * c, c)), _const_spec((c, w)), _const_spec((w, w)),
        ],
        out_specs=[
            pl.BlockSpec((c, 3 * w), lambda b, i: (b * nt + i, 0)),
            pl.BlockSpec((1, 16, w), lambda b, i: (b, 0, 0)),
            pl.BlockSpec((1, w, w), lambda b, i: (b, 0, 0)),
            pl.BlockSpec((1, c, w), lambda b, i: (b, 0, 0)),
        ],
        out_shape=[
            jax.ShapeDtypeStruct((r, 3 * w), BF16),
            jax.ShapeDtypeStruct((n_batch, 16, w), F32),
            jax.ShapeDtypeStruct((n_batch, w, w), F32),
            jax.ShapeDtypeStruct((n_batch, c, w), F32),
        ],
        scratch_shapes=[pltpu.VMEM((16, w), F32), pltpu.VMEM((w, w), F32)],
        compiler_params=_params("arbitrary", "arbitrary"),
        name="mix_prompt",
    )(p, p, lp["pool_w"], lp["pool_scale"], lp["log_lb"], lp["log1m_lb"], lp["om_lb"], lp["hg_g"],
      lp["sg_g"], lp["sg_b"], lp["sg_w"], lp["sg_bias"], lp["eones"])


def _attn_p_kernel(cfar_ref, lam_ref, qt_ref, k_ref, vt_ref, near_ref, subg_ref, o_ref,
                   va_ref, m_ref, acc_ref, s_ref, *, tq, tk):
    h = pl.program_id(1)
    qi = pl.program_id(2)
    dh = D_HEAD

    @pl.when(qi == 0)
    def _():
        va_ref[0:dh, :] = vt_ref[0].astype(BF16)
        rows = lax.broadcasted_iota(jnp.int32, (V_AUG - dh, va_ref.shape[1]), 0)
        va_ref[dh:V_AUG, :] = jnp.where(rows == 0, 1.0, 0.0).astype(BF16)

    qt = qt_ref[0]
    sub = lax.broadcasted_iota(jnp.int32, qt.shape, 0)
    zero = jnp.zeros_like(qt)
    qm = (jnp.where(sub < DA_HALF, qt, zero), jnp.where(sub >= DA_HALF, qt, zero))
    m_ref[...] = jnp.full_like(m_ref, NEG)
    acc_ref[...] = jnp.zeros_like(acc_ref)

    def qk(ki, slot):
        k0 = pl.multiple_of(ki * tk, tk)
        kblk = k_ref[0, pl.ds(k0, tk), :]
        for mi in range(2):
            s_ref[slot, mi] = jnp.dot(kblk, qm[mi], preferred_element_type=F32)

    def softmax_pv(ki, slot, near, shift):
        k0 = pl.multiple_of(ki * tk, tk)
        vblk = va_ref[:, pl.ds(k0, tk)]
        for mi in range(2):
            s = s_ref[slot, mi]
            if near is not None:
                s = s + near_ref[0, near]
            m_blk = jnp.max(s, axis=0, keepdims=True)
            if shift is not None:
                m_blk = m_blk + shift
            m_old = m_ref[mi]
            m_new = jnp.maximum(m_old, m_blk)
            alpha = jnp.exp(m_old - m_new)
            p = jnp.exp(s - (m_new if shift is None else m_new - shift)).astype(BF16)
            acc_ref[mi] = alpha * acc_ref[mi] + jnp.dot(vblk, p, preferred_element_type=F32)
            m_ref[mi] = m_new

    cfar = cfar_ref[h]
    n_diag = tq // tk
    kb0 = qi * n_diag

    def near_tiles(first_tile, first_near, n):
        for j in range(n):
            if j + 1 < n:
                qk(first_tile + j + 1, (j + 1) % 2)
            softmax_pv(first_tile + j, j % 2, first_near + j, None)

    @pl.when(qi == 0)
    def _():
        qk(0, 0)
        near_tiles(0, 1, n_diag)

    @pl.when(qi >= 1)
    def _():
        n_far = kb0 - 1
        odd = n_far % 2

        @pl.when(odd == 1)
        def _():
            qk(0, 0)
            softmax_pv(0, 0, None, cfar)

        qk(odd, 0)

        def far_pair(pi, carry):
            b = odd + 2 * pi
            qk(b + 1, 1)
            softmax_pv(b, 0, None, cfar)
            qk(b + 2, 0)
            softmax_pv(b + 1, 1, None, cfar)
            return carry

        lax.fori_loop(0, (n_far - odd) // 2, far_pair, 0)
        near_tiles(kb0 - 1, 0, n_diag + 1)

    a0 = acc_ref[0]
    a1 = acc_ref[1]
    o = a0[0:dh] / a0[dh:dh + 1] - lam_ref[0] * (a1[0:dh] / a1[dh:dh + 1])
    ms = jnp.mean(o * o, axis=0, keepdims=True)
    o_ref[0] = (o * lax.rsqrt(ms + EPS) * subg_ref[...]).astype(BF16)


def _attn_tiles(t):
    tq = min(ATT_TQ, t)
    return tq, min(ATT_TK, tq)


def _attn_prompt(qt, k_rows, kvt, lp, n_batch):
    t = qt.shape[2]
    tq, tk = _attn_tiles(t)
    nq = t // tq
    n_near = tq // tk + 1
    smem = pl.BlockSpec(memory_space=pltpu.SMEM)
    return pl.pallas_call(
        functools.partial(_attn_p_kernel, tq=tq, tk=tk),
        grid=(n_batch, N_HEADS, nq),
        in_specs=[
            smem, smem,
            pl.BlockSpec((1, D_HEAD, tq), lambda b, h, i: (b, h, i)),
            pl.BlockSpec((1, t, D_HEAD), lambda b, h, i: (h, b, 0)),
            pl.BlockSpec((1, D_HEAD, t), lambda b, h, i: (b, N_HEADS + h, 0)),
            pl.BlockSpec((1, n_near, tk, tq), lambda b, h, i: (h, 0, 0, 0)),
            _const_spec((D_HEAD, 1)),
        ],
        out_specs=pl.BlockSpec((1, D_HEAD, tq), lambda b, h, i: (b, h, i)),
        out_shape=jax.ShapeDtypeStruct((n_batch, W_GROUP, t), BF16),
        scratch_shapes=[
            pltpu.VMEM((V_AUG, t), BF16),
            pltpu.VMEM((2, 1, tq), F32), pltpu.VMEM((2, V_AUG, tq), F32),
            pltpu.VMEM((2, 2, tk, tq), F32),
        ],
        compiler_params=_params("arbitrary", "arbitrary", "arbitrary"),
        name="attn_prompt",
    )(lp["c_far"], lp["lam"], qt, k_rows, kvt, lp["bias_near"], lp["sub_g_col"])


def _outproj_p_kernel(x_ref, y_ref, yct_ref, wabd_ref, wc_ref, g_ref, b_ref, o_ref, *, alpha):
    acc = jnp.dot(y_ref[...], wabd_ref[...], preferred_element_type=F32)
    yc = yct_ref[0].astype(F32).T.astype(BF16)
    acc = acc + jnp.dot(yc, wc_ref[...], preferred_element_type=F32)
    o_ref[...] = _layer_norm(alpha * x_ref[...] + acc, g_ref[...], b_ref[...])


def _outproj_prompt(x, y_abd, y_ct, w_abd, w_c, g, b, alpha):
    r, d = x.shape
    t = y_ct.shape[2]
    tm = min(ROW_TILE, t)
    tpb = t // tm
    return pl.pallas_call(
        functools.partial(_outproj_p_kernel, alpha=alpha),
        grid=(r // tm,),
        in_specs=[
            pl.BlockSpec((tm, d), lambda i: (i, 0)),
            pl.BlockSpec((tm, 3 * W_GROUP), lambda i: (i, 0)),
            pl.BlockSpec((1, W_GROUP, tm), lambda i: (i // tpb, 0, i % tpb)),
            _const_spec(w_abd.shape), _const_spec(w_c.shape), _const_spec((1, d)), _const_spec((1, d)),
        ],
        out_specs=pl.BlockSpec((tm, d), lambda i: (i, 0)),
        out_shape=jax.ShapeDtypeStruct((r, d), F32),
        compiler_params=_params("arbitrary"),
        name="outproj_prompt",
    )(x, y_abd, y_ct, w_abd, w_c, g, b)


def _outproj_s_kernel(x_ref, ya_ref, ybt_ref, yct_ref, yd_ref, w_ref, g_ref, b_ref, o_ref, *, alpha):
    w = W_GROUP
    parts = (ya_ref[...], ybt_ref[...].T, yct_ref[...].T, yd_ref[...])
    acc = None
    for j, part in enumerate(parts):
        d = jnp.dot(part.astype(BF16), w_ref[j * w:(j + 1) * w, :], preferred_element_type=F32)
        acc = d if acc is None else acc + d
    o_ref[...] = _layer_norm(alpha * x_ref[...] + acc, g_ref[...], b_ref[...])


def _outproj_sample(x, y_a, y_bt, y_ct, y_d, w_out, g, b, alpha):
    n, d = x.shape
    w = W_GROUP
    return pl.pallas_call(
        functools.partial(_outproj_s_kernel, alpha=alpha),
        grid=(1,),
        in_specs=[
            _const_spec((n, d)), _const_spec((n, w)), _const_spec((w, n)), _const_spec((w, n)),
            _const_spec((n, w)), _const_spec(w_out.shape), _const_spec((1, d)), _const_spec((1, d)),
        ],
        out_specs=_const_spec((n, d)),
        out_shape=jax.ShapeDtypeStruct((n, d), F32),
        compiler_params=_params("arbitrary"),
        name="outproj_sample",
    )(x, y_a, y_bt, y_ct, y_d, w_out, g, b)


def _mix_s_kernel(pa_ref, pd_ref, pool_ref, poolw_ref, pscale_ref, sgg_ref, sgb_ref, w00_ref, b0_ref,
                  ya_ref, yd_ref, newpool_ref, vn_ref, *, cnts):
    a = pa_ref[...]
    n, w = a.shape
    grp = lax.broadcasted_iota(jnp.int32, (n, w), 1) >> 6
    acc = a
    means = []
    for j in range(1, POOL_BUF + 1):
        acc = acc + pool_ref[POOL_BUF - j]
        if j + 1 in POOL_WINDOWS:
            means.append(acc / cnts[POOL_WINDOWS.index(j + 1)])
    dpool = _group_select(grp, means) - a
    ya_ref[...] = jnp.dot(dpool.astype(BF16), poolw_ref[...], preferred_element_type=F32) * pscale_ref[...]
    for j in range(POOL_BUF - 1):
        newpool_ref[j] = pool_ref[j + 1]
    newpool_ref[POOL_BUF - 1] = a

    vn = _layer_norm(pd_ref[:, w:2 * w], sgg_ref[...], sgb_ref[...])
    vn_ref[...] = vn
    yd_ref[...] = pd_ref[:, 0:w] * (w00_ref[...] * vn + b0_ref[...])


def _mix_sample(p, pool, lp, past_len):
    n = p.shape[0]
    w = W_GROUP
    cnts = tuple(float(min(past_len + 1, win)) for win in POOL_WINDOWS)
    row = lambda: _const_spec((1, w))
    return pl.pallas_call(
        functools.partial(_mix_s_kernel, cnts=cnts),
        grid=(1,),
        in_specs=[
            pl.BlockSpec((n, w), lambda i: (0, 0)),
            pl.BlockSpec((n, 2 * w), lambda i: (0, 4)),
            _const_spec(pool.shape), _const_spec((w, w)), row(), row(), row(), row(), row(),
        ],
        out_specs=[_const_spec((n, w)), _const_spec((n, w)), _const_spec(pool.shape), _const_spec((n, w))],
        out_shape=[
            jax.ShapeDtypeStruct((n, w), F32), jax.ShapeDtypeStruct((n, w), F32),
            jax.ShapeDtypeStruct(pool.shape, F32), jax.ShapeDtypeStruct((n, w), F32),
        ],
        compiler_params=_params("arbitrary"),
        name="mix_sample",
    )(p, p, pool, lp["pool_w"], lp["pool_scale"], lp["sg_g"], lp["sg_b"], lp["sg_w00"], lp["sg_b0"])


def _hgrn_s_kernel(q_ref, f_ref, i_ref, g_ref, s_ref, loglb_ref, log1mlb_ref, omlb_ref, hgg_ref,
                   so_ref, yb_ref):
    z = f_ref[...]
    f = jnp.exp(_log_forget(z, loglb_ref[...], log1mlb_ref[...]))
    kin = omlb_ref[...] * jax.nn.sigmoid(-z)
    q = q_ref[...]
    v = i_ref[...]
    o = jnp.zeros_like(v)
    for d in range(D_HEAD):
        sn = f[d:d + 1] * s_ref[0, d] + kin[d:d + 1] * v
        so_ref[0, d] = sn
        o = o + q[d:d + 1] * sn
    ms = jnp.mean(o * o, axis=0, keepdims=True)
    yb_ref[...] = o * lax.rsqrt(ms + EPS) * hgg_ref[...] * _silu(g_ref[...])


def _hgrn_sample(pt, state, lp):
    n = pt.shape[1]
    dh = D_HEAD
    blk = lambda off: pl.BlockSpec((dh, n), lambda h: (off * N_HEADS + h, 0))
    col = lambda: pl.BlockSpec((dh, 1), lambda h: (h, 0))
    return pl.pallas_call(
        _hgrn_s_kernel,
        grid=(N_HEADS,),
        in_specs=[
            blk(1), blk(2), blk(3), blk(4),
            pl.BlockSpec((1, dh, dh, n), lambda h: (h, 0, 0, 0)),
            col(), col(), col(), _const_spec((dh, 1)),
        ],
        out_specs=[
            pl.BlockSpec((1, dh, dh, n), lambda h: (h, 0, 0, 0)),
            pl.BlockSpec((dh, n), lambda h: (h, 0)),
        ],
        out_shape=[
            jax.ShapeDtypeStruct(state.shape, F32),
            jax.ShapeDtypeStruct((W_GROUP, n), F32),
        ],
        compiler_params=_params("arbitrary"),
        name="hgrn_sample",
    )(pt, pt, pt, pt, state, lp["log_lb_col"], lp["log1m_lb_col"], lp["om_lb_col"], lp["hg_g_col"])


def _attn_s_one(n, k_pages, v_pages, lam, qt_ref, knt_ref, vnt_ref, bias_ref, bself_ref, n_pages, q_scale):
    w, ns = qt_ref.shape
    dh = D_HEAD
    sel = lax.broadcasted_iota(jnp.int32, (w, ns), 1) == n

    def column(ref):
        return jnp.sum(jnp.where(sel, ref[...], 0.0), axis=1, keepdims=True)

    qcol = column(qt_ref) * q_scale
    kcol = column(knt_ref)
    vcol = column(vnt_ref)

    n_grp = 2 * N_HEADS
    rows = ([], [])
    self_rows = ([], [])
    for h in range(N_HEADS):
        qh = qcol[h * dh:(h + 1) * dh]
        for j in range(n_pages):
            prod = k_pages[j][h] * qh
            rows[0].append(jnp.sum(prod[0:DA_HALF], axis=0, keepdims=True))
            rows[1].append(jnp.sum(prod[DA_HALF:dh], axis=0, keepdims=True))
        self_prod = qh * kcol[h * dh:(h + 1) * dh]
        for mi in range(2):
            s_self = (jnp.sum(self_prod[mi * DA_HALF:(mi + 1) * DA_HALF], axis=0, keepdims=True)
                      + bself_ref[h])
            self_rows[mi].append(jnp.broadcast_to(s_self, (n_pages, 1)))
    s = jnp.concatenate(rows[0] + rows[1], axis=0) + bias_ref[...]
    s_self = jnp.concatenate(self_rows[0] + self_rows[1], axis=0)

    def per_group(col, reduce):
        parts = [jnp.broadcast_to(reduce(col[g * n_pages:(g + 1) * n_pages], axis=0, keepdims=True),
                                  (n_pages, 1)) for g in range(n_grp)]
        return jnp.concatenate(parts, axis=0)

    m = jnp.maximum(per_group(jnp.max(s, axis=1, keepdims=True), jnp.max), s_self)
    p = jnp.exp(s - m)
    p_self = jnp.exp(s_self - m)
    inv_l = 1.0 / (per_group(jnp.sum(p, axis=1, keepdims=True), jnp.sum) + p_self)
    half = N_HEADS * n_pages
    a = (p * inv_l)[0:half] - lam * (p * inv_l)[half:2 * half]
    a_self = (p_self * inv_l)[0:half] - lam * (p_self * inv_l)[half:2 * half]
    weighted = []
    for h in range(N_HEADS):
        oh = v_pages[0][h] * a[h * n_pages:h * n_pages + 1]
        for j in range(1, n_pages):
            oh = oh + v_pages[j][h] * a[h * n_pages + j:h * n_pages + j + 1]
        weighted.append(oh)
    a_self_col = jnp.concatenate(
        [jnp.broadcast_to(a_self[h * n_pages:h * n_pages + 1], (dh, 1)) for h in range(N_HEADS)], axis=0)
    ocol = jnp.sum(jnp.concatenate(weighted, axis=0), axis=1, keepdims=True) + a_self_col * vcol
    return sel, ocol


def _attn_s_kernel(pt_ref, lam_ref, qt_ref, knt_ref, vnt_ref, bias_ref, bself_ref, subg_ref, *rest,
                   n_pages, sps, q_scale):
    o_ref = rest[2 * sps * n_pages]
    ot_ref = rest[2 * sps * n_pages + 1]
    del pt_ref
    dh = D_HEAD
    step = pl.program_id(0)
    ot = ot_ref[...]
    for u in range(sps):
        sel, ocol = _attn_s_one(step * sps + u, rest[u * n_pages:(u + 1) * n_pages],
                                rest[(sps + u) * n_pages:(sps + u + 1) * n_pages], lam_ref[0],
                                qt_ref, knt_ref, vnt_ref, bias_ref, bself_ref, n_pages, q_scale)
        ot = jnp.where(sel, ocol, ot)
    ot_ref[...] = ot

    @pl.when(step == pl.num_programs(0) - 1)
    def _():
        for h in range(N_HEADS):
            oh = ot_ref[h * dh:(h + 1) * dh, :]
            ms = jnp.mean(oh * oh, axis=0, keepdims=True)
            o_ref[h * dh:(h + 1) * dh, :] = oh * lax.rsqrt(ms + EPS) * subg_ref[...]


def _attn_sample(pt, cache_kt, cache_vt, page_table, layer, lp):
    n = pt.shape[1]
    n_pages = page_table.shape[1]
    w = W_GROUP
    dh = D_HEAD
    blk = lambda off: pl.BlockSpec((w, n), lambda i, tbl: (off, 0))
    sps = DEC_SAMPLES_PER_STEP
    page_specs = [
        pl.BlockSpec((None, None, N_HEADS, dh, PAGE),
                     functools.partial(lambda i, tbl, u, j: (layer, tbl[i * sps + u, j], 0, 0, 0), u=u, j=j))
        for u in range(sps) for j in range(n_pages)
    ]
    grid_spec = pltpu.PrefetchScalarGridSpec(
        num_scalar_prefetch=1,
        grid=(n // sps,),
        in_specs=[
            pl.BlockSpec(memory_space=pltpu.SMEM),
            blk(5), blk(6), blk(7),
            pl.BlockSpec((2 * N_HEADS * n_pages, PAGE), lambda i, tbl: (0, 0)),
            pl.BlockSpec((N_HEADS, 1, 1), lambda i, tbl: (0, 0, 0)),
            pl.BlockSpec((dh, 1), lambda i, tbl: (0, 0)),
        ] + page_specs + page_specs,
        out_specs=pl.BlockSpec((w, n), lambda i, tbl: (0, 0)),
        scratch_shapes=[pltpu.VMEM((w, n), F32)],
    )
    return pl.pallas_call(
        functools.partial(_attn_s_kernel, n_pages=n_pages, sps=sps, q_scale=DA_HALF ** -0.5),
        grid_spec=grid_spec,
        out_shape=jax.ShapeDtypeStruct((w, n), F32),
        compiler_params=_params("arbitrary"),
        name="attn_sample",
    )(page_table, lp["lam"], pt, pt, pt, lp["bias_past"], lp["bias_self"], lp["sub_g_col"],
      *([cache_kt] * (sps * n_pages)), *([cache_vt] * (sps * n_pages)))


def _rel_bucket(dist):
    n = jnp.maximum(dist, 0)
    max_exact = REL_BUCKETS // 2
    large = max_exact + (jnp.log(jnp.maximum(n, 1).astype(F32) / max_exact)
                         / math.log(REL_MAX_DIST / max_exact) * (REL_BUCKETS - max_exact)).astype(jnp.int32)
    large = jnp.minimum(large, REL_BUCKETS - 1)
    return jnp.where(n < max_exact, n, large)


def _layer_params(l, depth, tq, tk, past_len, n_pages, prm, lb_all):
    w = W_GROUP
    lam_init = 0.8 - 0.6 * math.exp(-0.3 * l)
    lam = (jnp.exp(jnp.sum(prm["diff_lam_q1"][l] * prm["diff_lam_k1"][l]))
           - jnp.exp(jnp.sum(prm["diff_lam_q2"][l] * prm["diff_lam_k2"][l])) + lam_init)
    rel_bias = prm["rel_bias"]

    def bias_of(dist):
        onehot = _rel_bucket(dist)[..., None] == jnp.arange(REL_BUCKETS, dtype=jnp.int32)
        table = rel_bias.T.reshape((N_HEADS,) + (1,) * dist.ndim + (REL_BUCKETS,))
        return jnp.sum(jnp.where(onehot[None], table, 0.0), axis=-1)

    span = tq + tk - 1
    jj = jnp.arange(span + 1, dtype=jnp.int32)
    tiles = []
    for r in range(-1, tq // tk):
        dist = jj - (tk - 1) - r * tk
        f = jnp.where((dist >= 0)[None], bias_of(dist), NEG)
        sheared = jnp.tile(f, (1, tk))[:, :tk * span].reshape(N_HEADS, tk, span)
        tiles.append(sheared[:, :, tk - 1:tk - 1 + tq])
    bias_near = jnp.stack(tiles, axis=1)
    kpos = jnp.arange(n_pages * PAGE, dtype=jnp.int32)
    bias_past = jnp.tile(bias_of(past_len - kpos).reshape(N_HEADS * n_pages, PAGE), (2, 1))
    lb = lb_all[l]
    w_in = prm["w_in"][l]
    w_int = w_in.T.astype(BF16)
    w_out = prm["w_out"][l].astype(BF16)
    tril = jnp.tril(jnp.ones((SG_CHUNK, SG_CHUNK), F32))
    sub_g = prm["diff_subln_g"][l] * (1.0 - lam_init)
    return {
        "ffn1_w_gu": prm["ffn1_w_gu"][l].astype(BF16), "ffn1_w_dn": prm["ffn1_w_dn"][l].astype(BF16),
        "ffn2_w_gu": prm["ffn2_w_gu"][l].astype(BF16), "ffn2_w_dn": prm["ffn2_w_dn"][l].astype(BF16),
        "ln_g": prm["ln_g"][l], "ln_b": prm["ln_b"][l],
        "w_in": w_in.astype(BF16), "w_int": w_int, "w_qkvt": w_int[5 * w:8 * w],
        "w_out": w_out,
        "w_abd": jnp.concatenate([w_out[0:2 * w], w_out[3 * w:4 * w]], axis=0),
        "w_c": w_out[2 * w:3 * w],
        "pool_w": jax.scipy.linalg.block_diag(*prm["pool_w"][l]).astype(BF16),
        "pool_scale": prm["pool_scale"][l][None],
        "log_lb": jnp.log(lb)[None], "log1m_lb": jnp.log1p(-lb)[None], "om_lb": (1.0 - lb)[None],
        "log_lb_col": jnp.log(lb)[:, None], "log1m_lb_col": jnp.log1p(-lb)[:, None],
        "om_lb_col": (1.0 - lb)[:, None],
        "hg_g": jnp.tile(prm["hgrn_norm_g"][l], N_HEADS)[None], "hg_g_col": prm["hgrn_norm_g"][l][:, None],
        "sg_g": prm["sgu_ln_g"][l][None], "sg_b": prm["sgu_ln_b"][l][None],
        "sg_w": (prm["sgu_w"][l] * tril).reshape(N_HEADS * SG_CHUNK, SG_CHUNK).astype(BF16),
        "sg_bias": jnp.repeat(prm["sgu_b"][l].T, D_HEAD, axis=1),
        "sg_w00": jnp.repeat(prm["sgu_w"][l][:, 0, 0], D_HEAD)[None],
        "sg_b0": jnp.repeat(prm["sgu_b"][l][:, 0], D_HEAD)[None],
        "eones": jnp.kron(jnp.eye(N_HEADS, dtype=F32), jnp.ones((D_HEAD, D_HEAD), F32)).astype(BF16),
        "lam": lam.reshape(1), "c_far": rel_bias[REL_BUCKETS - 1],
        "bias_near": bias_near,
        "bias_past": bias_past, "bias_self": rel_bias[0].reshape(N_HEADS, 1, 1),
        "sub_g_col": sub_g[:, None],
    }


def kernel(x_prompt, x_sample, state_pool, state_hgrn, cache_k, cache_v, page_table, rel_bias, ln_g, ln_b,
           ffn1_w_gu, ffn1_w_dn, ffn2_w_gu, ffn2_w_dn, w_in, w_out, pool_w, pool_scale, hgrn_lb,
           hgrn_norm_g, diff_lam_q1, diff_lam_k1, diff_lam_q2, diff_lam_k2, diff_subln_g, sgu_ln_g,
           sgu_ln_b, sgu_w, sgu_b):
    prm = dict(rel_bias=rel_bias, ln_g=ln_g, ln_b=ln_b, ffn1_w_gu=ffn1_w_gu, ffn1_w_dn=ffn1_w_dn,
               ffn2_w_gu=ffn2_w_gu, ffn2_w_dn=ffn2_w_dn, w_in=w_in, w_out=w_out, pool_w=pool_w,
               pool_scale=pool_scale, hgrn_norm_g=hgrn_norm_g, diff_lam_q1=diff_lam_q1,
               diff_lam_k1=diff_lam_k1, diff_lam_q2=diff_lam_q2, diff_lam_k2=diff_lam_k2,
               diff_subln_g=diff_subln_g, sgu_ln_g=sgu_ln_g, sgu_ln_b=sgu_ln_b, sgu_w=sgu_w, sgu_b=sgu_b)
    depth = w_in.shape[0]
    nb, t, d = x_prompt.shape
    ns = x_sample.shape[0]
    n_pages = page_table.shape[1]
    past_len = n_pages * PAGE
    alpha = (2.0 * depth) ** 0.25
    w = W_GROUP
    tq, tk = _attn_tiles(t)
    assert x_sample.shape[1] == 1 and t % SG_CHUNK == 0 and t % tq == 0 and tk >= REL_MAX_DIST

    lb_cum = jnp.cumsum(jax.nn.softmax(hgrn_lb.astype(F32), axis=0), axis=0)
    lb_all = jnp.maximum(lb_cum - lb_cum[:1], 0.0)

    cache_kt = jnp.transpose(cache_k, (0, 1, 3, 4, 2))
    cache_vt = jnp.transpose(cache_v, (0, 1, 3, 4, 2))
    hgrn_t = jnp.transpose(state_hgrn, (0, 2, 3, 4, 1))
    pool_t = jnp.transpose(state_pool, (0, 2, 1, 3))

    xp = x_prompt.reshape(nb * t, d)
    xs = x_sample.reshape(ns, d)
    outs = {k: [] for k in ("pool_p", "pool_s", "hgrn_p", "hgrn_s", "k_p", "k_s", "v_p", "v_s", "sgv_p", "sgv_s")}
    for l in range(depth):
        lp = _layer_params(l, depth, tq, tk, past_len, n_pages, prm, lb_all)
        g = [lp["ln_g"][i][None] for i in range(3)]
        b = [lp["ln_b"][i][None] for i in range(3)]

        xp = _ffn(xp, lp["ffn1_w_gu"], lp["ffn1_w_dn"], g[0], b[0], alpha)
        p, k_rows, q_t, kvt = _inproj_prompt(xp, lp["w_in"], lp["w_qkvt"], nb)
        y_abd, pool16, st, sgv = _mix_prompt(p, nb, lp)
        y_ct = _attn_prompt(q_t, k_rows, kvt, lp, nb)
        xp = _outproj_prompt(xp, y_abd, y_ct, lp["w_abd"], lp["w_c"], g[1], b[1], alpha)
        xp = _ffn(xp, lp["ffn2_w_gu"], lp["ffn2_w_dn"], g[2], b[2], alpha)
        kvt5 = kvt.reshape(nb, 2, N_HEADS, D_HEAD, t)
        outs["k_p"].append(jnp.transpose(kvt5[:, 0], (0, 3, 1, 2)))
        outs["v_p"].append(jnp.transpose(kvt5[:, 1], (0, 3, 1, 2)))
        outs["pool_p"].append(pool16[:, 1:])
        outs["hgrn_p"].append(jnp.stack(
            [jnp.swapaxes(st[:, h * D_HEAD:(h + 1) * D_HEAD, h * D_HEAD:(h + 1) * D_HEAD], 1, 2)
             for h in range(N_HEADS)], axis=1))
        outs["sgv_p"].append(sgv)

        xs = _ffn(xs, lp["ffn1_w_gu"], lp["ffn1_w_dn"], g[0], b[0], alpha)
        ps, pst = _inproj_sample(xs, lp["w_in"], lp["w_int"])
        y_a, y_d, new_pool, vn = _mix_sample(ps, pool_t[l], lp, past_len)
        new_state, y_bt = _hgrn_sample(pst, hgrn_t[l], lp)
        y_ct = _attn_sample(pst, cache_kt, cache_vt, page_table, l, lp)
        xs = _outproj_sample(xs, y_a, y_bt, y_ct, y_d, lp["w_out"], g[1], b[1], alpha)
        xs = _ffn(xs, lp["ffn2_w_gu"], lp["ffn2_w_dn"], g[2], b[2], alpha)
        outs["k_s"].append(jnp.transpose(pst[6 * w:7 * w].reshape(N_HEADS, D_HEAD, ns), (2, 0, 1))[:, None])
        outs["v_s"].append(jnp.transpose(pst[7 * w:8 * w].reshape(N_HEADS, D_HEAD, ns), (2, 0, 1))[:, None])
        outs["pool_s"].append(jnp.transpose(new_pool, (1, 0, 2)))
        outs["hgrn_s"].append(jnp.transpose(new_state, (3, 0, 1, 2)))
        outs["sgv_s"].append(vn[:, None])

    st = {k: jnp.stack(v, axis=0) for k, v in outs.items()}
    return (xp.reshape(nb, t, d), xs.reshape(ns, 1, d), st["pool_p"], st["pool_s"], st["hgrn_p"], st["hgrn_s"],
            st["k_p"], st["k_s"], st["v_p"], st["v_s"], st["sgv_p"], st["sgv_s"])
```

```python
import functools
import math

import jax
import jax.numpy as jnp
import jax.scipy.linalg
from jax import lax
from jax.experimental import pallas as pl
from jax.experimental.pallas import tpu as pltpu

F32 = jnp.float32
BF16 = jnp.bfloat16

N_MIX = 4
W_GROUP = 256
N_HEADS = 4
D_HEAD = 64
DA_HALF = 32
POOL_WINDOWS = (2, 4, 8, 16)
POOL_BUF = 15
PAGE = 128
SG_CHUNK = 128
HG_SUB = 16
V_AUG = 80
REL_BUCKETS = 32
REL_MAX_DIST = 128
EPS = 1e-5
NEG = -1e30
LOG2E = 1.0 / math.log(2.0)

VMEM_LIMIT = 56 * 1024 * 1024
MXU_DIM = 256
ROW_TILE = 512
ATT_TQ = 1024
ATT_TK = 512
DEC_SAMPLES_PER_STEP = 2

NT_DIMS = (((1,), (1,)), ((), ()))


def _params(*sem):
    return pltpu.CompilerParams(dimension_semantics=sem, vmem_limit_bytes=VMEM_LIMIT)


def _const_spec(shape, single=False):
    nd = len(shape)
    kw = {"pipeline_mode": pl.Buffered(1)} if single else {}
    return pl.BlockSpec(shape, lambda *_: (0,) * nd, **kw)


def _layer_spec(shape, layer, single=False, rows=None, row_block=0):
    kw = {"pipeline_mode": pl.Buffered(1)} if single else {}
    block = (None, shape[1] if rows is None else rows) + tuple(shape[2:])
    return pl.BlockSpec(block, lambda *_: (layer, row_block) + (0,) * (len(shape) - 2), **kw)


def _layer_norm(y, g, b):
    mu = jnp.mean(y, axis=-1, keepdims=True)
    yc = y - mu
    var = jnp.mean(yc * yc, axis=-1, keepdims=True)
    return yc * lax.rsqrt(var + EPS) * g + b


def _silu(x):
    return x * jax.nn.sigmoid(x)


def _log_forget(z, log_lb, log1m_lb):
    log_sig = -(jnp.maximum(-z, 0.0) + jnp.log1p(jnp.exp(-jnp.abs(z))))
    b = log1m_lb + log_sig
    return jnp.maximum(log_lb, b) + jnp.log1p(jnp.exp(-jnp.abs(log_lb - b)))


def _ffn_kernel(x_ref, wgu_ref, wdn_ref, g_ref, b_ref, o_ref, *, d_ff, bounds, alpha):
    x = x_ref[...]
    xb = x.astype(BF16)
    acc = None
    for lo, hi in zip(bounds[:-1], bounds[1:]):
        gate = jnp.dot(xb, wgu_ref[:, lo:hi], preferred_element_type=F32)
        up = jnp.dot(xb, wgu_ref[:, d_ff + lo:d_ff + hi], preferred_element_type=F32)
        h = (_silu(gate) * up).astype(BF16)
        part = jnp.dot(h, wdn_ref[lo:hi, :], preferred_element_type=F32)
        acc = part if acc is None else acc + part
    o_ref[...] = _layer_norm(alpha * x + 0.5 * acc, g_ref[...], b_ref[...])


def _ffn(x, w_gu, w_dn, layer, g, b, alpha):
    r, d = x.shape
    d_ff = w_dn.shape[1]
    tm = min(ROW_TILE, r)
    n_tiles = d_ff // MXU_DIM
    bounds = (0, (n_tiles + 1) // 2 * MXU_DIM, d_ff)
    return pl.pallas_call(
        functools.partial(_ffn_kernel, d_ff=d_ff, bounds=bounds, alpha=alpha),
        grid=(r // tm,),
        in_specs=[
            pl.BlockSpec((tm, d), lambda i: (i, 0)),
            _layer_spec(w_gu.shape, layer, single=True),
            _layer_spec(w_dn.shape, layer, single=True),
            _const_spec((1, d)),
            _const_spec((1, d)),
        ],
        out_specs=pl.BlockSpec((tm, d), lambda i: (i, 0)),
        out_shape=jax.ShapeDtypeStruct((r, d), F32),
        compiler_params=_params("arbitrary"),
        name="ffn",
    )(x, w_gu, w_dn, g, b)


def _inproj_p_kernel(x_ref, w_ref, wqkvt_ref, p_ref, k_ref, qt_ref, kvt_ref, *, q_scale):
    xb = x_ref[...].astype(BF16)
    p = jnp.dot(xb, w_ref[...], preferred_element_type=F32)
    p_ref[...] = p
    for h in range(N_HEADS):
        k_ref[h] = p[:, 6 * W_GROUP + h * D_HEAD:6 * W_GROUP + (h + 1) * D_HEAD].astype(BF16)
    qkvt = lax.dot_general(wqkvt_ref[...], xb, NT_DIMS, preferred_element_type=F32)
    qt_ref[0] = (qkvt[0:W_GROUP] * q_scale).astype(BF16)
    kvt_ref[0] = qkvt[W_GROUP:3 * W_GROUP]


def _inproj_prompt(x, w_in, w_qkvt, layer, n_batch):
    r, d = x.shape
    t = r // n_batch
    d_in = w_in.shape[2]
    tm = min(ROW_TILE, t)
    tpb = t // tm
    return pl.pallas_call(
        functools.partial(_inproj_p_kernel, q_scale=DA_HALF ** -0.5 * LOG2E),
        grid=(r // tm,),
        in_specs=[
            pl.BlockSpec((tm, d), lambda i: (i, 0)),
            _layer_spec(w_in.shape, layer),
            _layer_spec(w_qkvt.shape, layer),
        ],
        out_specs=[
            pl.BlockSpec((tm, d_in), lambda i: (i, 0)),
            pl.BlockSpec((N_HEADS, tm, D_HEAD), lambda i: (0, i, 0)),
            pl.BlockSpec((1, W_GROUP, tm), lambda i: (i // tpb, 0, i % tpb)),
            pl.BlockSpec((1, 2 * W_GROUP, tm), lambda i: (i // tpb, 0, i % tpb)),
        ],
        out_shape=[
            jax.ShapeDtypeStruct((r, d_in), F32),
            jax.ShapeDtypeStruct((N_HEADS, r, D_HEAD), BF16),
            jax.ShapeDtypeStruct((n_batch, W_GROUP, t), BF16),
            jax.ShapeDtypeStruct((n_batch, 2 * W_GROUP, t), F32),
        ],
        compiler_params=_params("arbitrary"),
        name="inproj_prompt",
    )(x, w_in, w_qkvt)


def _inproj_s_kernel(x_ref, w_ref, wt_ref, p_ref, pt_ref):
    xb = x_ref[...].astype(BF16)
    p_ref[...] = jnp.dot(xb, w_ref[...], preferred_element_type=F32)
    pt_ref[...] = lax.dot_general(wt_ref[...], xb, NT_DIMS, preferred_element_type=F32)


def _inproj_sample(x, w_in, w_int, layer):
    n, d = x.shape
    d_in = w_in.shape[2]
    return pl.pallas_call(
        _inproj_s_kernel,
        grid=(1,),
        in_specs=[_const_spec((n, d)), _layer_spec(w_in.shape, layer), _layer_spec(w_int.shape, layer)],
        out_specs=[_const_spec((n, d_in)), _const_spec((d_in, n))],
        out_shape=[jax.ShapeDtypeStruct((n, d_in), F32), jax.ShapeDtypeStruct((d_in, n), F32)],
        compiler_params=_params("arbitrary"),
        name="inproj_sample",
    )(x, w_in, w_int)


def _group_select(grp, parts):
    out = parts[N_HEADS - 1]
    for g in range(N_HEADS - 2, -1, -1):
        out = jnp.where(grp == g, parts[g], out)
    return out


def _mix_p_kernel(pa_ref, pd_ref, poolw_ref, pscale_ref, loglb_ref, log1mlb_ref, omlb_ref, hgg_ref,
                  sgg_ref, sgb_ref, sgw_ref, sgbias_ref, eones_ref,
                  y_ref, pool_ref, st_out_ref, sgv_ref,
                  prev_ref, st_ref):
    c = SG_CHUNK
    w = W_GROUP
    t = pl.program_id(1)

    @pl.when(t == 0)
    def _():
        prev_ref[...] = jnp.zeros_like(prev_ref)
        st_ref[...] = jnp.zeros_like(st_ref)

    a = pa_ref[:, 0:w]
    hq = pa_ref[:, w:2 * w]
    hf = pa_ref[:, 2 * w:3 * w]
    hi = pa_ref[:, 3 * w:4 * w]
    hg = pa_ref[:, 4 * w:5 * w]
    su = pd_ref[:, 0:w]
    sv = pd_ref[:, w:2 * w]
    lane = lax.broadcasted_iota(jnp.int32, (c, w), 1)
    row = lax.broadcasted_iota(jnp.int32, (c, w), 0)
    grp = lane >> 6

    e = jnp.concatenate([prev_ref[...], a], axis=0)
    s2 = e + pltpu.roll(e, 1, 0)
    s4 = s2 + pltpu.roll(s2, 2, 0)
    s8 = s4 + pltpu.roll(s4, 4, 0)
    s16 = s8 + pltpu.roll(s8, 8, 0)
    wsum = _group_select(grp, [s2[16:], s4[16:], s8[16:], s16[16:]])
    win = _group_select(grp, [jnp.full((c, w), v, jnp.int32) for v in POOL_WINDOWS])
    cnt = jnp.minimum(t * c + row + 1, win).astype(F32)
    dpool = wsum / cnt - a
    ya = jnp.dot(dpool.astype(BF16), poolw_ref[...], preferred_element_type=F32) * pscale_ref[...]
    prev_ref[...] = a[c - 16:]
    pool_ref[0] = a[c - 16:]

    vn = _layer_norm(sv, sgg_ref[...], sgb_ref[...])
    sg = jnp.dot(sgw_ref[...], vn.astype(BF16), preferred_element_type=F32)
    s_gate = _group_select(grp, [sg[g * c:(g + 1) * c] for g in range(N_HEADS)]) + sgbias_ref[...]
    yd = su * s_gate
    sgv_ref[0] = vn

    n_sub = c // HG_SUB
    logf = _log_forget(hf, loglb_ref[...], log1mlb_ref[...])
    kin = omlb_ref[...] * jax.nn.sigmoid(-hf)
    r16 = row & (HG_SUB - 1)
    bl = logf
    rv = logf
    for sh in (1, 2, 4, 8):
        bl = bl + jnp.where(r16 >= sh, pltpu.roll(bl, sh, 0), 0.0)
        rv = rv + jnp.where(r16 + sh < HG_SUB, pltpu.roll(rv, c - sh, 0), 0.0)
    sub_row = row >> 4
    qtb = (hq * jnp.exp(bl)).astype(BF16)
    kt = kin * jnp.exp(rv - logf)
    dec = jnp.exp(rv)
    vtb = hi.T.astype(BF16)
    bi0 = lax.broadcasted_iota(jnp.int32, (w, w), 0) >> 6
    bi1 = lax.broadcasted_iota(jnp.int32, (w, w), 1) >> 6
    blockmask = bi0 == bi1
    st = st_ref[...]
    seen = []
    for i in range(n_sub):
        seen.append(st.astype(BF16))
        km = jnp.where(sub_row == i, kt, 0.0).astype(BF16)
        u = jnp.dot(vtb, km, preferred_element_type=F32)
        st = dec[i * HG_SUB:i * HG_SUB + 1] * st + jnp.where(blockmask, u, 0.0)
    st_ref[...] = st
    st_out_ref[0] = st
    qx = jnp.concatenate([jnp.where(sub_row == i, qtb, jnp.zeros_like(qtb)) for i in range(n_sub)], axis=1)
    o = lax.dot_general(qx, jnp.concatenate(seen, axis=1), NT_DIMS,
                        preferred_element_type=F32)

    rt = lax.broadcasted_iota(jnp.int32, (HG_SUB, w), 0)
    xs = []
    for i in range(n_sub):
        rows = slice(i * HG_SUB, (i + 1) * HG_SUB)
        bli, qi, ki = bl[rows], hq[rows], kin[rows]
        for s in range(HG_SUB):
            ratio = jnp.exp(jnp.minimum(bli - bli[s:s + 1], 0.0))
            xs.append(jnp.where(rt >= s, ratio * qi * ki[s:s + 1], 0.0).astype(BF16))
    x = jnp.concatenate(xs, axis=0)
    r = jnp.dot(x, eones_ref[...], preferred_element_type=F32)
    o_intra = []
    for i in range(n_sub):
        vi = hi[i * HG_SUB:(i + 1) * HG_SUB]
        base = i * HG_SUB * HG_SUB
        oi = r[base:base + HG_SUB] * vi[0:1]
        for s in range(1, HG_SUB):
            oi = oi + r[base + s * HG_SUB:base + (s + 1) * HG_SUB] * vi[s:s + 1]
        o_intra.append(oi)
    o = o + jnp.concatenate(o_intra, axis=0)
    ms = [jnp.mean(jnp.square(o[:, g * D_HEAD:(g + 1) * D_HEAD]), axis=-1, keepdims=True)
          for g in range(N_HEADS)]
    yb = o * lax.rsqrt(_group_select(grp, ms) + EPS) * hgg_ref[...] * _silu(hg)

    y_ref[:, 0:w] = ya.astype(BF16)
    y_ref[:, w:2 * w] = yb.astype(BF16)
    y_ref[:, 2 * w:3 * w] = yd.astype(BF16)


def _mix_prompt(p, n_batch, lp):
    r = p.shape[0]
    t = r // n_batch
    c = SG_CHUNK
    nt = t // c
    w = W_GROUP
    row = lambda: _const_spec((1, w))
    return pl.pallas_call(
        _mix_p_kernel,
        grid=(n_batch, nt),
        in_specs=[
            pl.BlockSpec((c, 5 * w), lambda b, i: (b * nt + i, 0)),
            pl.BlockSpec((c, 2 * w), lambda b, i: (b * nt + i, 4)),
            _const_spec((w, w)), row(), row(), row(), row(), row(), row(), row(),
            _const_spec((N_HEADS * c, c)), _const_spec((c, w)), _const_spec((w, w)),
        ],
        out_specs=[
            pl.BlockSpec((c, 3 * w), lambda b, i: (b * nt + i, 0)),
            pl.BlockSpec((1, 16, w), lambda b, i: (b, 0, 0)),
            pl.BlockSpec((1, w, w), lambda b, i: (b, 0, 0)),
            pl.BlockSpec((1, c, w), lambda b, i: (b, 0, 0)),
        ],
        out_shape=[
            jax.ShapeDtypeStruct((r, 3 * w), BF16),
            jax.ShapeDtypeStruct((n_batch, 16, w), F32),
            jax.ShapeDtypeStruct((n_batch, w, w), F32),
            jax.ShapeDtypeStruct((n_batch, c, w), F32),
        ],
        scratch_shapes=[pltpu.VMEM((16, w), F32), pltpu.VMEM((w, w), F32)],
        compiler_params=_params("arbitrary", "arbitrary"),
        name="mix_prompt",
    )(p, p, lp["pool_w"], lp["pool_scale"], lp["log_lb"], lp["log1m_lb"], lp["om_lb"], lp["hg_g"],
      lp["sg_g"], lp["sg_b"], lp["sg_w"], lp["sg_bias"], lp["eones"])


def _attn_p_kernel(cfar_ref, lam_ref, qt_ref, k_ref, vt_ref, near_ref, subg_ref, o_ref,
                   va_ref, m_ref, acc_ref, s_ref, *, tq, tk):
    h = pl.program_id(1)
    qi = pl.program_id(2)
    dh = D_HEAD

    @pl.when(qi == 0)
    def _():
        va_ref[0:dh, :] = vt_ref[0].astype(BF16)
        rows = lax.broadcasted_iota(jnp.int32, (V_AUG - dh, va_ref.shape[1]), 0)
        va_ref[dh:V_AUG, :] = jnp.where(rows == 0, 1.0, 0.0).astype(BF16)

    qt = qt_ref[0]
    sub = lax.broadcasted_iota(jnp.int32, qt.shape, 0)
    zero = jnp.zeros_like(qt)
    qm = (jnp.where(sub < DA_HALF, qt, zero), jnp.where(sub >= DA_HALF, qt, zero))
    m_ref[...] = jnp.full_like(m_ref, NEG)
    acc_ref[...] = jnp.zeros_like(acc_ref)

    def qk(ki, slot):
        k0 = pl.multiple_of(ki * tk, tk)
        kblk = k_ref[0, pl.ds(k0, tk), :]
        for mi in range(2):
            s_ref[slot, mi] = jnp.dot(kblk, qm[mi], preferred_element_type=F32)

    def softmax_pv(ki, slot, near, shift):
        k0 = pl.multiple_of(ki * tk, tk)
        vblk = va_ref[:, pl.ds(k0, tk)]
        for mi in range(2):
            s = s_ref[slot, mi]
            if near is not None:
                s = s + near_ref[0, near]
            m_blk = jnp.max(s, axis=0, keepdims=True)
            if shift is not None:
                m_blk = m_blk + shift
            m_old = m_ref[mi]
            m_new = jnp.maximum(m_old, m_blk)
            alpha = jnp.exp2(m_old - m_new)
            p = jnp.exp2(s - (m_new if shift is None else m_new - shift)).astype(BF16)
            acc_ref[mi] = alpha * acc_ref[mi] + jnp.dot(vblk, p, preferred_element_type=F32)
            m_ref[mi] = m_new

    cfar = cfar_ref[h]
    n_diag = tq // tk
    kb0 = qi * n_diag

    def near_tiles(first_tile, first_near, n):
        for j in range(n):
            if j + 1 < n:
                qk(first_tile + j + 1, (j + 1) % 2)
            softmax_pv(first_tile + j, j % 2, first_near + j, None)

    @pl.when(qi == 0)
    def _():
        qk(0, 0)
        near_tiles(0, 1, n_diag)

    @pl.when(qi >= 1)
    def _():
        n_far = kb0 - 1
        odd = n_far % 2

        @pl.when(odd == 1)
        def _():
            qk(0, 0)
            softmax_pv(0, 0, None, cfar)

        qk(odd, 0)

        def far_pair(pi, carry):
            b = odd + 2 * pi
            qk(b + 1, 1)
            softmax_pv(b, 0, None, cfar)
            qk(b + 2, 0)
            softmax_pv(b + 1, 1, None, cfar)
            return carry

        lax.fori_loop(0, (n_far - odd) // 2, far_pair, 0)
        near_tiles(kb0 - 1, 0, n_diag + 1)

    a0 = acc_ref[0]
    a1 = acc_ref[1]
    o = a0[0:dh] / a0[dh:dh + 1] - lam_ref[0] * (a1[0:dh] / a1[dh:dh + 1])
    ms = jnp.mean(o * o, axis=0, keepdims=True)
    o_ref[0] = (o * lax.rsqrt(ms + EPS) * subg_ref[...]).astype(BF16)


def _attn_tiles(t):
    tq = min(ATT_TQ, t)
    return tq, min(ATT_TK, tq)


def _attn_prompt(qt, k_rows, kvt, lp, n_batch):
    t = qt.shape[2]
    tq, tk = _attn_tiles(t)
    nq = t // tq
    n_near = tq // tk + 1
    smem = pl.BlockSpec(memory_space=pltpu.SMEM)
    return pl.pallas_call(
        functools.partial(_attn_p_kernel, tq=tq, tk=tk),
        grid=(n_batch, N_HEADS, nq),
        in_specs=[
            smem, smem,
            pl.BlockSpec((1, D_HEAD, tq), lambda b, h, i: (b, h, i)),
            pl.BlockSpec((1, t, D_HEAD), lambda b, h, i: (h, b, 0)),
            pl.BlockSpec((1, D_HEAD, t), lambda b, h, i: (b, N_HEADS + h, 0)),
            pl.BlockSpec((1, n_near, tk, tq), lambda b, h, i: (h, 0, 0, 0)),
            _const_spec((D_HEAD, 1)),
        ],
        out_specs=pl.BlockSpec((1, D_HEAD, tq), lambda b, h, i: (b, h, i)),
        out_shape=jax.ShapeDtypeStruct((n_batch, W_GROUP, t), BF16),
        scratch_shapes=[
            pltpu.VMEM((V_AUG, t), BF16),
            pltpu.VMEM((2, 1, tq), F32), pltpu.VMEM((2, V_AUG, tq), F32),
            pltpu.VMEM((2, 2, tk, tq), F32),
        ],
        compiler_params=_params("arbitrary", "arbitrary", "arbitrary"),
        name="attn_prompt",
    )(lp["c_far"], lp["lam"], qt, k_rows, kvt, lp["bias_near"], lp["sub_g_col"])


def _outproj_p_kernel(x_ref, y_ref, yct_ref, wab_ref, wc_ref, wd_ref, g_ref, b_ref, o_ref, *, alpha):
    w = W_GROUP
    acc = jnp.dot(y_ref[:, 0:2 * w], wab_ref[...], preferred_element_type=F32)
    acc = acc + jnp.dot(y_ref[:, 2 * w:3 * w], wd_ref[...], preferred_element_type=F32)
    yc = yct_ref[0].astype(F32).T.astype(BF16)
    acc = acc + jnp.dot(yc, wc_ref[...], preferred_element_type=F32)
    o_ref[...] = _layer_norm(alpha * x_ref[...] + acc, g_ref[...], b_ref[...])


def _outproj_prompt(x, y_abd, y_ct, w_out, layer, g, b, alpha):
    r, d = x.shape
    w = W_GROUP
    t = y_ct.shape[2]
    tm = min(ROW_TILE, t)
    tpb = t // tm
    return pl.pallas_call(
        functools.partial(_outproj_p_kernel, alpha=alpha),
        grid=(r // tm,),
        in_specs=[
            pl.BlockSpec((tm, d), lambda i: (i, 0)),
            pl.BlockSpec((tm, 3 * W_GROUP), lambda i: (i, 0)),
            pl.BlockSpec((1, W_GROUP, tm), lambda i: (i // tpb, 0, i % tpb)),
            _layer_spec(w_out.shape, layer, rows=2 * w, row_block=0),
            _layer_spec(w_out.shape, layer, rows=w, row_block=2),
            _layer_spec(w_out.shape, layer, rows=w, row_block=3),
            _const_spec((1, d)), _const_spec((1, d)),
        ],
        out_specs=pl.BlockSpec((tm, d), lambda i: (i, 0)),
        out_shape=jax.ShapeDtypeStruct((r, d), F32),
        compiler_params=_params("arbitrary"),
        name="outproj_prompt",
    )(x, y_abd, y_ct, w_out, w_out, w_out, g, b)


def _outproj_s_kernel(x_ref, ya_ref, ybt_ref, yct_ref, yd_ref, w_ref, g_ref, b_ref, o_ref, *, alpha):
    w = W_GROUP
    parts = (ya_ref[...], ybt_ref[...].T, yct_ref[...].T, yd_ref[...])
    acc = None
    for j, part in enumerate(parts):
        d = jnp.dot(part.astype(BF16), w_ref[j * w:(j + 1) * w, :], preferred_element_type=F32)
        acc = d if acc is None else acc + d
    o_ref[...] = _layer_norm(alpha * x_ref[...] + acc, g_ref[...], b_ref[...])


def _outproj_sample(x, y_a, y_bt, y_ct, y_d, w_out, layer, g, b, alpha):
    n, d = x.shape
    w = W_GROUP
    return pl.pallas_call(
        functools.partial(_outproj_s_kernel, alpha=alpha),
        grid=(1,),
        in_specs=[
            _const_spec((n, d)), _const_spec((n, w)), _const_spec((w, n)), _const_spec((w, n)),
            _const_spec((n, w)), _layer_spec(w_out.shape, layer), _const_spec((1, d)), _const_spec((1, d)),
        ],
        out_specs=_const_spec((n, d)),
        out_shape=jax.ShapeDtypeStruct((n, d), F32),
        compiler_params=_params("arbitrary"),
        name="outproj_sample",
    )(x, y_a, y_bt, y_ct, y_d, w_out, g, b)


def _mix_s_kernel(pa_ref, pd_ref, pool_ref, poolw_ref, pscale_ref, sgg_ref, sgb_ref, w00_ref, b0_ref,
                  ya_ref, yd_ref, newpool_ref, vn_ref, *, cnts):
    a = pa_ref[...]
    n, w = a.shape
    grp = lax.broadcasted_iota(jnp.int32, (n, w), 1) >> 6
    acc = a
    means = []
    for j in range(1, POOL_BUF + 1):
        acc = acc + pool_ref[POOL_BUF - j]
        if j + 1 in POOL_WINDOWS:
            means.append(acc / cnts[POOL_WINDOWS.index(j + 1)])
    dpool = _group_select(grp, means) - a
    ya_ref[...] = jnp.dot(dpool.astype(BF16), poolw_ref[...], preferred_element_type=F32) * pscale_ref[...]
    for j in range(POOL_BUF - 1):
        newpool_ref[j] = pool_ref[j + 1]
    newpool_ref[POOL_BUF - 1] = a

    vn = _layer_norm(pd_ref[:, w:2 * w], sgg_ref[...], sgb_ref[...])
    vn_ref[...] = vn
    yd_ref[...] = pd_ref[:, 0:w] * (w00_ref[...] * vn + b0_ref[...])


def _mix_sample(p, pool, lp, past_len):
    n = p.shape[0]
    w = W_GROUP
    cnts = tuple(float(min(past_len + 1, win)) for win in POOL_WINDOWS)
    row = lambda: _const_spec((1, w))
    return pl.pallas_call(
        functools.partial(_mix_s_kernel, cnts=cnts),
        grid=(1,),
        in_specs=[
            pl.BlockSpec((n, w), lambda i: (0, 0)),
            pl.BlockSpec((n, 2 * w), lambda i: (0, 4)),
            _const_spec(pool.shape), _const_spec((w, w)), row(), row(), row(), row(), row(),
        ],
        out_specs=[_const_spec((n, w)), _const_spec((n, w)), _const_spec(pool.shape), _const_spec((n, w))],
        out_shape=[
            jax.ShapeDtypeStruct((n, w), F32), jax.ShapeDtypeStruct((n, w), F32),
            jax.ShapeDtypeStruct(pool.shape, F32), jax.ShapeDtypeStruct((n, w), F32),
        ],
        compiler_params=_params("arbitrary"),
        name="mix_sample",
    )(p, p, pool, lp["pool_w"], lp["pool_scale"], lp["sg_g"], lp["sg_b"], lp["sg_w00"], lp["sg_b0"])


def _hgrn_s_kernel(q_ref, f_ref, i_ref, g_ref, s_ref, loglb_ref, log1mlb_ref, omlb_ref, hgg_ref,
                   so_ref, yb_ref):
    z = f_ref[...]
    f = jnp.exp(_log_forget(z, loglb_ref[...], log1mlb_ref[...]))
    kin = omlb_ref[...] * jax.nn.sigmoid(-z)
    q = q_ref[...]
    v = i_ref[...]
    o = jnp.zeros_like(v)
    for d in range(D_HEAD):
        sn = f[d:d + 1] * s_ref[0, d] + kin[d:d + 1] * v
        so_ref[0, d] = sn
        o = o + q[d:d + 1] * sn
    ms = jnp.mean(o * o, axis=0, keepdims=True)
    yb_ref[...] = o * lax.rsqrt(ms + EPS) * hgg_ref[...] * _silu(g_ref[...])


def _hgrn_sample(pt, state, lp):
    n = pt.shape[1]
    dh = D_HEAD
    blk = lambda off: pl.BlockSpec((dh, n), lambda h: (off * N_HEADS + h, 0))
    col = lambda: pl.BlockSpec((dh, 1), lambda h: (h, 0))
    return pl.pallas_call(
        _hgrn_s_kernel,
        grid=(N_HEADS,),
        in_specs=[
            blk(1), blk(2), blk(3), blk(4),
            pl.BlockSpec((1, dh, dh, n), lambda h: (h, 0, 0, 0)),
            col(), col(), col(), _const_spec((dh, 1)),
        ],
        out_specs=[
            pl.BlockSpec((1, dh, dh, n), lambda h: (h, 0, 0, 0)),
            pl.BlockSpec((dh, n), lambda h: (h, 0)),
        ],
        out_shape=[
            jax.ShapeDtypeStruct(state.shape, F32),
            jax.ShapeDtypeStruct((W_GROUP, n), F32),
        ],
        compiler_params=_params("arbitrary"),
        name="hgrn_sample",
    )(pt, pt, pt, pt, state, lp["log_lb_col"], lp["log1m_lb_col"], lp["om_lb_col"], lp["hg_g_col"])


def _attn_s_one(n, k_pages, v_pages, lam, qt_ref, knt_ref, vnt_ref, bias_ref, bself_ref, n_pages, q_scale):
    w, ns = qt_ref.shape
    dh = D_HEAD
    sel = lax.broadcasted_iota(jnp.int32, (w, ns), 1) == n

    def column(ref):
        return jnp.sum(jnp.where(sel, ref[...], 0.0), axis=1, keepdims=True)

    qcol = column(qt_ref) * q_scale
    kcol = column(knt_ref)
    vcol = column(vnt_ref)

    n_grp = 2 * N_HEADS
    rows = ([], [])
    self_rows = ([], [])
    for h in range(N_HEADS):
        qh = qcol[h * dh:(h + 1) * dh]
        for j in range(n_pages):
            prod = k_pages[j][h] * qh
            rows[0].append(jnp.sum(prod[0:DA_HALF], axis=0, keepdims=True))
            rows[1].append(jnp.sum(prod[DA_HALF:dh], axis=0, keepdims=True))
        self_prod = qh * kcol[h * dh:(h + 1) * dh]
        for mi in range(2):
            s_self = (jnp.sum(self_prod[mi * DA_HALF:(mi + 1) * DA_HALF], axis=0, keepdims=True)
                      + bself_ref[h])
            self_rows[mi].append(jnp.broadcast_to(s_self, (n_pages, 1)))
    s = jnp.concatenate(rows[0] + rows[1], axis=0) + bias_ref[...]
    s_self = jnp.concatenate(self_rows[0] + self_rows[1], axis=0)

    def per_group(col, reduce):
        parts = [jnp.broadcast_to(reduce(col[g * n_pages:(g + 1) * n_pages], axis=0, keepdims=True),
                                  (n_pages, 1)) for g in range(n_grp)]
        return jnp.concatenate(parts, axis=0)

    m = jnp.maximum(per_group(jnp.max(s, axis=1, keepdims=True), jnp.max), s_self)
    p = jnp.exp(s - m)
    p_self = jnp.exp(s_self - m)
    inv_l = 1.0 / (per_group(jnp.sum(p, axis=1, keepdims=True), jnp.sum) + p_self)
    half = N_HEADS * n_pages
    a = (p * inv_l)[0:half] - lam * (p * inv_l)[half:2 * half]
    a_self = (p_self * inv_l)[0:half] - lam * (p_self * inv_l)[half:2 * half]
    weighted = []
    for h in range(N_HEADS):
        oh = v_pages[0][h] * a[h * n_pages:h * n_pages + 1]
        for j in range(1, n_pages):
            oh = oh + v_pages[j][h] * a[h * n_pages + j:h * n_pages + j + 1]
        weighted.append(oh)
    a_self_col = jnp.concatenate(
        [jnp.broadcast_to(a_self[h * n_pages:h * n_pages + 1], (dh, 1)) for h in range(N_HEADS)], axis=0)
    ocol = jnp.sum(jnp.concatenate(weighted, axis=0), axis=1, keepdims=True) + a_self_col * vcol
    return sel, ocol


def _attn_s_kernel(pt_ref, lam_ref, qt_ref, knt_ref, vnt_ref, bias_ref, bself_ref, subg_ref, *rest,
                   n_pages, sps, q_scale):
    o_ref = rest[2 * sps * n_pages]
    ot_ref = rest[2 * sps * n_pages + 1]
    del pt_ref
    dh = D_HEAD
    step = pl.program_id(0)
    ot = ot_ref[...]
    for u in range(sps):
        sel, ocol = _attn_s_one(step * sps + u, rest[u * n_pages:(u + 1) * n_pages],
                                rest[(sps + u) * n_pages:(sps + u + 1) * n_pages], lam_ref[0],
                                qt_ref, knt_ref, vnt_ref, bias_ref, bself_ref, n_pages, q_scale)
        ot = jnp.where(sel, ocol, ot)
    ot_ref[...] = ot

    @pl.when(step == pl.num_programs(0) - 1)
    def _():
        for h in range(N_HEADS):
            oh = ot_ref[h * dh:(h + 1) * dh, :]
            ms = jnp.mean(oh * oh, axis=0, keepdims=True)
            o_ref[h * dh:(h + 1) * dh, :] = oh * lax.rsqrt(ms + EPS) * subg_ref[...]


def _attn_sample(pt, cache_kt, cache_vt, page_table, layer, lp):
    n = pt.shape[1]
    n_pages = page_table.shape[1]
    w = W_GROUP
    dh = D_HEAD
    blk = lambda off: pl.BlockSpec((w, n), lambda i, tbl: (off, 0))
    sps = DEC_SAMPLES_PER_STEP
    page_specs = [
        pl.BlockSpec((None, None, N_HEADS, dh, PAGE),
                     functools.partial(lambda i, tbl, u, j: (layer, tbl[i * sps + u, j], 0, 0, 0), u=u, j=j))
        for u in range(sps) for j in range(n_pages)
    ]
    grid_spec = pltpu.PrefetchScalarGridSpec(
        num_scalar_prefetch=1,
        grid=(n // sps,),
        in_specs=[
            pl.BlockSpec(memory_space=pltpu.SMEM),
            blk(5), blk(6), blk(7),
            pl.BlockSpec((2 * N_HEADS * n_pages, PAGE), lambda i, tbl: (0, 0)),
            pl.BlockSpec((N_HEADS, 1, 1), lambda i, tbl: (0, 0, 0)),
            pl.BlockSpec((dh, 1), lambda i, tbl: (0, 0)),
        ] + page_specs + page_specs,
        out_specs=pl.BlockSpec((w, n), lambda i, tbl: (0, 0)),
        scratch_shapes=[pltpu.VMEM((w, n), F32)],
    )
    return pl.pallas_call(
        functools.partial(_attn_s_kernel, n_pages=n_pages, sps=sps, q_scale=DA_HALF ** -0.5),
        grid_spec=grid_spec,
        out_shape=jax.ShapeDtypeStruct((w, n), F32),
        compiler_params=_params("arbitrary"),
        name="attn_sample",
    )(page_table, lp["lam"], pt, pt, pt, lp["bias_past"], lp["bias_self"], lp["sub_g_col"],
      *([cache_kt] * (sps * n_pages)), *([cache_vt] * (sps * n_pages)))


def _rel_bucket(dist):
    n = jnp.maximum(dist, 0)
    max_exact = REL_BUCKETS // 2
    large = max_exact + (jnp.log(jnp.maximum(n, 1).astype(F32) / max_exact)
                         / math.log(REL_MAX_DIST / max_exact) * (REL_BUCKETS - max_exact)).astype(jnp.int32)
    large = jnp.minimum(large, REL_BUCKETS - 1)
    return jnp.where(n < max_exact, n, large)


def _layer_params(l, depth, tq, tk, past_len, n_pages, prm, lb_all):
    w = W_GROUP
    lam_init = 0.8 - 0.6 * math.exp(-0.3 * l)
    lam = (jnp.exp(jnp.sum(prm["diff_lam_q1"][l] * prm["diff_lam_k1"][l]))
           - jnp.exp(jnp.sum(prm["diff_lam_q2"][l] * prm["diff_lam_k2"][l])) + lam_init)
    rel_bias = prm["rel_bias"]

    def bias_of(dist):
        onehot = _rel_bucket(dist)[..., None] == jnp.arange(REL_BUCKETS, dtype=jnp.int32)
        table = rel_bias.T.reshape((N_HEADS,) + (1,) * dist.ndim + (REL_BUCKETS,))
        return jnp.sum(jnp.where(onehot[None], table, 0.0), axis=-1)

    span = tq + tk - 1
    jj = jnp.arange(span + 1, dtype=jnp.int32)
    tiles = []
    for r in range(-1, tq // tk):
        dist = jj - (tk - 1) - r * tk
        f = jnp.where((dist >= 0)[None], bias_of(dist), NEG)
        sheared = jnp.tile(f, (1, tk))[:, :tk * span].reshape(N_HEADS, tk, span)
        tiles.append(sheared[:, :, tk - 1:tk - 1 + tq])
    bias_near = jnp.stack(tiles, axis=1) * LOG2E
    kpos = jnp.arange(n_pages * PAGE, dtype=jnp.int32)
    bias_past = jnp.tile(bias_of(past_len - kpos).reshape(N_HEADS * n_pages, PAGE), (2, 1))
    lb = lb_all[l]
    tril = jnp.tril(jnp.ones((SG_CHUNK, SG_CHUNK), F32))
    sub_g = prm["diff_subln_g"][l] * (1.0 - lam_init)
    return {
        "ln_g": prm["ln_g"][l], "ln_b": prm["ln_b"][l],
        "pool_w": jax.scipy.linalg.block_diag(*prm["pool_w"][l]).astype(BF16),
        "pool_scale": prm["pool_scale"][l][None],
        "log_lb": jnp.log(lb)[None], "log1m_lb": jnp.log1p(-lb)[None], "om_lb": (1.0 - lb)[None],
        "log_lb_col": jnp.log(lb)[:, None], "log1m_lb_col": jnp.log1p(-lb)[:, None],
        "om_lb_col": (1.0 - lb)[:, None],
        "hg_g": jnp.tile(prm["hgrn_norm_g"][l], N_HEADS)[None], "hg_g_col": prm["hgrn_norm_g"][l][:, None],
        "sg_g": prm["sgu_ln_g"][l][None], "sg_b": prm["sgu_ln_b"][l][None],
        "sg_w": (prm["sgu_w"][l] * tril).reshape(N_HEADS * SG_CHUNK, SG_CHUNK).astype(BF16),
        "sg_bias": jnp.repeat(prm["sgu_b"][l].T, D_HEAD, axis=1),
        "sg_w00": jnp.repeat(prm["sgu_w"][l][:, 0, 0], D_HEAD)[None],
        "sg_b0": jnp.repeat(prm["sgu_b"][l][:, 0], D_HEAD)[None],
        "eones": jnp.kron(jnp.eye(N_HEADS, dtype=F32), jnp.ones((D_HEAD, D_HEAD), F32)).astype(BF16),
        "lam": lam.reshape(1), "c_far": rel_bias[REL_BUCKETS - 1] * LOG2E,
        "bias_near": bias_near,
        "bias_past": bias_past, "bias_self": rel_bias[0].reshape(N_HEADS, 1, 1),
        "sub_g_col": sub_g[:, None],
    }


def kernel(x_prompt, x_sample, state_pool, state_hgrn, cache_k, cache_v, page_table, rel_bias, ln_g, ln_b,
           ffn1_w_gu, ffn1_w_dn, ffn2_w_gu, ffn2_w_dn, w_in, w_out, pool_w, pool_scale, hgrn_lb,
           hgrn_norm_g, diff_lam_q1, diff_lam_k1, diff_lam_q2, diff_lam_k2, diff_subln_g, sgu_ln_g,
           sgu_ln_b, sgu_w, sgu_b):
    prm = dict(rel_bias=rel_bias, ln_g=ln_g, ln_b=ln_b, ffn1_w_gu=ffn1_w_gu, ffn1_w_dn=ffn1_w_dn,
               ffn2_w_gu=ffn2_w_gu, ffn2_w_dn=ffn2_w_dn, w_in=w_in, w_out=w_out, pool_w=pool_w,
               pool_scale=pool_scale, hgrn_norm_g=hgrn_norm_g, diff_lam_q1=diff_lam_q1,
               diff_lam_k1=diff_lam_k1, diff_lam_q2=diff_lam_q2, diff_lam_k2=diff_lam_k2,
               diff_subln_g=diff_subln_g, sgu_ln_g=sgu_ln_g, sgu_ln_b=sgu_ln_b, sgu_w=sgu_w, sgu_b=sgu_b)
    depth = w_in.shape[0]
    nb, t, d = x_prompt.shape
    ns = x_sample.shape[0]
    n_pages = page_table.shape[1]
    past_len = n_pages * PAGE
    alpha = (2.0 * depth) ** 0.25
    w = W_GROUP
    tq, tk = _attn_tiles(t)
    assert x_sample.shape[1] == 1 and t % SG_CHUNK == 0 and t % tq == 0 and tk >= REL_MAX_DIST

    lb_cum = jnp.cumsum(jax.nn.softmax(hgrn_lb.astype(F32), axis=0), axis=0)
    lb_all = jnp.maximum(lb_cum - lb_cum[:1], 0.0)

    cache_kt = jnp.transpose(cache_k, (0, 1, 3, 4, 2))
    cache_vt = jnp.transpose(cache_v, (0, 1, 3, 4, 2))
    hgrn_t = jnp.transpose(state_hgrn, (0, 2, 3, 4, 1))
    pool_t = jnp.transpose(state_pool, (0, 2, 1, 3))

    wb = {k: prm[k].astype(BF16) for k in ("ffn1_w_gu", "ffn1_w_dn", "ffn2_w_gu", "ffn2_w_dn", "w_in", "w_out")}
    w_int = jnp.swapaxes(prm["w_in"], 1, 2).astype(BF16)
    w_qkvt = w_int[:, 5 * w:8 * w]

    xp = x_prompt.reshape(nb * t, d)
    xs = x_sample.reshape(ns, d)
    outs = {k: [] for k in ("pool_p", "pool_s", "hgrn_p", "hgrn_s", "k_p", "k_s", "v_p", "v_s", "sgv_p", "sgv_s")}
    for l in range(depth):
        lp = _layer_params(l, depth, tq, tk, past_len, n_pages, prm, lb_all)
        g = [lp["ln_g"][i][None] for i in range(3)]
        b = [lp["ln_b"][i][None] for i in range(3)]

        xp = _ffn(xp, wb["ffn1_w_gu"], wb["ffn1_w_dn"], l, g[0], b[0], alpha)
        p, k_rows, q_t, kvt = _inproj_prompt(xp, wb["w_in"], w_qkvt, l, nb)
        y_abd, pool16, st, sgv = _mix_prompt(p, nb, lp)
        y_ct = _attn_prompt(q_t, k_rows, kvt, lp, nb)
        xp = _outproj_prompt(xp, y_abd, y_ct, wb["w_out"], l, g[1], b[1], alpha)
        xp = _ffn(xp, wb["ffn2_w_gu"], wb["ffn2_w_dn"], l, g[2], b[2], alpha)
        kvt5 = kvt.reshape(nb, 2, N_HEADS, D_HEAD, t)
        outs["k_p"].append(jnp.transpose(kvt5[:, 0], (0, 3, 1, 2)))
        outs["v_p"].append(jnp.transpose(kvt5[:, 1], (0, 3, 1, 2)))
        outs["pool_p"].append(pool16[:, 1:])
        outs["hgrn_p"].append(jnp.stack(
            [jnp.swapaxes(st[:, h * D_HEAD:(h + 1) * D_HEAD, h * D_HEAD:(h + 1) * D_HEAD], 1, 2)
             for h in range(N_HEADS)], axis=1))
        outs["sgv_p"].append(sgv)

        xs = _ffn(xs, wb["ffn1_w_gu"], wb["ffn1_w_dn"], l, g[0], b[0], alpha)
        ps, pst = _inproj_sample(xs, wb["w_in"], w_int, l)
        y_a, y_d, new_pool, vn = _mix_sample(ps, pool_t[l], lp, past_len)
        new_state, y_bt = _hgrn_sample(pst, hgrn_t[l], lp)
        y_ct = _attn_sample(pst, cache_kt, cache_vt, page_table, l, lp)
        xs = _outproj_sample(xs, y_a, y_bt, y_ct, y_d, wb["w_out"], l, g[1], b[1], alpha)
        xs = _ffn(xs, wb["ffn2_w_gu"], wb["ffn2_w_dn"], l, g[2], b[2], alpha)
        outs["k_s"].append(jnp.transpose(pst[6 * w:7 * w].reshape(N_HEADS, D_HEAD, ns), (2, 0, 1))[:, None])
        outs["v_s"].append(jnp.transpose(pst[7 * w:8 * w].reshape(N_HEADS, D_HEAD, ns), (2, 0, 1))[:, None])
        outs["pool_s"].append(jnp.transpose(new_pool, (1, 0, 2)))
        outs["hgrn_s"].append(jnp.transpose(new_state, (3, 0, 1, 2)))
        outs["sgv_s"].append(vn[:, None])

    st = {k: jnp.stack(v, axis=0) for k, v in outs.items()}
    return (xp.reshape(nb, t, d), xs.reshape(ns, 1, d), st["pool_p"], st["pool_s"], st["hgrn_p"], st["hgrn_s"],
            st["k_p"], st["k_s"], st["v_p"], st["v_s"], st["sgv_p"], st["sgv_s"])
```

```python
import functools
import math

import jax
import jax.numpy as jnp
import jax.scipy.linalg
from jax import lax
from jax.experimental import pallas as pl
from jax.experimental.pallas import tpu as pltpu

F32 = jnp.float32
BF16 = jnp.bfloat16

N_MIX = 4
W_GROUP = 256
N_HEADS = 4
D_HEAD = 64
DA_HALF = 32
POOL_WINDOWS = (2, 4, 8, 16)
POOL_BUF = 15
PAGE = 128
SG_CHUNK = 128
HG_SUB = 16
V_AUG = 80
REL_BUCKETS = 32
REL_MAX_DIST = 128
EPS = 1e-5
NEG = -1e30
LOG2E = 1.0 / math.log(2.0)

VMEM_LIMIT = 56 * 1024 * 1024
MXU_DIM = 256
ROW_TILE = 512
ATT_TQ = 1024
ATT_TK = 512
DEC_SAMPLES_PER_STEP = 2

NT_DIMS = (((1,), (1,)), ((), ()))


def _params(*sem):
    return pltpu.CompilerParams(dimension_semantics=sem, vmem_limit_bytes=VMEM_LIMIT)


def _const_spec(shape, single=False):
    nd = len(shape)
    kw = {"pipeline_mode": pl.Buffered(1)} if single else {}
    return pl.BlockSpec(shape, lambda *_: (0,) * nd, **kw)


def _layer_spec(shape, layer, single=False, rows=None, row_block=0):
    kw = {"pipeline_mode": pl.Buffered(1)} if single else {}
    block = (None, shape[1] if rows is None else rows) + tuple(shape[2:])
    return pl.BlockSpec(block, lambda *_: (layer, row_block) + (0,) * (len(shape) - 2), **kw)


def _layer_norm(y, g, b):
    mu = jnp.mean(y, axis=-1, keepdims=True)
    yc = y - mu
    var = jnp.mean(yc * yc, axis=-1, keepdims=True)
    return yc * lax.rsqrt(var + EPS) * g + b


def _silu(x):
    return x * jax.nn.sigmoid(x)


def _log_forget(z, log_lb, log1m_lb):
    log_sig = -(jnp.maximum(-z, 0.0) + jnp.log1p(jnp.exp(-jnp.abs(z))))
    b = log1m_lb + log_sig
    return jnp.maximum(log_lb, b) + jnp.log1p(jnp.exp(-jnp.abs(log_lb - b)))


def _ffn_kernel(x_ref, wgu_ref, wdn_ref, g_ref, b_ref, o_ref, *, d_ff, bounds, alpha):
    x = x_ref[...]
    xb = x.astype(BF16)
    acc = None
    for lo, hi in zip(bounds[:-1], bounds[1:]):
        gate = jnp.dot(xb, wgu_ref[:, lo:hi], preferred_element_type=F32)
        up = jnp.dot(xb, wgu_ref[:, d_ff + lo:d_ff + hi], preferred_element_type=F32)
        h = (_silu(gate) * up).astype(BF16)
        part = jnp.dot(h, wdn_ref[lo:hi, :], preferred_element_type=F32)
        acc = part if acc is None else acc + part
    o_ref[...] = _layer_norm(alpha * x + 0.5 * acc, g_ref[...], b_ref[...])


def _ffn(x, w_gu, w_dn, layer, g, b, alpha):
    r, d = x.shape
    d_ff = w_dn.shape[1]
    tm = min(ROW_TILE, r)
    n_tiles = d_ff // MXU_DIM
    bounds = (0, (n_tiles + 1) // 2 * MXU_DIM, d_ff)
    return pl.pallas_call(
        functools.partial(_ffn_kernel, d_ff=d_ff, bounds=bounds, alpha=alpha),
        grid=(r // tm,),
        in_specs=[
            pl.BlockSpec((tm, d), lambda i: (i, 0)),
            _layer_spec(w_gu.shape, layer, single=True),
            _layer_spec(w_dn.shape, layer, single=True),
            _const_spec((1, d)),
            _const_spec((1, d)),
        ],
        out_specs=pl.BlockSpec((tm, d), lambda i: (i, 0)),
        out_shape=jax.ShapeDtypeStruct((r, d), F32),
        compiler_params=_params("arbitrary"),
        name="ffn",
    )(x, w_gu, w_dn, g, b)


def _inproj_p_kernel(x_ref, w_ref, wqkvt_ref, p_ref, k_ref, qt_ref, kvt_ref, *, q_scale):
    xb = x_ref[...].astype(BF16)
    p = jnp.dot(xb, w_ref[...], preferred_element_type=F32)
    p_ref[...] = p
    for h in range(N_HEADS):
        k_ref[h] = p[:, 6 * W_GROUP + h * D_HEAD:6 * W_GROUP + (h + 1) * D_HEAD].astype(BF16)
    qkvt = lax.dot_general(wqkvt_ref[...], xb, NT_DIMS, preferred_element_type=F32)
    qt_ref[0] = (qkvt[0:W_GROUP] * q_scale).astype(BF16)
    kvt_ref[0] = qkvt[W_GROUP:3 * W_GROUP]


def _inproj_prompt(x, w_in, w_qkvt, layer, n_batch):
    r, d = x.shape
    t = r // n_batch
    d_in = w_in.shape[2]
    tm = min(ROW_TILE, t)
    tpb = t // tm
    return pl.pallas_call(
        functools.partial(_inproj_p_kernel, q_scale=DA_HALF ** -0.5 * LOG2E),
        grid=(r // tm,),
        in_specs=[
            pl.BlockSpec((tm, d), lambda i: (i, 0)),
            _layer_spec(w_in.shape, layer),
            _layer_spec(w_qkvt.shape, layer),
        ],
        out_specs=[
            pl.BlockSpec((tm, d_in), lambda i: (i, 0)),
            pl.BlockSpec((N_HEADS, tm, D_HEAD), lambda i: (0, i, 0)),
            pl.BlockSpec((1, W_GROUP, tm), lambda i: (i // tpb, 0, i % tpb)),
            pl.BlockSpec((1, 2 * W_GROUP, tm), lambda i: (i // tpb, 0, i % tpb)),
        ],
        out_shape=[
            jax.ShapeDtypeStruct((r, d_in), F32),
            jax.ShapeDtypeStruct((N_HEADS, r, D_HEAD), BF16),
            jax.ShapeDtypeStruct((n_batch, W_GROUP, t), BF16),
            jax.ShapeDtypeStruct((n_batch, 2 * W_GROUP, t), F32),
        ],
        compiler_params=_params("arbitrary"),
        name="inproj_prompt",
    )(x, w_in, w_qkvt)


def _inproj_s_kernel(x_ref, w_ref, wt_ref, p_ref, pt_ref):
    xb = x_ref[...].astype(BF16)
    p_ref[...] = jnp.dot(xb, w_ref[...], preferred_element_type=F32)
    pt_ref[...] = lax.dot_general(wt_ref[...], xb, NT_DIMS, preferred_element_type=F32)


def _inproj_sample(x, w_in, w_int, layer):
    n, d = x.shape
    d_in = w_in.shape[2]
    return pl.pallas_call(
        _inproj_s_kernel,
        grid=(1,),
        in_specs=[_const_spec((n, d)), _layer_spec(w_in.shape, layer), _layer_spec(w_int.shape, layer)],
        out_specs=[_const_spec((n, d_in)), _const_spec((d_in, n))],
        out_shape=[jax.ShapeDtypeStruct((n, d_in), F32), jax.ShapeDtypeStruct((d_in, n), F32)],
        compiler_params=_params("arbitrary"),
        name="inproj_sample",
    )(x, w_in, w_int)


def _group_select(grp, parts):
    out = parts[N_HEADS - 1]
    for g in range(N_HEADS - 2, -1, -1):
        out = jnp.where(grp == g, parts[g], out)
    return out


def _mix_p_kernel(pa_ref, pd_ref, poolw_ref, pscale_ref, loglb_ref, log1mlb_ref, omlb_ref, hgg_ref,
                  sgg_ref, sgb_ref, sgw_ref, sgbias_ref, eones_ref,
                  y_ref, pool_ref, st_out_ref, sgv_ref,
                  prev_ref, st_ref):
    c = SG_CHUNK
    w = W_GROUP
    t = pl.program_id(1)

    @pl.when(t == 0)
    def _():
        prev_ref[...] = jnp.zeros_like(prev_ref)
        st_ref[...] = jnp.zeros_like(st_ref)

    a = pa_ref[:, 0:w]
    hq = pa_ref[:, w:2 * w]
    hf = pa_ref[:, 2 * w:3 * w]
    hi = pa_ref[:, 3 * w:4 * w]
    hg = pa_ref[:, 4 * w:5 * w]
    su = pd_ref[:, 0:w]
    sv = pd_ref[:, w:2 * w]
    lane = lax.broadcasted_iota(jnp.int32, (c, w), 1)
    row = lax.broadcasted_iota(jnp.int32, (c, w), 0)
    grp = lane >> 6

    e = jnp.concatenate([prev_ref[...], a], axis=0)
    s2 = e + pltpu.roll(e, 1, 0)
    s4 = s2 + pltpu.roll(s2, 2, 0)
    s8 = s4 + pltpu.roll(s4, 4, 0)
    s16 = s8 + pltpu.roll(s8, 8, 0)
    wsum = _group_select(grp, [s2[16:], s4[16:], s8[16:], s16[16:]])
    win = _group_select(grp, [jnp.full((c, w), v, jnp.int32) for v in POOL_WINDOWS])
    cnt = jnp.minimum(t * c + row + 1, win).astype(F32)
    dpool = wsum / cnt - a
    ya = jnp.dot(dpool.astype(BF16), poolw_ref[...], preferred_element_type=F32) * pscale_ref[...]
    prev_ref[...] = a[c - 16:]
    pool_ref[0] = a[c - 16:]

    vn = _layer_norm(sv, sgg_ref[...], sgb_ref[...])
    sg = jnp.dot(sgw_ref[...], vn.astype(BF16), preferred_element_type=F32)
    s_gate = _group_select(grp, [sg[g * c:(g + 1) * c] for g in range(N_HEADS)]) + sgbias_ref[...]
    yd = su * s_gate
    sgv_ref[0] = vn

    n_sub = c // HG_SUB
    logf = _log_forget(hf, loglb_ref[...], log1mlb_ref[...])
    kin = omlb_ref[...] * jax.nn.sigmoid(-hf)
    r16 = row & (HG_SUB - 1)
    bl = logf
    rv = logf
    for sh in (1, 2, 4, 8):
        bl = bl + jnp.where(r16 >= sh, pltpu.roll(bl, sh, 0), 0.0)
        rv = rv + jnp.where(r16 + sh < HG_SUB, pltpu.roll(rv, c - sh, 0), 0.0)
    sub_row = row >> 4
    qtb = (hq * jnp.exp(bl)).astype(BF16)
    kt = kin * jnp.exp(rv - logf)
    dec = jnp.exp(rv)
    vtb = hi.T.astype(BF16)
    bi0 = lax.broadcasted_iota(jnp.int32, (w, w), 0) >> 6
    bi1 = lax.broadcasted_iota(jnp.int32, (w, w), 1) >> 6
    blockmask = bi0 == bi1
    st = st_ref[...]
    seen = []
    for i in range(n_sub):
        seen.append(st.astype(BF16))
        km = jnp.where(sub_row == i, kt, 0.0).astype(BF16)
        u = jnp.dot(vtb, km, preferred_element_type=F32)
        st = dec[i * HG_SUB:i * HG_SUB + 1] * st + jnp.where(blockmask, u, 0.0)
    st_ref[...] = st
    st_out_ref[0] = st
    qx = jnp.concatenate([jnp.where(sub_row == i, qtb, jnp.zeros_like(qtb)) for i in range(n_sub)], axis=1)
    o = lax.dot_general(qx, jnp.concatenate(seen, axis=1), NT_DIMS,
                        preferred_element_type=F32)

    rt = {lo: lax.broadcasted_iota(jnp.int32, (HG_SUB - lo, w), 0) + lo for lo in (0, 8)}
    xs = []
    for i in range(n_sub):
        rows = slice(i * HG_SUB, (i + 1) * HG_SUB)
        bli, qi, ki = bl[rows], hq[rows], kin[rows]
        for s in range(HG_SUB):
            lo = (s // 8) * 8
            ratio = jnp.exp(jnp.minimum(bli[lo:] - bli[s:s + 1], 0.0))
            x_s = jnp.where(rt[lo] >= s, ratio * qi[lo:] * ki[s:s + 1], 0.0)
            if lo:
                x_s = jnp.concatenate([jnp.zeros((lo, w), F32), x_s], axis=0)
            xs.append(x_s.astype(BF16))
    x = jnp.concatenate(xs, axis=0)
    r = jnp.dot(x, eones_ref[...], preferred_element_type=F32)
    o_intra = []
    for i in range(n_sub):
        vi = hi[i * HG_SUB:(i + 1) * HG_SUB]
        base = i * HG_SUB * HG_SUB
        oi = r[base:base + HG_SUB] * vi[0:1]
        for s in range(1, HG_SUB):
            oi = oi + r[base + s * HG_SUB:base + (s + 1) * HG_SUB] * vi[s:s + 1]
        o_intra.append(oi)
    o = o + jnp.concatenate(o_intra, axis=0)
    ms = [jnp.mean(jnp.square(o[:, g * D_HEAD:(g + 1) * D_HEAD]), axis=-1, keepdims=True)
          for g in range(N_HEADS)]
    yb = o * lax.rsqrt(_group_select(grp, ms) + EPS) * hgg_ref[...] * _silu(hg)

    y_ref[:, 0:w] = ya.astype(BF16)
    y_ref[:, w:2 * w] = yb.astype(BF16)
    y_ref[:, 2 * w:3 * w] = yd.astype(BF16)


def _mix_prompt(p, n_batch, lp):
    r = p.shape[0]
    t = r // n_batch
    c = SG_CHUNK
    nt = t // c
    w = W_GROUP
    row = lambda: _const_spec((1, w))
    return pl.pallas_call(
        _mix_p_kernel,
        grid=(n_batch, nt),
        in_specs=[
            pl.BlockSpec((c, 5 * w), lambda b, i: (b * nt + i, 0)),
            pl.BlockSpec((c, 2 * w), lambda b, i: (b * nt + i, 4)),
            _const_spec((w, w)), row(), row(), row(), row(), row(), row(), row(),
            _const_spec((N_HEADS * c, c)), _const_spec((c, w)), _const_spec((w, w)),
        ],
        out_specs=[
            pl.BlockSpec((c, 3 * w), lambda b, i: (b * nt + i, 0)),
            pl.BlockSpec((1, 16, w), lambda b, i: (b, 0, 0)),
            pl.BlockSpec((1, w, w), lambda b, i: (b, 0, 0)),
            pl.BlockSpec((1, c, w), lambda b, i: (b, 0, 0)),
        ],
        out_shape=[
            jax.ShapeDtypeStruct((r, 3 * w), BF16),
            jax.ShapeDtypeStruct((n_batch, 16, w), F32),
            jax.ShapeDtypeStruct((n_batch, w, w), F32),
            jax.ShapeDtypeStruct((n_batch, c, w), F32),
        ],
        scratch_shapes=[pltpu.VMEM((16, w), F32), pltpu.VMEM((w, w), F32)],
        compiler_params=_params("arbitrary", "arbitrary"),
        name="mix_prompt",
    )(p, p, lp["pool_w"], lp["pool_scale"], lp["log_lb"], lp["log1m_lb"], lp["om_lb"], lp["hg_g"],
      lp["sg_g"], lp["sg_b"], lp["sg_w"], lp["sg_bias"], lp["eones"])


def _attn_p_kernel(cfar_ref, lam_ref, qt_ref, k_ref, vt_ref, near_ref, subg_ref, o_ref,
                   va_ref, m_ref, acc_ref, s_ref, *, tq, tk):
    h = pl.program_id(1)
    qi = pl.program_id(2)
    dh = D_HEAD

    @pl.when(qi == 0)
    def _():
        va_ref[0:dh, :] = vt_ref[0].astype(BF16)
        rows = lax.broadcasted_iota(jnp.int32, (V_AUG - dh, va_ref.shape[1]), 0)
        va_ref[dh:V_AUG, :] = jnp.where(rows == 0, 1.0, 0.0).astype(BF16)

    qt = qt_ref[0]
    sub = lax.broadcasted_iota(jnp.int32, qt.shape, 0)
    zero = jnp.zeros_like(qt)
    qm = (jnp.where(sub < DA_HALF, qt, zero), jnp.where(sub >= DA_HALF, qt, zero))
    m_ref[...] = jnp.full_like(m_ref, NEG)
    acc_ref[...] = jnp.zeros_like(acc_ref)

    def qk(ki, slot, q0=0):
        k0 = pl.multiple_of(ki * tk, tk)
        kblk = k_ref[0, pl.ds(k0, tk), :]
        for mi in range(2):
            s_ref[slot, mi, :, q0:] = jnp.dot(kblk, qm[mi][:, q0:], preferred_element_type=F32)

    def softmax_pv(ki, slot, near, shift, q0=0):
        k0 = pl.multiple_of(ki * tk, tk)
        vblk = va_ref[:, pl.ds(k0, tk)]
        for mi in range(2):
            s = s_ref[slot, mi, :, q0:]
            if near is not None:
                s = s + near_ref[0, near, :, q0:]
            m_blk = jnp.max(s, axis=0, keepdims=True)
            if shift is not None:
                m_blk = m_blk + shift
            m_old = m_ref[mi, :, q0:]
            m_new = jnp.maximum(m_old, m_blk)
            alpha = jnp.exp2(m_old - m_new)
            p = jnp.exp2(s - (m_new if shift is None else m_new - shift)).astype(BF16)
            acc_ref[mi, :, q0:] = alpha * acc_ref[mi, :, q0:] + jnp.dot(vblk, p, preferred_element_type=F32)
            m_ref[mi, :, q0:] = m_new

    cfar = cfar_ref[h]
    n_diag = tq // tk
    kb0 = qi * n_diag

    def near_tiles(first_tile, first_near, n, slot0):
        q0s = [max(first_near + j - 1, 0) * tk for j in range(n)]
        for j in range(n):
            if j + 1 < n:
                qk(first_tile + j + 1, (slot0 + j + 1) % 2, q0s[j + 1])
            softmax_pv(first_tile + j, (slot0 + j) % 2, first_near + j, None, q0s[j])

    def far_pair(pi, carry):
        b = 2 * pi
        qk(b + 1, 1)
        softmax_pv(b, 0, None, cfar)
        qk(b + 2, 0)
        softmax_pv(b + 1, 1, None, cfar)
        return carry

    @pl.when(qi == 0)
    def _():
        qk(0, 0)
        near_tiles(0, 1, n_diag, 0)

    @pl.when(qi >= 1)
    def _():
        n_far = kb0 - 1
        qk(0, 0)
        lax.fori_loop(0, n_far // 2, far_pair, 0)

        def tail(odd):
            if odd:
                qk(n_far, 1)
                softmax_pv(n_far - 1, 0, None, cfar)
            near_tiles(n_far, 0, n_diag + 1, 1 if odd else 0)

        if n_diag % 2 == 0:
            tail(True)
        else:
            pl.when(n_far % 2 == 1)(lambda: tail(True))
            pl.when(n_far % 2 == 0)(lambda: tail(False))

    a0 = acc_ref[0]
    a1 = acc_ref[1]
    o = a0[0:dh] / a0[dh:dh + 1] - lam_ref[0] * (a1[0:dh] / a1[dh:dh + 1])
    ms = jnp.mean(o * o, axis=0, keepdims=True)
    o_ref[0] = (o * lax.rsqrt(ms + EPS) * subg_ref[...]).astype(BF16)


def _attn_tiles(t):
    tq = min(ATT_TQ, t)
    return tq, min(ATT_TK, tq)


def _attn_prompt(qt, k_rows, kvt, lp, n_batch):
    t = qt.shape[2]
    tq, tk = _attn_tiles(t)
    nq = t // tq
    n_near = tq // tk + 1
    smem = pl.BlockSpec(memory_space=pltpu.SMEM)
    return pl.pallas_call(
        functools.partial(_attn_p_kernel, tq=tq, tk=tk),
        grid=(n_batch, N_HEADS, nq),
        in_specs=[
            smem, smem,
            pl.BlockSpec((1, D_HEAD, tq), lambda b, h, i: (b, h, i)),
            pl.BlockSpec((1, t, D_HEAD), lambda b, h, i: (h, b, 0)),
            pl.BlockSpec((1, D_HEAD, t), lambda b, h, i: (b, N_HEADS + h, 0)),
            pl.BlockSpec((1, n_near, tk, tq), lambda b, h, i: (h, 0, 0, 0)),
            _const_spec((D_HEAD, 1)),
        ],
        out_specs=pl.BlockSpec((1, D_HEAD, tq), lambda b, h, i: (b, h, i)),
        out_shape=jax.ShapeDtypeStruct((n_batch, W_GROUP, t), BF16),
        scratch_shapes=[
            pltpu.VMEM((V_AUG, t), BF16),
            pltpu.VMEM((2, 1, tq), F32), pltpu.VMEM((2, V_AUG, tq), F32),
            pltpu.VMEM((2, 2, tk, tq), F32),
        ],
        compiler_params=_params("arbitrary", "arbitrary", "arbitrary"),
        name="attn_prompt",
    )(lp["c_far"], lp["lam"], qt, k_rows, kvt, lp["bias_near"], lp["sub_g_col"])


def _outproj_p_kernel(x_ref, y_ref, yct_ref, wab_ref, wc_ref, wd_ref, g_ref, b_ref, o_ref, *, alpha):
    w = W_GROUP
    acc = jnp.dot(y_ref[:, 0:2 * w], wab_ref[...], preferred_element_type=F32)
    acc = acc + jnp.dot(y_ref[:, 2 * w:3 * w], wd_ref[...], preferred_element_type=F32)
    yc = yct_ref[0].astype(F32).T.astype(BF16)
    acc = acc + jnp.dot(yc, wc_ref[...], preferred_element_type=F32)
    o_ref[...] = _layer_norm(alpha * x_ref[...] + acc, g_ref[...], b_ref[...])


def _outproj_prompt(x, y_abd, y_ct, w_out, layer, g, b, alpha):
    r, d = x.shape
    w = W_GROUP
    t = y_ct.shape[2]
    tm = min(ROW_TILE, t)
    tpb = t // tm
    return pl.pallas_call(
        functools.partial(_outproj_p_kernel, alpha=alpha),
        grid=(r // tm,),
        in_specs=[
            pl.BlockSpec((tm, d), lambda i: (i, 0)),
            pl.BlockSpec((tm, 3 * W_GROUP), lambda i: (i, 0)),
            pl.BlockSpec((1, W_GROUP, tm), lambda i: (i // tpb, 0, i % tpb)),
            _layer_spec(w_out.shape, layer, rows=2 * w, row_block=0),
            _layer_spec(w_out.shape, layer, rows=w, row_block=2),
            _layer_spec(w_out.shape, layer, rows=w, row_block=3),
            _const_spec((1, d)), _const_spec((1, d)),
        ],
        out_specs=pl.BlockSpec((tm, d), lambda i: (i, 0)),
        out_shape=jax.ShapeDtypeStruct((r, d), F32),
        compiler_params=_params("arbitrary"),
        name="outproj_prompt",
    )(x, y_abd, y_ct, w_out, w_out, w_out, g, b)


def _outproj_s_kernel(x_ref, ya_ref, ybt_ref, yct_ref, yd_ref, w_ref, g_ref, b_ref, o_ref, *, alpha):
    w = W_GROUP
    parts = (ya_ref[...], ybt_ref[...].T, yct_ref[...].T, yd_ref[...])
    acc = None
    for j, part in enumerate(parts):
        d = jnp.dot(part.astype(BF16), w_ref[j * w:(j + 1) * w, :], preferred_element_type=F32)
        acc = d if acc is None else acc + d
    o_ref[...] = _layer_norm(alpha * x_ref[...] + acc, g_ref[...], b_ref[...])


def _outproj_sample(x, y_a, y_bt, y_ct, y_d, w_out, layer, g, b, alpha):
    n, d = x.shape
    w = W_GROUP
    return pl.pallas_call(
        functools.partial(_outproj_s_kernel, alpha=alpha),
        grid=(1,),
        in_specs=[
            _const_spec((n, d)), _const_spec((n, w)), _const_spec((w, n)), _const_spec((w, n)),
            _const_spec((n, w)), _layer_spec(w_out.shape, layer), _const_spec((1, d)), _const_spec((1, d)),
        ],
        out_specs=_const_spec((n, d)),
        out_shape=jax.ShapeDtypeStruct((n, d), F32),
        compiler_params=_params("arbitrary"),
        name="outproj_sample",
    )(x, y_a, y_bt, y_ct, y_d, w_out, g, b)


def _mix_s_kernel(pa_ref, pd_ref, pool_ref, poolw_ref, pscale_ref, sgg_ref, sgb_ref, w00_ref, b0_ref,
                  ya_ref, yd_ref, newpool_ref, vn_ref, *, cnts):
    a = pa_ref[...]
    n, w = a.shape
    grp = lax.broadcasted_iota(jnp.int32, (n, w), 1) >> 6
    acc = a
    means = []
    for j in range(1, POOL_BUF + 1):
        acc = acc + pool_ref[POOL_BUF - j]
        if j + 1 in POOL_WINDOWS:
            means.append(acc / cnts[POOL_WINDOWS.index(j + 1)])
    dpool = _group_select(grp, means) - a
    ya_ref[...] = jnp.dot(dpool.astype(BF16), poolw_ref[...], preferred_element_type=F32) * pscale_ref[...]
    for j in range(POOL_BUF - 1):
        newpool_ref[j] = pool_ref[j + 1]
    newpool_ref[POOL_BUF - 1] = a

    vn = _layer_norm(pd_ref[:, w:2 * w], sgg_ref[...], sgb_ref[...])
    vn_ref[...] = vn
    yd_ref[...] = pd_ref[:, 0:w] * (w00_ref[...] * vn + b0_ref[...])


def _mix_sample(p, pool, lp, past_len):
    n = p.shape[0]
    w = W_GROUP
    cnts = tuple(float(min(past_len + 1, win)) for win in POOL_WINDOWS)
    row = lambda: _const_spec((1, w))
    return pl.pallas_call(
        functools.partial(_mix_s_kernel, cnts=cnts),
        grid=(1,),
        in_specs=[
            pl.BlockSpec((n, w), lambda i: (0, 0)),
            pl.BlockSpec((n, 2 * w), lambda i: (0, 4)),
            _const_spec(pool.shape), _const_spec((w, w)), row(), row(), row(), row(), row(),
        ],
        out_specs=[_const_spec((n, w)), _const_spec((n, w)), _const_spec(pool.shape), _const_spec((n, w))],
        out_shape=[
            jax.ShapeDtypeStruct((n, w), F32), jax.ShapeDtypeStruct((n, w), F32),
            jax.ShapeDtypeStruct(pool.shape, F32), jax.ShapeDtypeStruct((n, w), F32),
        ],
        compiler_params=_params("arbitrary"),
        name="mix_sample",
    )(p, p, pool, lp["pool_w"], lp["pool_scale"], lp["sg_g"], lp["sg_b"], lp["sg_w00"], lp["sg_b0"])


def _hgrn_s_kernel(q_ref, f_ref, i_ref, g_ref, s_ref, loglb_ref, log1mlb_ref, omlb_ref, hgg_ref,
                   so_ref, yb_ref):
    z = f_ref[...]
    f = jnp.exp(_log_forget(z, loglb_ref[...], log1mlb_ref[...]))
    kin = omlb_ref[...] * jax.nn.sigmoid(-z)
    q = q_ref[...]
    v = i_ref[...]
    o = jnp.zeros_like(v)
    for d in range(D_HEAD):
        sn = f[d:d + 1] * s_ref[0, d] + kin[d:d + 1] * v
        so_ref[0, d] = sn
        o = o + q[d:d + 1] * sn
    ms = jnp.mean(o * o, axis=0, keepdims=True)
    yb_ref[...] = o * lax.rsqrt(ms + EPS) * hgg_ref[...] * _silu(g_ref[...])


def _hgrn_sample(pt, state, lp):
    n = pt.shape[1]
    dh = D_HEAD
    blk = lambda off: pl.BlockSpec((dh, n), lambda h: (off * N_HEADS + h, 0))
    col = lambda: pl.BlockSpec((dh, 1), lambda h: (h, 0))
    return pl.pallas_call(
        _hgrn_s_kernel,
        grid=(N_HEADS,),
        in_specs=[
            blk(1), blk(2), blk(3), blk(4),
            pl.BlockSpec((1, dh, dh, n), lambda h: (h, 0, 0, 0)),
            col(), col(), col(), _const_spec((dh, 1)),
        ],
        out_specs=[
            pl.BlockSpec((1, dh, dh, n), lambda h: (h, 0, 0, 0)),
            pl.BlockSpec((dh, n), lambda h: (h, 0)),
        ],
        out_shape=[
            jax.ShapeDtypeStruct(state.shape, F32),
            jax.ShapeDtypeStruct((W_GROUP, n), F32),
        ],
        compiler_params=_params("arbitrary"),
        name="hgrn_sample",
    )(pt, pt, pt, pt, state, lp["log_lb_col"], lp["log1m_lb_col"], lp["om_lb_col"], lp["hg_g_col"])


def _attn_s_one(n, k_pages, v_pages, lam, qt_ref, knt_ref, vnt_ref, bias_ref, bself_ref, n_pages, q_scale):
    w, ns = qt_ref.shape
    dh = D_HEAD
    sel = lax.broadcasted_iota(jnp.int32, (w, ns), 1) == n

    def column(ref):
        return jnp.sum(jnp.where(sel, ref[...], 0.0), axis=1, keepdims=True)

    qcol = column(qt_ref) * q_scale
    kcol = column(knt_ref)
    vcol = column(vnt_ref)

    n_grp = 2 * N_HEADS
    rows = ([], [])
    self_rows = ([], [])
    for h in range(N_HEADS):
        qh = qcol[h * dh:(h + 1) * dh]
        for j in range(n_pages):
            prod = k_pages[j][h] * qh
            rows[0].append(jnp.sum(prod[0:DA_HALF], axis=0, keepdims=True))
            rows[1].append(jnp.sum(prod[DA_HALF:dh], axis=0, keepdims=True))
        self_prod = qh * kcol[h * dh:(h + 1) * dh]
        for mi in range(2):
            s_self = (jnp.sum(self_prod[mi * DA_HALF:(mi + 1) * DA_HALF], axis=0, keepdims=True)
                      + bself_ref[h])
            self_rows[mi].append(jnp.broadcast_to(s_self, (n_pages, 1)))
    s = jnp.concatenate(rows[0] + rows[1], axis=0) + bias_ref[...]
    s_self = jnp.concatenate(self_rows[0] + self_rows[1], axis=0)

    def per_group(col, reduce):
        parts = [jnp.broadcast_to(reduce(col[g * n_pages:(g + 1) * n_pages], axis=0, keepdims=True),
                                  (n_pages, 1)) for g in range(n_grp)]
        return jnp.concatenate(parts, axis=0)

    m = jnp.maximum(per_group(jnp.max(s, axis=1, keepdims=True), jnp.max), s_self)
    p = jnp.exp(s - m)
    p_self = jnp.exp(s_self - m)
    inv_l = 1.0 / (per_group(jnp.sum(p, axis=1, keepdims=True), jnp.sum) + p_self)
    half = N_HEADS * n_pages
    a = (p * inv_l)[0:half] - lam * (p * inv_l)[half:2 * half]
    a_self = (p_self * inv_l)[0:half] - lam * (p_self * inv_l)[half:2 * half]
    weighted = []
    for h in range(N_HEADS):
        oh = v_pages[0][h] * a[h * n_pages:h * n_pages + 1]
        for j in range(1, n_pages):
            oh = oh + v_pages[j][h] * a[h * n_pages + j:h * n_pages + j + 1]
        weighted.append(oh)
    a_self_col = jnp.concatenate(
        [jnp.broadcast_to(a_self[h * n_pages:h * n_pages + 1], (dh, 1)) for h in range(N_HEADS)], axis=0)
    ocol = jnp.sum(jnp.concatenate(weighted, axis=0), axis=1, keepdims=True) + a_self_col * vcol
    return sel, ocol


def _attn_s_kernel(pt_ref, lam_ref, qt_ref, knt_ref, vnt_ref, bias_ref, bself_ref, subg_ref, *rest,
                   n_pages, sps, q_scale):
    o_ref = rest[2 * sps * n_pages]
    ot_ref = rest[2 * sps * n_pages + 1]
    del pt_ref
    dh = D_HEAD
    step = pl.program_id(0)
    ot = ot_ref[...]
    for u in range(sps):
        sel, ocol = _attn_s_one(step * sps + u, rest[u * n_pages:(u + 1) * n_pages],
                                rest[(sps + u) * n_pages:(sps + u + 1) * n_pages], lam_ref[0],
                                qt_ref, knt_ref, vnt_ref, bias_ref, bself_ref, n_pages, q_scale)
        ot = jnp.where(sel, ocol, ot)
    ot_ref[...] = ot

    @pl.when(step == pl.num_programs(0) - 1)
    def _():
        for h in range(N_HEADS):
            oh = ot_ref[h * dh:(h + 1) * dh, :]
            ms = jnp.mean(oh * oh, axis=0, keepdims=True)
            o_ref[h * dh:(h + 1) * dh, :] = oh * lax.rsqrt(ms + EPS) * subg_ref[...]


def _attn_sample(pt, cache_kt, cache_vt, page_table, layer, lp):
    n = pt.shape[1]
    n_pages = page_table.shape[1]
    w = W_GROUP
    dh = D_HEAD
    blk = lambda off: pl.BlockSpec((w, n), lambda i, tbl: (off, 0))
    sps = DEC_SAMPLES_PER_STEP
    page_specs = [
        pl.BlockSpec((None, None, N_HEADS, dh, PAGE),
                     functools.partial(lambda i, tbl, u, j: (layer, tbl[i * sps + u, j], 0, 0, 0), u=u, j=j))
        for u in range(sps) for j in range(n_pages)
    ]
    grid_spec = pltpu.PrefetchScalarGridSpec(
        num_scalar_prefetch=1,
        grid=(n // sps,),
        in_specs=[
            pl.BlockSpec(memory_space=pltpu.SMEM),
            blk(5), blk(6), blk(7),
            pl.BlockSpec((2 * N_HEADS * n_pages, PAGE), lambda i, tbl: (0, 0)),
            pl.BlockSpec((N_HEADS, 1, 1), lambda i, tbl: (0, 0, 0)),
            pl.BlockSpec((dh, 1), lambda i, tbl: (0, 0)),
        ] + page_specs + page_specs,
        out_specs=pl.BlockSpec((w, n), lambda i, tbl: (0, 0)),
        scratch_shapes=[pltpu.VMEM((w, n), F32)],
    )
    return pl.pallas_call(
        functools.partial(_attn_s_kernel, n_pages=n_pages, sps=sps, q_scale=DA_HALF ** -0.5),
        grid_spec=grid_spec,
        out_shape=jax.ShapeDtypeStruct((w, n), F32),
        compiler_params=_params("arbitrary"),
        name="attn_sample",
    )(page_table, lp["lam"], pt, pt, pt, lp["bias_past"], lp["bias_self"], lp["sub_g_col"],
      *([cache_kt] * (sps * n_pages)), *([cache_vt] * (sps * n_pages)))


def _rel_bucket(dist):
    n = jnp.maximum(dist, 0)
    max_exact = REL_BUCKETS // 2
    large = max_exact + (jnp.log(jnp.maximum(n, 1).astype(F32) / max_exact)
                         / math.log(REL_MAX_DIST / max_exact) * (REL_BUCKETS - max_exact)).astype(jnp.int32)
    large = jnp.minimum(large, REL_BUCKETS - 1)
    return jnp.where(n < max_exact, n, large)


def _layer_params(l, depth, tq, tk, past_len, n_pages, prm, lb_all):
    w = W_GROUP
    lam_init = 0.8 - 0.6 * math.exp(-0.3 * l)
    lam = (jnp.exp(jnp.sum(prm["diff_lam_q1"][l] * prm["diff_lam_k1"][l]))
           - jnp.exp(jnp.sum(prm["diff_lam_q2"][l] * prm["diff_lam_k2"][l])) + lam_init)
    rel_bias = prm["rel_bias"]

    def bias_of(dist):
        onehot = _rel_bucket(dist)[..., None] == jnp.arange(REL_BUCKETS, dtype=jnp.int32)
        table = rel_bias.T.reshape((N_HEADS,) + (1,) * dist.ndim + (REL_BUCKETS,))
        return jnp.sum(jnp.where(onehot[None], table, 0.0), axis=-1)

    span = tq + tk - 1
    jj = jnp.arange(span + 1, dtype=jnp.int32)
    tiles = []
    for r in range(-1, tq // tk):
        dist = jj - (tk - 1) - r * tk
        f = jnp.where((dist >= 0)[None], bias_of(dist), NEG)
        sheared = jnp.tile(f, (1, tk))[:, :tk * span].reshape(N_HEADS, tk, span)
        tiles.append(sheared[:, :, tk - 1:tk - 1 + tq])
    bias_near = jnp.stack(tiles, axis=1) * LOG2E
    kpos = jnp.arange(n_pages * PAGE, dtype=jnp.int32)
    bias_past = jnp.tile(bias_of(past_len - kpos).reshape(N_HEADS * n_pages, PAGE), (2, 1))
    lb = lb_all[l]
    tril = jnp.tril(jnp.ones((SG_CHUNK, SG_CHUNK), F32))
    sub_g = prm["diff_subln_g"][l] * (1.0 - lam_init)
    return {
        "ln_g": prm["ln_g"][l], "ln_b": prm["ln_b"][l],
        "pool_w": jax.scipy.linalg.block_diag(*prm["pool_w"][l]).astype(BF16),
        "pool_scale": prm["pool_scale"][l][None],
        "log_lb": jnp.log(lb)[None], "log1m_lb": jnp.log1p(-lb)[None], "om_lb": (1.0 - lb)[None],
        "log_lb_col": jnp.log(lb)[:, None], "log1m_lb_col": jnp.log1p(-lb)[:, None],
        "om_lb_col": (1.0 - lb)[:, None],
        "hg_g": jnp.tile(prm["hgrn_norm_g"][l], N_HEADS)[None], "hg_g_col": prm["hgrn_norm_g"][l][:, None],
        "sg_g": prm["sgu_ln_g"][l][None], "sg_b": prm["sgu_ln_b"][l][None],
        "sg_w": (prm["sgu_w"][l] * tril).reshape(N_HEADS * SG_CHUNK, SG_CHUNK).astype(BF16),
        "sg_bias": jnp.repeat(prm["sgu_b"][l].T, D_HEAD, axis=1),
        "sg_w00": jnp.repeat(prm["sgu_w"][l][:, 0, 0], D_HEAD)[None],
        "sg_b0": jnp.repeat(prm["sgu_b"][l][:, 0], D_HEAD)[None],
        "eones": jnp.kron(jnp.eye(N_HEADS, dtype=F32), jnp.ones((D_HEAD, D_HEAD), F32)).astype(BF16),
        "lam": lam.reshape(1), "c_far": rel_bias[REL_BUCKETS - 1] * LOG2E,
        "bias_near": bias_near,
        "bias_past": bias_past, "bias_self": rel_bias[0].reshape(N_HEADS, 1, 1),
        "sub_g_col": sub_g[:, None],
    }


def kernel(x_prompt, x_sample, state_pool, state_hgrn, cache_k, cache_v, page_table, rel_bias, ln_g, ln_b,
           ffn1_w_gu, ffn1_w_dn, ffn2_w_gu, ffn2_w_dn, w_in, w_out, pool_w, pool_scale, hgrn_lb,
           hgrn_norm_g, diff_lam_q1, diff_lam_k1, diff_lam_q2, diff_lam_k2, diff_subln_g, sgu_ln_g,
           sgu_ln_b, sgu_w, sgu_b):
    prm = dict(rel_bias=rel_bias, ln_g=ln_g, ln_b=ln_b, ffn1_w_gu=ffn1_w_gu, ffn1_w_dn=ffn1_w_dn,
               ffn2_w_gu=ffn2_w_gu, ffn2_w_dn=ffn2_w_dn, w_in=w_in, w_out=w_out, pool_w=pool_w,
               pool_scale=pool_scale, hgrn_norm_g=hgrn_norm_g, diff_lam_q1=diff_lam_q1,
               diff_lam_k1=diff_lam_k1, diff_lam_q2=diff_lam_q2, diff_lam_k2=diff_lam_k2,
               diff_subln_g=diff_subln_g, sgu_ln_g=sgu_ln_g, sgu_ln_b=sgu_ln_b, sgu_w=sgu_w, sgu_b=sgu_b)
    depth = w_in.shape[0]
    nb, t, d = x_prompt.shape
    ns = x_sample.shape[0]
    n_pages = page_table.shape[1]
    past_len = n_pages * PAGE
    alpha = (2.0 * depth) ** 0.25
    w = W_GROUP
    tq, tk = _attn_tiles(t)
    assert x_sample.shape[1] == 1 and t % SG_CHUNK == 0 and t % tq == 0 and tk >= REL_MAX_DIST

    lb_cum = jnp.cumsum(jax.nn.softmax(hgrn_lb.astype(F32), axis=0), axis=0)
    lb_all = jnp.maximum(lb_cum - lb_cum[:1], 0.0)

    cache_kt = jnp.transpose(cache_k, (0, 1, 3, 4, 2))
    cache_vt = jnp.transpose(cache_v, (0, 1, 3, 4, 2))
    hgrn_t = jnp.transpose(state_hgrn, (0, 2, 3, 4, 1))
    pool_t = jnp.transpose(state_pool, (0, 2, 1, 3))

    wb = {k: prm[k].astype(BF16) for k in ("ffn1_w_gu", "ffn1_w_dn", "ffn2_w_gu", "ffn2_w_dn", "w_in", "w_out")}
    w_int = jnp.swapaxes(prm["w_in"], 1, 2).astype(BF16)
    w_qkvt = w_int[:, 5 * w:8 * w]

    xp = x_prompt.reshape(nb * t, d)
    xs = x_sample.reshape(ns, d)
    outs = {k: [] for k in ("pool_p", "pool_s", "hgrn_p", "hgrn_s", "k_p", "k_s", "v_p", "v_s", "sgv_p", "sgv_s")}
    for l in range(depth):
        lp = _layer_params(l, depth, tq, tk, past_len, n_pages, prm, lb_all)
        g = [lp["ln_g"][i][None] for i in range(3)]
        b = [lp["ln_b"][i][None] for i in range(3)]

        xp = _ffn(xp, wb["ffn1_w_gu"], wb["ffn1_w_dn"], l, g[0], b[0], alpha)
        p, k_rows, q_t, kvt = _inproj_prompt(xp, wb["w_in"], w_qkvt, l, nb)
        y_abd, pool16, st, sgv = _mix_prompt(p, nb, lp)
        y_ct = _attn_prompt(q_t, k_rows, kvt, lp, nb)
        xp = _outproj_prompt(xp, y_abd, y_ct, wb["w_out"], l, g[1], b[1], alpha)
        xp = _ffn(xp, wb["ffn2_w_gu"], wb["ffn2_w_dn"], l, g[2], b[2], alpha)
        kvt5 = kvt.reshape(nb, 2, N_HEADS, D_HEAD, t)
        outs["k_p"].append(jnp.transpose(kvt5[:, 0], (0, 3, 1, 2)))
        outs["v_p"].append(jnp.transpose(kvt5[:, 1], (0, 3, 1, 2)))
        outs["pool_p"].append(pool16[:, 1:])
        outs["hgrn_p"].append(jnp.stack(
            [jnp.swapaxes(st[:, h * D_HEAD:(h + 1) * D_HEAD, h * D_HEAD:(h + 1) * D_HEAD], 1, 2)
             for h in range(N_HEADS)], axis=1))
        outs["sgv_p"].append(sgv)

        xs = _ffn(xs, wb["ffn1_w_gu"], wb["ffn1_w_dn"], l, g[0], b[0], alpha)
        ps, pst = _inproj_sample(xs, wb["w_in"], w_int, l)
        y_a, y_d, new_pool, vn = _mix_sample(ps, pool_t[l], lp, past_len)
        new_state, y_bt = _hgrn_sample(pst, hgrn_t[l], lp)
        y_ct = _attn_sample(pst, cache_kt, cache_vt, page_table, l, lp)
        xs = _outproj_sample(xs, y_a, y_bt, y_ct, y_d, wb["w_out"], l, g[1], b[1], alpha)
        xs = _ffn(xs, wb["ffn2_w_gu"], wb["ffn2_w_dn"], l, g[2], b[2], alpha)
        outs["k_s"].append(jnp.transpose(pst[6 * w:7 * w].reshape(N_HEADS, D_HEAD, ns), (2, 0, 1))[:, None])
        outs["v_s"].append(jnp.transpose(pst[7 * w:8 * w].reshape(N_HEADS, D_HEAD, ns), (2, 0, 1))[:, None])
        outs["pool_s"].append(jnp.transpose(new_pool, (1, 0, 2)))
        outs["hgrn_s"].append(jnp.transpose(new_state, (3, 0, 1, 2)))
        outs["sgv_s"].append(vn[:, None])

    st = {k: jnp.stack(v, axis=0) for k, v in outs.items()}
    return (xp.reshape(nb, t, d), xs.reshape(ns, 1, d), st["pool_p"], st["pool_s"], st["hgrn_p"], st["hgrn_s"],
            st["k_p"], st["k_s"], st["v_p"], st["v_s"], st["sgv_p"], st["sgv_s"])
```

```python
import functools
import math

import jax
import jax.numpy as jnp
import jax.scipy.linalg
from jax import lax
from jax.experimental import pallas as pl
from jax.experimental.pallas import tpu as pltpu

F32 = jnp.float32
BF16 = jnp.bfloat16

N_MIX = 4
W_GROUP = 256
N_HEADS = 4
D_HEAD = 64
DA_HALF = 32
POOL_WINDOWS = (2, 4, 8, 16)
POOL_BUF = 15
PAGE = 128
SG_CHUNK = 128
HG_SUB = 16
V_AUG = 80
REL_BUCKETS = 32
REL_MAX_DIST = 128
EPS = 1e-5
NEG = -1e30
LOG2E = 1.0 / math.log(2.0)

VMEM_LIMIT = 56 * 1024 * 1024
MXU_DIM = 256
ROW_TILE = 512
ATT_TQ = 1024
ATT_TK = 512
DEC_SAMPLES_PER_STEP = 2

NT_DIMS = (((1,), (1,)), ((), ()))


def _params(*sem):
    return pltpu.CompilerParams(dimension_semantics=sem, vmem_limit_bytes=VMEM_LIMIT)


def _const_spec(shape, single=False):
    nd = len(shape)
    kw = {"pipeline_mode": pl.Buffered(1)} if single else {}
    return pl.BlockSpec(shape, lambda *_: (0,) * nd, **kw)


def _layer_spec(shape, layer, single=False, rows=None, row_block=0):
    kw = {"pipeline_mode": pl.Buffered(1)} if single else {}
    block = (None, shape[1] if rows is None else rows) + tuple(shape[2:])
    return pl.BlockSpec(block, lambda *_: (layer, row_block) + (0,) * (len(shape) - 2), **kw)


def _layer_norm(y, g, b):
    mu = jnp.mean(y, axis=-1, keepdims=True)
    yc = y - mu
    var = jnp.mean(yc * yc, axis=-1, keepdims=True)
    return yc * lax.rsqrt(var + EPS) * g + b


def _silu(x):
    return x * jax.nn.sigmoid(x)


def _log_forget(z, log_lb, log1m_lb):
    log_sig = -(jnp.maximum(-z, 0.0) + jnp.log1p(jnp.exp(-jnp.abs(z))))
    b = log1m_lb + log_sig
    return jnp.maximum(log_lb, b) + jnp.log1p(jnp.exp(-jnp.abs(log_lb - b)))


def _swiglu_ln(x, wgu_ref, wdn_ref, g, b, d_ff, bounds, alpha):
    xb = x.astype(BF16)
    acc = None
    for lo, hi in zip(bounds[:-1], bounds[1:]):
        gate = jnp.dot(xb, wgu_ref[:, lo:hi], preferred_element_type=F32)
        up = jnp.dot(xb, wgu_ref[:, d_ff + lo:d_ff + hi], preferred_element_type=F32)
        h = (_silu(gate) * up).astype(BF16)
        part = jnp.dot(h, wdn_ref[lo:hi, :], preferred_element_type=F32)
        acc = part if acc is None else acc + part
    return _layer_norm(alpha * x + 0.5 * acc, g, b)


def _ffn_kernel(x_ref, wgu_ref, wdn_ref, g_ref, b_ref, o_ref, *, d_ff, bounds, alpha):
    o_ref[...] = _swiglu_ln(x_ref[...], wgu_ref, wdn_ref, g_ref[...], b_ref[...], d_ff, bounds, alpha)


def _ffn_bounds(d_ff):
    n_tiles = d_ff // MXU_DIM
    return (0, (n_tiles + 1) // 2 * MXU_DIM, d_ff)


def _ffn(x, w_gu, w_dn, layer, g, b, alpha):
    r, d = x.shape
    d_ff = w_dn.shape[1]
    tm = min(ROW_TILE, r)
    bounds = _ffn_bounds(d_ff)
    return pl.pallas_call(
        functools.partial(_ffn_kernel, d_ff=d_ff, bounds=bounds, alpha=alpha),
        grid=(r // tm,),
        in_specs=[
            pl.BlockSpec((tm, d), lambda i: (i, 0)),
            _layer_spec(w_gu.shape, layer, single=True),
            _layer_spec(w_dn.shape, layer, single=True),
            _const_spec((1, d)),
            _const_spec((1, d)),
        ],
        out_specs=pl.BlockSpec((tm, d), lambda i: (i, 0)),
        out_shape=jax.ShapeDtypeStruct((r, d), F32),
        compiler_params=_params("arbitrary"),
        name="ffn",
    )(x, w_gu, w_dn, g, b)


def _inproj_p_kernel(x_ref, w_ref, wqkvt_ref, pa_ref, pd_ref, k_ref, qt_ref, kt_ref, vt_ref, *, q_scale):
    w = W_GROUP
    xb = x_ref[...].astype(BF16)
    pa_ref[...] = jnp.dot(xb, w_ref[:, 0:5 * w], preferred_element_type=F32)
    pd_ref[...] = jnp.dot(xb, w_ref[:, 8 * w:10 * w], preferred_element_type=F32)
    k = jnp.dot(xb, w_ref[:, 6 * w:7 * w], preferred_element_type=F32)
    for h in range(N_HEADS):
        k_ref[h] = k[:, h * D_HEAD:(h + 1) * D_HEAD].astype(BF16)
    qkvt = lax.dot_general(wqkvt_ref[...], xb, NT_DIMS, preferred_element_type=F32)
    qt_ref[0] = (qkvt[0:w] * q_scale).astype(BF16)
    kt_ref[0] = qkvt[w:2 * w]
    vt_ref[0] = qkvt[2 * w:3 * w]


def _inproj_prompt(x, w_in, w_qkvt, layer, n_batch):
    r, d = x.shape
    t = r // n_batch
    w = W_GROUP
    tm = min(ROW_TILE, t)
    tpb = t // tm
    chan_major = lambda: pl.BlockSpec((1, w, tm), lambda i: (i // tpb, 0, i % tpb))
    return pl.pallas_call(
        functools.partial(_inproj_p_kernel, q_scale=DA_HALF ** -0.5 * LOG2E),
        grid=(r // tm,),
        in_specs=[
            pl.BlockSpec((tm, d), lambda i: (i, 0)),
            _layer_spec(w_in.shape, layer),
            _layer_spec(w_qkvt.shape, layer),
        ],
        out_specs=[
            pl.BlockSpec((tm, 5 * w), lambda i: (i, 0)),
            pl.BlockSpec((tm, 2 * w), lambda i: (i, 0)),
            pl.BlockSpec((N_HEADS, tm, D_HEAD), lambda i: (0, i, 0)),
            chan_major(), chan_major(), chan_major(),
        ],
        out_shape=[
            jax.ShapeDtypeStruct((r, 5 * w), F32),
            jax.ShapeDtypeStruct((r, 2 * w), F32),
            jax.ShapeDtypeStruct((N_HEADS, r, D_HEAD), BF16),
            jax.ShapeDtypeStruct((n_batch, w, t), BF16),
            jax.ShapeDtypeStruct((n_batch, w, t), F32),
            jax.ShapeDtypeStruct((n_batch, w, t), F32),
        ],
        compiler_params=_params("arbitrary"),
        name="inproj_prompt",
    )(x, w_in, w_qkvt)


def _inproj_s_kernel(x_ref, w_ref, wt_ref, p_ref, pt_ref):
    xb = x_ref[...].astype(BF16)
    p_ref[...] = jnp.dot(xb, w_ref[...], preferred_element_type=F32)
    pt_ref[...] = lax.dot_general(wt_ref[...], xb, NT_DIMS, preferred_element_type=F32)


def _inproj_sample(x, w_in, w_int, layer):
    n, d = x.shape
    d_in = w_in.shape[2]
    return pl.pallas_call(
        _inproj_s_kernel,
        grid=(1,),
        in_specs=[_const_spec((n, d)), _layer_spec(w_in.shape, layer), _layer_spec(w_int.shape, layer)],
        out_specs=[_const_spec((n, d_in)), _const_spec((d_in, n))],
        out_shape=[jax.ShapeDtypeStruct((n, d_in), F32), jax.ShapeDtypeStruct((d_in, n), F32)],
        compiler_params=_params("arbitrary"),
        name="inproj_sample",
    )(x, w_in, w_int)


def _group_select(grp, parts):
    out = parts[N_HEADS - 1]
    for g in range(N_HEADS - 2, -1, -1):
        out = jnp.where(grp == g, parts[g], out)
    return out


def _mix_p_kernel(pa_ref, pd_ref, poolw_ref, pscale_ref, loglb_ref, log1mlb_ref, omlb_ref, hgg_ref,
                  sgg_ref, sgb_ref, sgw_ref, sgbias_ref, eones_ref,
                  y_ref, pool_ref, st_out_ref, sgv_ref,
                  prev_ref, st_ref):
    c = SG_CHUNK
    w = W_GROUP
    t = pl.program_id(1)

    @pl.when(t == 0)
    def _():
        prev_ref[...] = jnp.zeros_like(prev_ref)
        st_ref[...] = jnp.zeros_like(st_ref)

    a = pa_ref[:, 0:w]
    hq = pa_ref[:, w:2 * w]
    hf = pa_ref[:, 2 * w:3 * w]
    hi = pa_ref[:, 3 * w:4 * w]
    hg = pa_ref[:, 4 * w:5 * w]
    su = pd_ref[:, 0:w]
    sv = pd_ref[:, w:2 * w]
    lane = lax.broadcasted_iota(jnp.int32, (c, w), 1)
    row = lax.broadcasted_iota(jnp.int32, (c, w), 0)
    grp = lane >> 6

    e = jnp.concatenate([prev_ref[...], a], axis=0)
    s2 = e + pltpu.roll(e, 1, 0)
    s4 = s2 + pltpu.roll(s2, 2, 0)
    s8 = s4 + pltpu.roll(s4, 4, 0)
    s16 = s8 + pltpu.roll(s8, 8, 0)
    wsum = _group_select(grp, [s2[16:], s4[16:], s8[16:], s16[16:]])
    win = _group_select(grp, [jnp.full((c, w), v, jnp.int32) for v in POOL_WINDOWS])
    cnt = jnp.minimum(t * c + row + 1, win).astype(F32)
    dpool = wsum / cnt - a
    ya = jnp.dot(dpool.astype(BF16), poolw_ref[...], preferred_element_type=F32) * pscale_ref[...]
    prev_ref[...] = a[c - 16:]
    pool_ref[0] = a[c - 16:]

    vn = _layer_norm(sv, sgg_ref[...], sgb_ref[...])
    sg = jnp.dot(sgw_ref[...], vn.astype(BF16), preferred_element_type=F32)
    s_gate = _group_select(grp, [sg[g * c:(g + 1) * c] for g in range(N_HEADS)]) + sgbias_ref[...]
    yd = su * s_gate
    sgv_ref[0] = vn

    n_sub = c // HG_SUB
    logf = _log_forget(hf, loglb_ref[...], log1mlb_ref[...])
    kin = omlb_ref[...] * jax.nn.sigmoid(-hf)
    r16 = row & (HG_SUB - 1)
    bl = logf
    rv = logf
    for sh in (1, 2, 4, 8):
        bl = bl + jnp.where(r16 >= sh, pltpu.roll(bl, sh, 0), 0.0)
        rv = rv + jnp.where(r16 + sh < HG_SUB, pltpu.roll(rv, c - sh, 0), 0.0)
    sub_row = row >> 4
    qtb = (hq * jnp.exp(bl)).astype(BF16)
    kt = kin * jnp.exp(rv - logf)
    dec = jnp.exp(rv)
    vtb = hi.T.astype(BF16)
    bi0 = lax.broadcasted_iota(jnp.int32, (w, w), 0) >> 6
    bi1 = lax.broadcasted_iota(jnp.int32, (w, w), 1) >> 6
    blockmask = bi0 == bi1
    st = st_ref[...]
    seen = []
    for i in range(n_sub):
        seen.append(st.astype(BF16))
        km = jnp.where(sub_row == i, kt, 0.0).astype(BF16)
        u = jnp.dot(vtb, km, preferred_element_type=F32)
        st = dec[i * HG_SUB:i * HG_SUB + 1] * st + jnp.where(blockmask, u, 0.0)
    st_ref[...] = st
    st_out_ref[0] = st
    qx = jnp.concatenate([jnp.where(sub_row == i, qtb, jnp.zeros_like(qtb)) for i in range(n_sub)], axis=1)
    o = lax.dot_general(qx, jnp.concatenate(seen, axis=1), NT_DIMS,
                        preferred_element_type=F32)

    rt = {lo: lax.broadcasted_iota(jnp.int32, (HG_SUB - lo, w), 0) + lo for lo in (0, 8)}
    xs = []
    for i in range(n_sub):
        rows = slice(i * HG_SUB, (i + 1) * HG_SUB)
        bli, qi, ki = bl[rows], hq[rows], kin[rows]
        for s in range(HG_SUB):
            lo = (s // 8) * 8
            ratio = jnp.exp(jnp.minimum(bli[lo:] - bli[s:s + 1], 0.0))
            x_s = jnp.where(rt[lo] >= s, ratio * qi[lo:] * ki[s:s + 1], 0.0)
            if lo:
                x_s = jnp.concatenate([jnp.zeros((lo, w), F32), x_s], axis=0)
            xs.append(x_s.astype(BF16))
    x = jnp.concatenate(xs, axis=0)
    r = jnp.dot(x, eones_ref[...], preferred_element_type=F32)
    o_intra = []
    for i in range(n_sub):
        vi = hi[i * HG_SUB:(i + 1) * HG_SUB]
        base = i * HG_SUB * HG_SUB
        oi = r[base:base + HG_SUB] * vi[0:1]
        for s in range(1, HG_SUB):
            oi = oi + r[base + s * HG_SUB:base + (s + 1) * HG_SUB] * vi[s:s + 1]
        o_intra.append(oi)
    o = o + jnp.concatenate(o_intra, axis=0)
    ms = [jnp.mean(jnp.square(o[:, g * D_HEAD:(g + 1) * D_HEAD]), axis=-1, keepdims=True)
          for g in range(N_HEADS)]
    yb = o * lax.rsqrt(_group_select(grp, ms) + EPS) * hgg_ref[...] * _silu(hg)

    y_ref[:, 0:w] = ya.astype(BF16)
    y_ref[:, w:2 * w] = yb.astype(BF16)
    y_ref[:, 2 * w:3 * w] = yd.astype(BF16)


def _mix_prompt(p_a, p_d, n_batch, lp):
    r = p_a.shape[0]
    t = r // n_batch
    c = SG_CHUNK
    nt = t // c
    w = W_GROUP
    row = lambda: _const_spec((1, w))
    return pl.pallas_call(
        _mix_p_kernel,
        grid=(n_batch, nt),
        in_specs=[
            pl.BlockSpec((c, 5 * w), lambda b, i: (b * nt + i, 0)),
            pl.BlockSpec((c, 2 * w), lambda b, i: (b * nt + i, 0)),
            _const_spec((w, w)), row(), row(), row(), row(), row(), row(), row(),
            _const_spec((N_HEADS * c, c)), _const_spec((c, w)), _const_spec((w, w)),
        ],
        out_specs=[
            pl.BlockSpec((c, 3 * w), lambda b, i: (b * nt + i, 0)),
            pl.BlockSpec((1, 16, w), lambda b, i: (b, 0, 0)),
            pl.BlockSpec((1, w, w), lambda b, i: (b, 0, 0)),
            pl.BlockSpec((1, c, w), lambda b, i: (b, 0, 0)),
        ],
        out_shape=[
            jax.ShapeDtypeStruct((r, 3 * w), BF16),
            jax.ShapeDtypeStruct((n_batch, 16, w), F32),
            jax.ShapeDtypeStruct((n_batch, w, w), F32),
            jax.ShapeDtypeStruct((n_batch, c, w), F32),
        ],
        scratch_shapes=[pltpu.VMEM((16, w), F32), pltpu.VMEM((w, w), F32)],
        compiler_params=_params("arbitrary", "arbitrary"),
        name="mix_prompt",
    )(p_a, p_d, lp["pool_w"], lp["pool_scale"], lp["log_lb"], lp["log1m_lb"], lp["om_lb"], lp["hg_g"],
      lp["sg_g"], lp["sg_b"], lp["sg_w"], lp["sg_bias"], lp["eones"])


def _attn_p_kernel(cfar_ref, lam_ref, qt_ref, k_ref, vt_ref, near_ref, subg_ref, o_ref,
                   va_ref, m_ref, acc_ref, s_ref, *, tq, tk):
    h = pl.program_id(1)
    qi = pl.program_id(2)
    dh = D_HEAD

    @pl.when(qi == 0)
    def _():
        va_ref[0:dh, :] = vt_ref[0].astype(BF16)
        rows = lax.broadcasted_iota(jnp.int32, (V_AUG - dh, va_ref.shape[1]), 0)
        va_ref[dh:V_AUG, :] = jnp.where(rows == 0, 1.0, 0.0).astype(BF16)

    qt = qt_ref[0]
    sub = lax.broadcasted_iota(jnp.int32, qt.shape, 0)
    zero = jnp.zeros_like(qt)
    qm = (jnp.where(sub < DA_HALF, qt, zero), jnp.where(sub >= DA_HALF, qt, zero))
    m_ref[...] = jnp.full_like(m_ref, NEG)
    acc_ref[...] = jnp.zeros_like(acc_ref)

    def qk(ki, slot, q0=0):
        k0 = pl.multiple_of(ki * tk, tk)
        kblk = k_ref[0, pl.ds(k0, tk), :]
        for mi in range(2):
            s_ref[slot, mi, :, q0:] = jnp.dot(kblk, qm[mi][:, q0:], preferred_element_type=F32)

    def softmax_pv(ki, slot, near, shift, q0=0):
        k0 = pl.multiple_of(ki * tk, tk)
        vblk = va_ref[:, pl.ds(k0, tk)]
        for mi in range(2):
            s = s_ref[slot, mi, :, q0:]
            if near is not None:
                s = s + near_ref[0, near, :, q0:]
            m_blk = jnp.max(s, axis=0, keepdims=True)
            if shift is not None:
                m_blk = m_blk + shift
            m_old = m_ref[mi, :, q0:]
            m_new = jnp.maximum(m_old, m_blk)
            alpha = jnp.exp2(m_old - m_new)
            p = jnp.exp2(s - (m_new if shift is None else m_new - shift)).astype(BF16)
            acc_ref[mi, :, q0:] = alpha * acc_ref[mi, :, q0:] + jnp.dot(vblk, p, preferred_element_type=F32)
            m_ref[mi, :, q0:] = m_new

    cfar = cfar_ref[h]
    n_diag = tq // tk
    kb0 = qi * n_diag

    def near_tiles(first_tile, first_near, n, slot0):
        q0s = [max(first_near + j - 1, 0) * tk for j in range(n)]
        for j in range(n):
            if j + 1 < n:
                qk(first_tile + j + 1, (slot0 + j + 1) % 2, q0s[j + 1])
            softmax_pv(first_tile + j, (slot0 + j) % 2, first_near + j, None, q0s[j])

    def far_pair(pi, carry):
        b = 2 * pi
        qk(b + 1, 1)
        softmax_pv(b, 0, None, cfar)
        qk(b + 2, 0)
        softmax_pv(b + 1, 1, None, cfar)
        return carry

    @pl.when(qi == 0)
    def _():
        qk(0, 0)
        near_tiles(0, 1, n_diag, 0)

    @pl.when(qi >= 1)
    def _():
        n_far = kb0 - 1
        qk(0, 0)
        lax.fori_loop(0, n_far // 2, far_pair, 0)

        def tail(odd):
            if odd:
                qk(n_far, 1)
                softmax_pv(n_far - 1, 0, None, cfar)
            near_tiles(n_far, 0, n_diag + 1, 1 if odd else 0)

        if n_diag % 2 == 0:
            tail(True)
        else:
            pl.when(n_far % 2 == 1)(lambda: tail(True))
            pl.when(n_far % 2 == 0)(lambda: tail(False))

    a0 = acc_ref[0]
    a1 = acc_ref[1]
    o = a0[0:dh] / a0[dh:dh + 1] - lam_ref[0] * (a1[0:dh] / a1[dh:dh + 1])
    ms = jnp.mean(o * o, axis=0, keepdims=True)
    o_ref[0] = (o * lax.rsqrt(ms + EPS) * subg_ref[...]).astype(BF16)


def _attn_tiles(t):
    tq = min(ATT_TQ, t)
    return tq, min(ATT_TK, tq)


def _attn_prompt(qt, k_rows, vt, lp, n_batch):
    t = qt.shape[2]
    tq, tk = _attn_tiles(t)
    nq = t // tq
    n_near = tq // tk + 1
    smem = pl.BlockSpec(memory_space=pltpu.SMEM)
    return pl.pallas_call(
        functools.partial(_attn_p_kernel, tq=tq, tk=tk),
        grid=(n_batch, N_HEADS, nq),
        in_specs=[
            smem, smem,
            pl.BlockSpec((1, D_HEAD, tq), lambda b, h, i: (b, h, i)),
            pl.BlockSpec((1, t, D_HEAD), lambda b, h, i: (h, b, 0)),
            pl.BlockSpec((1, D_HEAD, t), lambda b, h, i: (b, h, 0)),
            pl.BlockSpec((1, n_near, tk, tq), lambda b, h, i: (h, 0, 0, 0)),
            _const_spec((D_HEAD, 1)),
        ],
        out_specs=pl.BlockSpec((1, D_HEAD, tq), lambda b, h, i: (b, h, i)),
        out_shape=jax.ShapeDtypeStruct((n_batch, W_GROUP, t), BF16),
        scratch_shapes=[
            pltpu.VMEM((V_AUG, t), BF16),
            pltpu.VMEM((2, 1, tq), F32), pltpu.VMEM((2, V_AUG, tq), F32),
            pltpu.VMEM((2, 2, tk, tq), F32),
        ],
        compiler_params=_params("arbitrary", "arbitrary", "arbitrary"),
        name="attn_prompt",
    )(lp["c_far"], lp["lam"], qt, k_rows, vt, lp["bias_near"], lp["sub_g_col"])


def _outproj_ffn_p_kernel(x_ref, y_ref, yct_ref, wab_ref, wc_ref, wd_ref, g1_ref, b1_ref,
                          wgu_ref, wdn_ref, g2_ref, b2_ref, o_ref, *, d_ff, bounds, alpha):
    w = W_GROUP
    acc = jnp.dot(y_ref[:, 0:2 * w], wab_ref[...], preferred_element_type=F32)
    acc = acc + jnp.dot(y_ref[:, 2 * w:3 * w], wd_ref[...], preferred_element_type=F32)
    yc = yct_ref[0].astype(F32).T.astype(BF16)
    acc = acc + jnp.dot(yc, wc_ref[...], preferred_element_type=F32)
    x2 = _layer_norm(alpha * x_ref[...] + acc, g1_ref[...], b1_ref[...])
    o_ref[...] = _swiglu_ln(x2, wgu_ref, wdn_ref, g2_ref[...], b2_ref[...], d_ff, bounds, alpha)


def _outproj_ffn_prompt(x, y_abd, y_ct, w_out, w_gu, w_dn, layer, g1, b1, g2, b2, alpha):
    r, d = x.shape
    w = W_GROUP
    t = y_ct.shape[2]
    tm = min(ROW_TILE, t)
    tpb = t // tm
    d_ff = w_dn.shape[1]
    return pl.pallas_call(
        functools.partial(_outproj_ffn_p_kernel, d_ff=d_ff, bounds=_ffn_bounds(d_ff), alpha=alpha),
        grid=(r // tm,),
        in_specs=[
            pl.BlockSpec((tm, d), lambda i: (i, 0)),
            pl.BlockSpec((tm, 3 * W_GROUP), lambda i: (i, 0)),
            pl.BlockSpec((1, W_GROUP, tm), lambda i: (i // tpb, 0, i % tpb)),
            _layer_spec(w_out.shape, layer, rows=2 * w, row_block=0),
            _layer_spec(w_out.shape, layer, rows=w, row_block=2),
            _layer_spec(w_out.shape, layer, rows=w, row_block=3),
            _const_spec((1, d)), _const_spec((1, d)),
            _layer_spec(w_gu.shape, layer, single=True),
            _layer_spec(w_dn.shape, layer, single=True),
            _const_spec((1, d)), _const_spec((1, d)),
        ],
        out_specs=pl.BlockSpec((tm, d), lambda i: (i, 0)),
        out_shape=jax.ShapeDtypeStruct((r, d), F32),
        compiler_params=_params("arbitrary"),
        name="outproj_ffn_prompt",
    )(x, y_abd, y_ct, w_out, w_out, w_out, g1, b1, w_gu, w_dn, g2, b2)


def _outproj_s_kernel(x_ref, ya_ref, ybt_ref, yct_ref, yd_ref, w_ref, g_ref, b_ref, o_ref, *, alpha):
    w = W_GROUP
    parts = (ya_ref[...], ybt_ref[...].T, yct_ref[...].T, yd_ref[...])
    acc = None
    for j, part in enumerate(parts):
        d = jnp.dot(part.astype(BF16), w_ref[j * w:(j + 1) * w, :], preferred_element_type=F32)
        acc = d if acc is None else acc + d
    o_ref[...] = _layer_norm(alpha * x_ref[...] + acc, g_ref[...], b_ref[...])


def _outproj_sample(x, y_a, y_bt, y_ct, y_d, w_out, layer, g, b, alpha):
    n, d = x.shape
    w = W_GROUP
    return pl.pallas_call(
        functools.partial(_outproj_s_kernel, alpha=alpha),
        grid=(1,),
        in_specs=[
            _const_spec((n, d)), _const_spec((n, w)), _const_spec((w, n)), _const_spec((w, n)),
            _const_spec((n, w)), _layer_spec(w_out.shape, layer), _const_spec((1, d)), _const_spec((1, d)),
        ],
        out_specs=_const_spec((n, d)),
        out_shape=jax.ShapeDtypeStruct((n, d), F32),
        compiler_params=_params("arbitrary"),
        name="outproj_sample",
    )(x, y_a, y_bt, y_ct, y_d, w_out, g, b)


def _mix_s_kernel(pa_ref, pd_ref, pool_ref, poolw_ref, pscale_ref, sgg_ref, sgb_ref, w00_ref, b0_ref,
                  ya_ref, yd_ref, newpool_ref, vn_ref, *, cnts):
    a = pa_ref[...]
    n, w = a.shape
    grp = lax.broadcasted_iota(jnp.int32, (n, w), 1) >> 6
    acc = a
    means = []
    for j in range(1, POOL_BUF + 1):
        acc = acc + pool_ref[POOL_BUF - j]
        if j + 1 in POOL_WINDOWS:
            means.append(acc / cnts[POOL_WINDOWS.index(j + 1)])
    dpool = _group_select(grp, means) - a
    ya_ref[...] = jnp.dot(dpool.astype(BF16), poolw_ref[...], preferred_element_type=F32) * pscale_ref[...]
    for j in range(POOL_BUF - 1):
        newpool_ref[j] = pool_ref[j + 1]
    newpool_ref[POOL_BUF - 1] = a

    vn = _layer_norm(pd_ref[:, w:2 * w], sgg_ref[...], sgb_ref[...])
    vn_ref[...] = vn
    yd_ref[...] = pd_ref[:, 0:w] * (w00_ref[...] * vn + b0_ref[...])


def _mix_sample(p, pool, lp, past_len):
    n = p.shape[0]
    w = W_GROUP
    cnts = tuple(float(min(past_len + 1, win)) for win in POOL_WINDOWS)
    row = lambda: _const_spec((1, w))
    return pl.pallas_call(
        functools.partial(_mix_s_kernel, cnts=cnts),
        grid=(1,),
        in_specs=[
            pl.BlockSpec((n, w), lambda i: (0, 0)),
            pl.BlockSpec((n, 2 * w), lambda i: (0, 4)),
            _const_spec(pool.shape), _const_spec((w, w)), row(), row(), row(), row(), row(),
        ],
        out_specs=[_const_spec((n, w)), _const_spec((n, w)), _const_spec(pool.shape), _const_spec((n, w))],
        out_shape=[
            jax.ShapeDtypeStruct((n, w), F32), jax.ShapeDtypeStruct((n, w), F32),
            jax.ShapeDtypeStruct(pool.shape, F32), jax.ShapeDtypeStruct((n, w), F32),
        ],
        compiler_params=_params("arbitrary"),
        name="mix_sample",
    )(p, p, pool, lp["pool_w"], lp["pool_scale"], lp["sg_g"], lp["sg_b"], lp["sg_w00"], lp["sg_b0"])


def _hgrn_s_kernel(q_ref, f_ref, i_ref, g_ref, s_ref, loglb_ref, log1mlb_ref, omlb_ref, hgg_ref,
                   so_ref, yb_ref):
    z = f_ref[...]
    f = jnp.exp(_log_forget(z, loglb_ref[...], log1mlb_ref[...]))
    kin = omlb_ref[...] * jax.nn.sigmoid(-z)
    q = q_ref[...]
    v = i_ref[...]
    o = jnp.zeros_like(v)
    for d in range(D_HEAD):
        sn = f[d:d + 1] * s_ref[0, d] + kin[d:d + 1] * v
        so_ref[0, d] = sn
        o = o + q[d:d + 1] * sn
    ms = jnp.mean(o * o, axis=0, keepdims=True)
    yb_ref[...] = o * lax.rsqrt(ms + EPS) * hgg_ref[...] * _silu(g_ref[...])


def _hgrn_sample(pt, state, lp):
    n = pt.shape[1]
    dh = D_HEAD
    blk = lambda off: pl.BlockSpec((dh, n), lambda h: (off * N_HEADS + h, 0))
    col = lambda: pl.BlockSpec((dh, 1), lambda h: (h, 0))
    return pl.pallas_call(
        _hgrn_s_kernel,
        grid=(N_HEADS,),
        in_specs=[
            blk(1), blk(2), blk(3), blk(4),
            pl.BlockSpec((1, dh, dh, n), lambda h: (h, 0, 0, 0)),
            col(), col(), col(), _const_spec((dh, 1)),
        ],
        out_specs=[
            pl.BlockSpec((1, dh, dh, n), lambda h: (h, 0, 0, 0)),
            pl.BlockSpec((dh, n), lambda h: (h, 0)),
        ],
        out_shape=[
            jax.ShapeDtypeStruct(state.shape, F32),
            jax.ShapeDtypeStruct((W_GROUP, n), F32),
        ],
        compiler_params=_params("arbitrary"),
        name="hgrn_sample",
    )(pt, pt, pt, pt, state, lp["log_lb_col"], lp["log1m_lb_col"], lp["om_lb_col"], lp["hg_g_col"])


def _attn_s_one(n, k_pages, v_pages, lam, qt_ref, knt_ref, vnt_ref, bias_ref, bself_ref, n_pages, q_scale):
    w, ns = qt_ref.shape
    dh = D_HEAD
    sel = lax.broadcasted_iota(jnp.int32, (w, ns), 1) == n

    def column(ref):
        return jnp.sum(jnp.where(sel, ref[...], 0.0), axis=1, keepdims=True)

    qcol = column(qt_ref) * q_scale
    kcol = column(knt_ref)
    vcol = column(vnt_ref)

    n_grp = 2 * N_HEADS
    rows = ([], [])
    self_rows = ([], [])
    for h in range(N_HEADS):
        qh = qcol[h * dh:(h + 1) * dh]
        for j in range(n_pages):
            prod = k_pages[j][h] * qh
            rows[0].append(jnp.sum(prod[0:DA_HALF], axis=0, keepdims=True))
            rows[1].append(jnp.sum(prod[DA_HALF:dh], axis=0, keepdims=True))
        self_prod = qh * kcol[h * dh:(h + 1) * dh]
        for mi in range(2):
            s_self = (jnp.sum(self_prod[mi * DA_HALF:(mi + 1) * DA_HALF], axis=0, keepdims=True)
                      + bself_ref[h])
            self_rows[mi].append(jnp.broadcast_to(s_self, (n_pages, 1)))
    s = jnp.concatenate(rows[0] + rows[1], axis=0) + bias_ref[...]
    s_self = jnp.concatenate(self_rows[0] + self_rows[1], axis=0)

    def per_group(col, reduce):
        parts = [jnp.broadcast_to(reduce(col[g * n_pages:(g + 1) * n_pages], axis=0, keepdims=True),
                                  (n_pages, 1)) for g in range(n_grp)]
        return jnp.concatenate(parts, axis=0)

    m = jnp.maximum(per_group(jnp.max(s, axis=1, keepdims=True), jnp.max), s_self)
    p = jnp.exp(s - m)
    p_self = jnp.exp(s_self - m)
    inv_l = 1.0 / (per_group(jnp.sum(p, axis=1, keepdims=True), jnp.sum) + p_self)
    half = N_HEADS * n_pages
    a = (p * inv_l)[0:half] - lam * (p * inv_l)[half:2 * half]
    a_self = (p_self * inv_l)[0:half] - lam * (p_self * inv_l)[half:2 * half]
    weighted = []
    for h in range(N_HEADS):
        oh = v_pages[0][h] * a[h * n_pages:h * n_pages + 1]
        for j in range(1, n_pages):
            oh = oh + v_pages[j][h] * a[h * n_pages + j:h * n_pages + j + 1]
        weighted.append(oh)
    a_self_col = jnp.concatenate(
        [jnp.broadcast_to(a_self[h * n_pages:h * n_pages + 1], (dh, 1)) for h in range(N_HEADS)], axis=0)
    ocol = jnp.sum(jnp.concatenate(weighted, axis=0), axis=1, keepdims=True) + a_self_col * vcol
    return sel, ocol


def _attn_s_kernel(pt_ref, lam_ref, qt_ref, knt_ref, vnt_ref, bias_ref, bself_ref, subg_ref, *rest,
                   n_pages, sps, q_scale):
    o_ref = rest[2 * sps * n_pages]
    ot_ref = rest[2 * sps * n_pages + 1]
    del pt_ref
    dh = D_HEAD
    step = pl.program_id(0)
    ot = ot_ref[...]
    for u in range(sps):
        sel, ocol = _attn_s_one(step * sps + u, rest[u * n_pages:(u + 1) * n_pages],
                                rest[(sps + u) * n_pages:(sps + u + 1) * n_pages], lam_ref[0],
                                qt_ref, knt_ref, vnt_ref, bias_ref, bself_ref, n_pages, q_scale)
        ot = jnp.where(sel, ocol, ot)
    ot_ref[...] = ot

    @pl.when(step == pl.num_programs(0) - 1)
    def _():
        for h in range(N_HEADS):
            oh = ot_ref[h * dh:(h + 1) * dh, :]
            ms = jnp.mean(oh * oh, axis=0, keepdims=True)
            o_ref[h * dh:(h + 1) * dh, :] = oh * lax.rsqrt(ms + EPS) * subg_ref[...]


def _attn_sample(pt, cache_kt, cache_vt, page_table, layer, lp):
    n = pt.shape[1]
    n_pages = page_table.shape[1]
    w = W_GROUP
    dh = D_HEAD
    blk = lambda off: pl.BlockSpec((w, n), lambda i, tbl: (off, 0))
    sps = DEC_SAMPLES_PER_STEP
    page_specs = [
        pl.BlockSpec((None, None, N_HEADS, dh, PAGE),
                     functools.partial(lambda i, tbl, u, j: (layer, tbl[i * sps + u, j], 0, 0, 0), u=u, j=j))
        for u in range(sps) for j in range(n_pages)
    ]
    grid_spec = pltpu.PrefetchScalarGridSpec(
        num_scalar_prefetch=1,
        grid=(n // sps,),
        in_specs=[
            pl.BlockSpec(memory_space=pltpu.SMEM),
            blk(5), blk(6), blk(7),
            pl.BlockSpec((2 * N_HEADS * n_pages, PAGE), lambda i, tbl: (0, 0)),
            pl.BlockSpec((N_HEADS, 1, 1), lambda i, tbl: (0, 0, 0)),
            pl.BlockSpec((dh, 1), lambda i, tbl: (0, 0)),
        ] + page_specs + page_specs,
        out_specs=pl.BlockSpec((w, n), lambda i, tbl: (0, 0)),
        scratch_shapes=[pltpu.VMEM((w, n), F32)],
    )
    return pl.pallas_call(
        functools.partial(_attn_s_kernel, n_pages=n_pages, sps=sps, q_scale=DA_HALF ** -0.5),
        grid_spec=grid_spec,
        out_shape=jax.ShapeDtypeStruct((w, n), F32),
        compiler_params=_params("arbitrary"),
        name="attn_sample",
    )(page_table, lp["lam"], pt, pt, pt, lp["bias_past"], lp["bias_self"], lp["sub_g_col"],
      *([cache_kt] * (sps * n_pages)), *([cache_vt] * (sps * n_pages)))


def _rel_bucket(dist):
    n = jnp.maximum(dist, 0)
    max_exact = REL_BUCKETS // 2
    large = max_exact + (jnp.log(jnp.maximum(n, 1).astype(F32) / max_exact)
                         / math.log(REL_MAX_DIST / max_exact) * (REL_BUCKETS - max_exact)).astype(jnp.int32)
    large = jnp.minimum(large, REL_BUCKETS - 1)
    return jnp.where(n < max_exact, n, large)


def _layer_params(l, depth, tq, tk, past_len, n_pages, prm, lb_all):
    w = W_GROUP
    lam_init = 0.8 - 0.6 * math.exp(-0.3 * l)
    lam = (jnp.exp(jnp.sum(prm["diff_lam_q1"][l] * prm["diff_lam_k1"][l]))
           - jnp.exp(jnp.sum(prm["diff_lam_q2"][l] * prm["diff_lam_k2"][l])) + lam_init)
    rel_bias = prm["rel_bias"]

    def bias_of(dist):
        onehot = _rel_bucket(dist)[..., None] == jnp.arange(REL_BUCKETS, dtype=jnp.int32)
        table = rel_bias.T.reshape((N_HEADS,) + (1,) * dist.ndim + (REL_BUCKETS,))
        return jnp.sum(jnp.where(onehot[None], table, 0.0), axis=-1)

    span = tq + tk - 1
    jj = jnp.arange(span + 1, dtype=jnp.int32)
    tiles = []
    for r in range(-1, tq // tk):
        dist = jj - (tk - 1) - r * tk
        f = jnp.where((dist >= 0)[None], bias_of(dist), NEG)
        sheared = jnp.tile(f, (1, tk))[:, :tk * span].reshape(N_HEADS, tk, span)
        tiles.append(sheared[:, :, tk - 1:tk - 1 + tq])
    bias_near = jnp.stack(tiles, axis=1) * LOG2E
    kpos = jnp.arange(n_pages * PAGE, dtype=jnp.int32)
    bias_past = jnp.tile(bias_of(past_len - kpos).reshape(N_HEADS * n_pages, PAGE), (2, 1))
    lb = lb_all[l]
    tril = jnp.tril(jnp.ones((SG_CHUNK, SG_CHUNK), F32))
    sub_g = prm["diff_subln_g"][l] * (1.0 - lam_init)
    return {
        "ln_g": prm["ln_g"][l], "ln_b": prm["ln_b"][l],
        "pool_w": jax.scipy.linalg.block_diag(*prm["pool_w"][l]).astype(BF16),
        "pool_scale": prm["pool_scale"][l][None],
        "log_lb": jnp.log(lb)[None], "log1m_lb": jnp.log1p(-lb)[None], "om_lb": (1.0 - lb)[None],
        "log_lb_col": jnp.log(lb)[:, None], "log1m_lb_col": jnp.log1p(-lb)[:, None],
        "om_lb_col": (1.0 - lb)[:, None],
        "hg_g": jnp.tile(prm["hgrn_norm_g"][l], N_HEADS)[None], "hg_g_col": prm["hgrn_norm_g"][l][:, None],
        "sg_g": prm["sgu_ln_g"][l][None], "sg_b": prm["sgu_ln_b"][l][None],
        "sg_w": (prm["sgu_w"][l] * tril).reshape(N_HEADS * SG_CHUNK, SG_CHUNK).astype(BF16),
        "sg_bias": jnp.repeat(prm["sgu_b"][l].T, D_HEAD, axis=1),
        "sg_w00": jnp.repeat(prm["sgu_w"][l][:, 0, 0], D_HEAD)[None],
        "sg_b0": jnp.repeat(prm["sgu_b"][l][:, 0], D_HEAD)[None],
        "eones": jnp.kron(jnp.eye(N_HEADS, dtype=F32), jnp.ones((D_HEAD, D_HEAD), F32)).astype(BF16),
        "lam": lam.reshape(1), "c_far": rel_bias[REL_BUCKETS - 1] * LOG2E,
        "bias_near": bias_near,
        "bias_past": bias_past, "bias_self": rel_bias[0].reshape(N_HEADS, 1, 1),
        "sub_g_col": sub_g[:, None],
    }


def kernel(x_prompt, x_sample, state_pool, state_hgrn, cache_k, cache_v, page_table, rel_bias, ln_g, ln_b,
           ffn1_w_gu, ffn1_w_dn, ffn2_w_gu, ffn2_w_dn, w_in, w_out, pool_w, pool_scale, hgrn_lb,
           hgrn_norm_g, diff_lam_q1, diff_lam_k1, diff_lam_q2, diff_lam_k2, diff_subln_g, sgu_ln_g,
           sgu_ln_b, sgu_w, sgu_b):
    prm = dict(rel_bias=rel_bias, ln_g=ln_g, ln_b=ln_b, ffn1_w_gu=ffn1_w_gu, ffn1_w_dn=ffn1_w_dn,
               ffn2_w_gu=ffn2_w_gu, ffn2_w_dn=ffn2_w_dn, w_in=w_in, w_out=w_out, pool_w=pool_w,
               pool_scale=pool_scale, hgrn_norm_g=hgrn_norm_g, diff_lam_q1=diff_lam_q1,
               diff_lam_k1=diff_lam_k1, diff_lam_q2=diff_lam_q2, diff_lam_k2=diff_lam_k2,
               diff_subln_g=diff_subln_g, sgu_ln_g=sgu_ln_g, sgu_ln_b=sgu_ln_b, sgu_w=sgu_w, sgu_b=sgu_b)
    depth = w_in.shape[0]
    nb, t, d = x_prompt.shape
    ns = x_sample.shape[0]
    n_pages = page_table.shape[1]
    past_len = n_pages * PAGE
    alpha = (2.0 * depth) ** 0.25
    w = W_GROUP
    tq, tk = _attn_tiles(t)
    assert x_sample.shape[1] == 1 and t % SG_CHUNK == 0 and t % tq == 0 and tk >= REL_MAX_DIST

    lb_cum = jnp.cumsum(jax.nn.softmax(hgrn_lb.astype(F32), axis=0), axis=0)
    lb_all = jnp.maximum(lb_cum - lb_cum[:1], 0.0)

    cache_kt = jnp.transpose(cache_k, (0, 1, 3, 4, 2))
    cache_vt = jnp.transpose(cache_v, (0, 1, 3, 4, 2))
    hgrn_t = jnp.transpose(state_hgrn, (0, 2, 3, 4, 1))
    pool_t = jnp.transpose(state_pool, (0, 2, 1, 3))

    wb = {k: prm[k].astype(BF16) for k in ("ffn1_w_gu", "ffn1_w_dn", "ffn2_w_gu", "ffn2_w_dn", "w_in", "w_out")}
    w_int = jnp.swapaxes(prm["w_in"], 1, 2).astype(BF16)
    w_qkvt = w_int[:, 5 * w:8 * w]

    xp = x_prompt.reshape(nb * t, d)
    xs = x_sample.reshape(ns, d)
    outs = {k: [] for k in ("pool_p", "pool_s", "hgrn_p", "hgrn_s", "k_p", "k_s", "v_p", "v_s", "sgv_p", "sgv_s")}
    for l in range(depth):
        lp = _layer_params(l, depth, tq, tk, past_len, n_pages, prm, lb_all)
        g = [lp["ln_g"][i][None] for i in range(3)]
        b = [lp["ln_b"][i][None] for i in range(3)]

        xp = _ffn(xp, wb["ffn1_w_gu"], wb["ffn1_w_dn"], l, g[0], b[0], alpha)
        p_a, p_d, k_rows, q_t, k_t, v_t = _inproj_prompt(xp, wb["w_in"], w_qkvt, l, nb)
        y_abd, pool16, st, sgv = _mix_prompt(p_a, p_d, nb, lp)
        y_ct = _attn_prompt(q_t, k_rows, v_t, lp, nb)
        xp = _outproj_ffn_prompt(xp, y_abd, y_ct, wb["w_out"], wb["ffn2_w_gu"], wb["ffn2_w_dn"], l,
                                 g[1], b[1], g[2], b[2], alpha)
        outs["k_p"].append(jnp.transpose(k_t.reshape(nb, N_HEADS, D_HEAD, t), (0, 3, 1, 2)))
        outs["v_p"].append(jnp.transpose(v_t.reshape(nb, N_HEADS, D_HEAD, t), (0, 3, 1, 2)))
        outs["pool_p"].append(pool16[:, 1:])
        outs["hgrn_p"].append(jnp.stack(
            [jnp.swapaxes(st[:, h * D_HEAD:(h + 1) * D_HEAD, h * D_HEAD:(h + 1) * D_HEAD], 1, 2)
             for h in range(N_HEADS)], axis=1))
        outs["sgv_p"].append(sgv)

        xs = _ffn(xs, wb["ffn1_w_gu"], wb["ffn1_w_dn"], l, g[0], b[0], alpha)
        ps, pst = _inproj_sample(xs, wb["w_in"], w_int, l)
        y_a, y_d, new_pool, vn = _mix_sample(ps, pool_t[l], lp, past_len)
        new_state, y_bt = _hgrn_sample(pst, hgrn_t[l], lp)
        y_ct = _attn_sample(pst, cache_kt, cache_vt, page_table, l, lp)
        xs = _outproj_sample(xs, y_a, y_bt, y_ct, y_d, wb["w_out"], l, g[1], b[1], alpha)
        xs = _ffn(xs, wb["ffn2_w_gu"], wb["ffn2_w_dn"], l, g[2], b[2], alpha)
        outs["k_s"].append(jnp.transpose(pst[6 * w:7 * w].reshape(N_HEADS, D_HEAD, ns), (2, 0, 1))[:, None])
        outs["v_s"].append(jnp.transpose(pst[7 * w:8 * w].reshape(N_HEADS, D_HEAD, ns), (2, 0, 1))[:, None])
        outs["pool_s"].append(jnp.transpose(new_pool, (1, 0, 2)))
        outs["hgrn_s"].append(jnp.transpose(new_state, (3, 0, 1, 2)))
        outs["sgv_s"].append(vn[:, None])

    st = {k: jnp.stack(v, axis=0) for k, v in outs.items()}
    return (xp.reshape(nb, t, d), xs.reshape(ns, 1, d), st["pool_p"], st["pool_s"], st["hgrn_p"], st["hgrn_s"],
            st["k_p"], st["k_s"], st["v_p"], st["v_s"], st["sgv_p"], st["sgv_s"])
```

```python
import functools
import math

import jax
import jax.numpy as jnp
import jax.scipy.linalg
from jax import lax
from jax.experimental import pallas as pl
from jax.experimental.pallas import tpu as pltpu

F32 = jnp.float32
BF16 = jnp.bfloat16

N_MIX = 4
W_GROUP = 256
N_HEADS = 4
D_HEAD = 64
DA_HALF = 32
POOL_WINDOWS = (2, 4, 8, 16)
POOL_BUF = 15
PAGE = 128
SG_CHUNK = 128
HG_SUB = 16
V_AUG = 80
REL_BUCKETS = 32
REL_MAX_DIST = 128
EPS = 1e-5
NEG = -1e30
LOG2E = 1.0 / math.log(2.0)

VMEM_LIMIT = 56 * 1024 * 1024
MXU_DIM = 256
ROW_TILE = 512
FFN_CHUNKS = 2
ATT_TQ = 1024
ATT_TK = 512
DEC_SAMPLES_PER_STEP = 2

NT_DIMS = (((1,), (1,)), ((), ()))


def _params(*sem):
    return pltpu.CompilerParams(dimension_semantics=sem, vmem_limit_bytes=VMEM_LIMIT)


def _const_spec(shape, single=False):
    nd = len(shape)
    kw = {"pipeline_mode": pl.Buffered(1)} if single else {}
    return pl.BlockSpec(shape, lambda *_: (0,) * nd, **kw)


def _layer_spec(shape, layer, single=False, rows=None, row_block=0):
    kw = {"pipeline_mode": pl.Buffered(1)} if single else {}
    block = (None, shape[1] if rows is None else rows) + tuple(shape[2:])
    return pl.BlockSpec(block, lambda *_: (layer, row_block) + (0,) * (len(shape) - 2), **kw)


def _layer_norm(y, g, b):
    mu = jnp.mean(y, axis=-1, keepdims=True)
    yc = y - mu
    var = jnp.mean(yc * yc, axis=-1, keepdims=True)
    return yc * lax.rsqrt(var + EPS) * g + b


def _silu(x):
    return x * jax.nn.sigmoid(x)


def _log_forget(z, log_lb, log1m_lb):
    log_sig = jnp.minimum(z, 0.0) - jnp.log(1.0 + jnp.exp(-jnp.abs(z)))
    b = log1m_lb + log_sig
    return jnp.maximum(log_lb, b) + jnp.log(1.0 + jnp.exp(-jnp.abs(log_lb - b)))


def _swiglu_ln(x, wgu_ref, wdn_ref, g, b, d_ff, bounds, alpha):
    xb = x.astype(BF16)
    acc = None
    for lo, hi in zip(bounds[:-1], bounds[1:]):
        gate = jnp.dot(xb, wgu_ref[:, lo:hi], preferred_element_type=F32)
        up = jnp.dot(xb, wgu_ref[:, d_ff + lo:d_ff + hi], preferred_element_type=F32)
        h = (_silu(gate) * up).astype(BF16)
        part = jnp.dot(h, wdn_ref[lo:hi, :], preferred_element_type=F32)
        acc = part if acc is None else acc + part
    return _layer_norm(alpha * x + 0.5 * acc, g, b)


def _ffn_kernel(x_ref, wgu_ref, wdn_ref, g_ref, b_ref, o_ref, *, d_ff, bounds, alpha):
    o_ref[...] = _swiglu_ln(x_ref[...], wgu_ref, wdn_ref, g_ref[...], b_ref[...], d_ff, bounds, alpha)


def _ffn_bounds(d_ff):
    n_tiles = d_ff // MXU_DIM
    cuts = [round(n_tiles * c / FFN_CHUNKS) * MXU_DIM for c in range(FFN_CHUNKS)]
    return tuple(cuts) + (d_ff,)


def _ffn(x, w_gu, w_dn, layer, g, b, alpha):
    r, d = x.shape
    d_ff = w_dn.shape[1]
    tm = min(ROW_TILE, r)
    bounds = _ffn_bounds(d_ff)
    return pl.pallas_call(
        functools.partial(_ffn_kernel, d_ff=d_ff, bounds=bounds, alpha=alpha),
        grid=(r // tm,),
        in_specs=[
            pl.BlockSpec((tm, d), lambda i: (i, 0)),
            _layer_spec(w_gu.shape, layer, single=True),
            _layer_spec(w_dn.shape, layer, single=True),
            _const_spec((1, d)),
            _const_spec((1, d)),
        ],
        out_specs=pl.BlockSpec((tm, d), lambda i: (i, 0)),
        out_shape=jax.ShapeDtypeStruct((r, d), F32),
        compiler_params=_params("arbitrary"),
        name="ffn",
    )(x, w_gu, w_dn, g, b)


def _ffn_inproj_p_kernel(x_ref, wgu_ref, wdn_ref, g_ref, b_ref, w_ref, wqkvt_ref,
                         x1_ref, pa_ref, pd_ref, k_ref, qt_ref, kt_ref, vt_ref, *, d_ff, bounds, alpha, q_scale):
    w = W_GROUP
    x1 = _swiglu_ln(x_ref[...], wgu_ref, wdn_ref, g_ref[...], b_ref[...], d_ff, bounds, alpha)
    x1_ref[...] = x1
    xb = x1.astype(BF16)
    pa_ref[...] = jnp.dot(xb, w_ref[:, 0:5 * w], preferred_element_type=F32)
    pd_ref[...] = jnp.dot(xb, w_ref[:, 8 * w:10 * w], preferred_element_type=F32)
    k = jnp.dot(xb, w_ref[:, 6 * w:7 * w], preferred_element_type=F32)
    for h in range(N_HEADS):
        k_ref[h] = k[:, h * D_HEAD:(h + 1) * D_HEAD].astype(BF16)
    qkvt = lax.dot_general(wqkvt_ref[...], xb, NT_DIMS, preferred_element_type=F32)
    qt_ref[0] = (qkvt[0:w] * q_scale).astype(BF16)
    kt_ref[0] = qkvt[w:2 * w]
    vt_ref[0] = qkvt[2 * w:3 * w]


def _ffn_inproj_prompt(x, w_gu, w_dn, g, b, w_in, w_qkvt, layer, n_batch, alpha):
    r, d = x.shape
    t = r // n_batch
    w = W_GROUP
    tm = min(ROW_TILE, t)
    tpb = t // tm
    d_ff = w_dn.shape[1]
    chan_major = lambda: pl.BlockSpec((1, w, tm), lambda i: (i // tpb, 0, i % tpb))
    return pl.pallas_call(
        functools.partial(_ffn_inproj_p_kernel, d_ff=d_ff, bounds=_ffn_bounds(d_ff), alpha=alpha,
                          q_scale=DA_HALF ** -0.5 * LOG2E),
        grid=(r // tm,),
        in_specs=[
            pl.BlockSpec((tm, d), lambda i: (i, 0)),
            _layer_spec(w_gu.shape, layer, single=True),
            _layer_spec(w_dn.shape, layer, single=True),
            _const_spec((1, d)), _const_spec((1, d)),
            _layer_spec(w_in.shape, layer, single=True),
            _layer_spec(w_qkvt.shape, layer, single=True),
        ],
        out_specs=[
            pl.BlockSpec((tm, d), lambda i: (i, 0)),
            pl.BlockSpec((tm, 5 * w), lambda i: (i, 0)),
            pl.BlockSpec((tm, 2 * w), lambda i: (i, 0)),
            pl.BlockSpec((N_HEADS, tm, D_HEAD), lambda i: (0, i, 0)),
            chan_major(), chan_major(), chan_major(),
        ],
        out_shape=[
            jax.ShapeDtypeStruct((r, d), F32),
            jax.ShapeDtypeStruct((r, 5 * w), F32),
            jax.ShapeDtypeStruct((r, 2 * w), F32),
            jax.ShapeDtypeStruct((N_HEADS, r, D_HEAD), BF16),
            jax.ShapeDtypeStruct((n_batch, w, t), BF16),
            jax.ShapeDtypeStruct((n_batch, w, t), F32),
            jax.ShapeDtypeStruct((n_batch, w, t), F32),
        ],
        compiler_params=_params("arbitrary"),
        name="ffn_inproj_prompt",
    )(x, w_gu, w_dn, g, b, w_in, w_qkvt)


def _inproj_s_kernel(x_ref, w_ref, wt_ref, p_ref, pt_ref):
    xb = x_ref[...].astype(BF16)
    p_ref[...] = jnp.dot(xb, w_ref[...], preferred_element_type=F32)
    pt_ref[...] = lax.dot_general(wt_ref[...], xb, NT_DIMS, preferred_element_type=F32)


def _inproj_sample(x, w_in, w_int, layer):
    n, d = x.shape
    d_in = w_in.shape[2]
    return pl.pallas_call(
        _inproj_s_kernel,
        grid=(1,),
        in_specs=[_const_spec((n, d)), _layer_spec(w_in.shape, layer), _layer_spec(w_int.shape, layer)],
        out_specs=[_const_spec((n, d_in)), _const_spec((d_in, n))],
        out_shape=[jax.ShapeDtypeStruct((n, d_in), F32), jax.ShapeDtypeStruct((d_in, n), F32)],
        compiler_params=_params("arbitrary"),
        name="inproj_sample",
    )(x, w_in, w_int)


def _group_select(grp, parts):
    out = parts[N_HEADS - 1]
    for g in range(N_HEADS - 2, -1, -1):
        out = jnp.where(grp == g, parts[g], out)
    return out


def _mix_p_kernel(pa_ref, pd_ref, poolw_ref, pscale_ref, loglb_ref, log1mlb_ref, omlb_ref, hgg_ref,
                  sgg_ref, sgb_ref, sgw_ref, sgbias_ref, eones_ref,
                  y_ref, pool_ref, st_out_ref, sgv_ref,
                  prev_ref, st_ref):
    c = SG_CHUNK
    w = W_GROUP
    t = pl.program_id(1)

    @pl.when(t == 0)
    def _():
        prev_ref[...] = jnp.zeros_like(prev_ref)
        st_ref[...] = jnp.zeros_like(st_ref)

    a = pa_ref[:, 0:w]
    hq = pa_ref[:, w:2 * w]
    hf = pa_ref[:, 2 * w:3 * w]
    hi = pa_ref[:, 3 * w:4 * w]
    hg = pa_ref[:, 4 * w:5 * w]
    su = pd_ref[:, 0:w]
    sv = pd_ref[:, w:2 * w]
    lane = lax.broadcasted_iota(jnp.int32, (c, w), 1)
    row = lax.broadcasted_iota(jnp.int32, (c, w), 0)
    grp = lane >> 6

    e = jnp.concatenate([prev_ref[...], a], axis=0)
    s2 = e + pltpu.roll(e, 1, 0)
    s4 = s2 + pltpu.roll(s2, 2, 0)
    s8 = s4 + pltpu.roll(s4, 4, 0)
    s16 = s8 + pltpu.roll(s8, 8, 0)
    wsum = _group_select(grp, [s2[16:], s4[16:], s8[16:], s16[16:]])
    win = _group_select(grp, [jnp.full((c, w), v, jnp.int32) for v in POOL_WINDOWS])
    cnt = jnp.minimum(t * c + row + 1, win).astype(F32)
    dpool = wsum / cnt - a
    ya = jnp.dot(dpool.astype(BF16), poolw_ref[...], preferred_element_type=F32) * pscale_ref[...]
    prev_ref[...] = a[c - 16:]
    pool_ref[0] = a[c - 16:]

    vn = _layer_norm(sv, sgg_ref[...], sgb_ref[...])
    sg = jnp.dot(sgw_ref[...], vn.astype(BF16), preferred_element_type=F32)
    s_gate = _group_select(grp, [sg[g * c:(g + 1) * c] for g in range(N_HEADS)]) + sgbias_ref[...]
    yd = su * s_gate
    sgv_ref[0] = vn

    n_sub = c // HG_SUB
    logf = _log_forget(hf, loglb_ref[...], log1mlb_ref[...])
    kin = omlb_ref[...] * jax.nn.sigmoid(-hf)
    r16 = row & (HG_SUB - 1)
    bl = logf
    rv = logf
    for sh in (1, 2, 4, 8):
        bl = bl + jnp.where(r16 >= sh, pltpu.roll(bl, sh, 0), 0.0)
        rv = rv + jnp.where(r16 + sh < HG_SUB, pltpu.roll(rv, c - sh, 0), 0.0)
    sub_row = row >> 4
    qtb = (hq * jnp.exp(bl)).astype(BF16)
    kt = kin * jnp.exp(rv - logf)
    dec = jnp.exp(rv)
    vtb = hi.T.astype(BF16)
    bi0 = lax.broadcasted_iota(jnp.int32, (w, w), 0) >> 6
    bi1 = lax.broadcasted_iota(jnp.int32, (w, w), 1) >> 6
    blockmask = bi0 == bi1
    st = st_ref[...]
    seen = []
    for i in range(n_sub):
        seen.append(st.astype(BF16))
        km = jnp.where(sub_row == i, kt, 0.0).astype(BF16)
        u = jnp.dot(vtb, km, preferred_element_type=F32)
        st = dec[i * HG_SUB:i * HG_SUB + 1] * st + jnp.where(blockmask, u, 0.0)
    st_ref[...] = st
    st_out_ref[0] = st
    qx = jnp.concatenate([jnp.where(sub_row == i, qtb, jnp.zeros_like(qtb)) for i in range(n_sub)], axis=1)
    o = lax.dot_general(qx, jnp.concatenate(seen, axis=1), NT_DIMS,
                        preferred_element_type=F32)

    rt = {lo: lax.broadcasted_iota(jnp.int32, (HG_SUB - lo, w), 0) + lo for lo in (0, 8)}
    bl2 = bl * LOG2E
    xs = []
    for i in range(n_sub):
        rows = slice(i * HG_SUB, (i + 1) * HG_SUB)
        bli, qi, ki = bl2[rows], hq[rows], kin[rows]
        for s in range(HG_SUB):
            lo = (s // 8) * 8
            ratio = jnp.exp2(jnp.where(rt[lo] >= s, bli[lo:] - bli[s:s + 1], NEG))
            x_s = ratio * qi[lo:] * ki[s:s + 1]
            if lo:
                x_s = jnp.concatenate([jnp.zeros((lo, w), F32), x_s], axis=0)
            xs.append(x_s.astype(BF16))
    x = jnp.concatenate(xs, axis=0)
    r = jnp.dot(x, eones_ref[...], preferred_element_type=F32)
    o_intra = []
    for i in range(n_sub):
        vi = hi[i * HG_SUB:(i + 1) * HG_SUB]
        base = i * HG_SUB * HG_SUB
        oi = r[base:base + HG_SUB] * vi[0:1]
        for s in range(1, HG_SUB):
            oi = oi + r[base + s * HG_SUB:base + (s + 1) * HG_SUB] * vi[s:s + 1]
        o_intra.append(oi)
    o = o + jnp.concatenate(o_intra, axis=0)
    ms = [jnp.mean(jnp.square(o[:, g * D_HEAD:(g + 1) * D_HEAD]), axis=-1, keepdims=True)
          for g in range(N_HEADS)]
    yb = o * lax.rsqrt(_group_select(grp, ms) + EPS) * hgg_ref[...] * _silu(hg)

    y_ref[:, 0:w] = ya.astype(BF16)
    y_ref[:, w:2 * w] = yb.astype(BF16)
    y_ref[:, 2 * w:3 * w] = yd.astype(BF16)


def _mix_prompt(p_a, p_d, n_batch, lp):
    r = p_a.shape[0]
    t = r // n_batch
    c = SG_CHUNK
    nt = t // c
    w = W_GROUP
    row = lambda: _const_spec((1, w))
    return pl.pallas_call(
        _mix_p_kernel,
        grid=(n_batch, nt),
        in_specs=[
            pl.BlockSpec((c, 5 * w), lambda b, i: (b * nt + i, 0)),
            pl.BlockSpec((c, 2 * w), lambda b, i: (b * nt + i, 0)),
            _const_spec((w, w)), row(), row(), row(), row(), row(), row(), row(),
            _const_spec((N_HEADS * c, c)), _const_spec((c, w)), _const_spec((w, w)),
        ],
        out_specs=[
            pl.BlockSpec((c, 3 * w), lambda b, i: (b * nt + i, 0)),
            pl.BlockSpec((1, 16, w), lambda b, i: (b, 0, 0)),
            pl.BlockSpec((1, w, w), lambda b, i: (b, 0, 0)),
            pl.BlockSpec((1, c, w), lambda b, i: (b, 0, 0)),
        ],
        out_shape=[
            jax.ShapeDtypeStruct((r, 3 * w), BF16),
            jax.ShapeDtypeStruct((n_batch, 16, w), F32),
            jax.ShapeDtypeStruct((n_batch, w, w), F32),
            jax.ShapeDtypeStruct((n_batch, c, w), F32),
        ],
        scratch_shapes=[pltpu.VMEM((16, w), F32), pltpu.VMEM((w, w), F32)],
        compiler_params=_params("arbitrary", "arbitrary"),
        name="mix_prompt",
    )(p_a, p_d, lp["pool_w"], lp["pool_scale"], lp["log_lb"], lp["log1m_lb"], lp["om_lb"], lp["hg_g"],
      lp["sg_g"], lp["sg_b"], lp["sg_w"], lp["sg_bias"], lp["eones"])


def _attn_p_kernel(cfar_ref, lam_ref, qt_ref, k_ref, vt_ref, near_ref, subg_ref, o_ref,
                   va_ref, m_ref, acc_ref, s_ref, mb_ref, *, tq, tk):
    h = pl.program_id(1)
    qi = pl.program_id(2)
    dh = D_HEAD

    @pl.when(qi == 0)
    def _():
        va_ref[0:dh, :] = vt_ref[0].astype(BF16)
        rows = lax.broadcasted_iota(jnp.int32, (V_AUG - dh, va_ref.shape[1]), 0)
        va_ref[dh:V_AUG, :] = jnp.where(rows == 0, 1.0, 0.0).astype(BF16)

    qt = qt_ref[0]
    sub = lax.broadcasted_iota(jnp.int32, qt.shape, 0)
    zero = jnp.zeros_like(qt)
    qm = (jnp.where(sub < DA_HALF, qt, zero), jnp.where(sub >= DA_HALF, qt, zero))
    m_ref[...] = jnp.full_like(m_ref, NEG)
    acc_ref[...] = jnp.zeros_like(acc_ref)

    def qk(ki, slot, near=None, q0=0):
        k0 = pl.multiple_of(ki * tk, tk)
        kblk = k_ref[0, pl.ds(k0, tk), :]
        for mi in range(2):
            s = jnp.dot(kblk, qm[mi][:, q0:], preferred_element_type=F32)
            if near is not None:
                s = s + near_ref[0, near, :, q0:]
            s_ref[slot, mi, :, q0:] = s
            mb_ref[slot, mi, :, q0:] = jnp.max(s, axis=0, keepdims=True)

    def softmax_pv(ki, slot, shift, q0=0):
        k0 = pl.multiple_of(ki * tk, tk)
        vblk = va_ref[:, pl.ds(k0, tk)]
        for mi in range(2):
            s = s_ref[slot, mi, :, q0:]
            m_blk = mb_ref[slot, mi, :, q0:]
            if shift is not None:
                m_blk = m_blk + shift
            m_old = m_ref[mi, :, q0:]
            m_new = jnp.maximum(m_old, m_blk)
            alpha = jnp.exp2(m_old - m_new)
            p = jnp.exp2(s - (m_new if shift is None else m_new - shift)).astype(BF16)
            acc_ref[mi, :, q0:] = alpha * acc_ref[mi, :, q0:] + jnp.dot(vblk, p, preferred_element_type=F32)
            m_ref[mi, :, q0:] = m_new

    cfar = cfar_ref[h]
    n_diag = tq // tk
    kb0 = qi * n_diag

    def near_tiles(first_tile, first_near, n, slot0):
        q0s = [max(first_near + j - 1, 0) * tk for j in range(n)]
        for j in range(n):
            if j + 1 < n:
                qk(first_tile + j + 1, (slot0 + j + 1) % 2, first_near + j + 1, q0s[j + 1])
            softmax_pv(first_tile + j, (slot0 + j) % 2, None, q0s[j])

    def far_pair(pi, carry):
        b = 2 * pi
        qk(b + 1, 1)
        softmax_pv(b, 0, cfar)
        qk(b + 2, 0)
        softmax_pv(b + 1, 1, cfar)
        return carry

    @pl.when(qi == 0)
    def _():
        qk(0, 0, 1)
        near_tiles(0, 1, n_diag, 0)

    @pl.when(qi >= 1)
    def _():
        n_far = kb0 - 1
        qk(0, 0)
        lax.fori_loop(0, (n_far - 1) // 2, far_pair, 0)

        def tail(rem):
            if rem == 2:
                qk(n_far - 1, 1)
                softmax_pv(n_far - 2, 0, cfar)
            last_slot = rem % 2
            qk(n_far, last_slot, 0)
            softmax_pv(n_far - 1, 1 - last_slot, cfar)
            near_tiles(n_far, 0, n_diag + 1, last_slot)

        if n_diag % 2 == 0:
            tail(1)
        else:
            pl.when(n_far % 2 == 1)(lambda: tail(1))
            pl.when(n_far % 2 == 0)(lambda: tail(2))

    a0 = acc_ref[0]
    a1 = acc_ref[1]
    o = a0[0:dh] / a0[dh:dh + 1] - lam_ref[0] * (a1[0:dh] / a1[dh:dh + 1])
    ms = jnp.mean(o * o, axis=0, keepdims=True)
    o_ref[0] = (o * lax.rsqrt(ms + EPS) * subg_ref[...]).astype(BF16)


def _attn_tiles(t):
    tq = min(ATT_TQ, t)
    return tq, min(ATT_TK, tq // 2)


def _attn_prompt(qt, k_rows, vt, lp, n_batch):
    t = qt.shape[2]
    tq, tk = _attn_tiles(t)
    nq = t // tq
    n_near = tq // tk + 1
    smem = pl.BlockSpec(memory_space=pltpu.SMEM)
    return pl.pallas_call(
        functools.partial(_attn_p_kernel, tq=tq, tk=tk),
        grid=(n_batch, N_HEADS, nq),
        in_specs=[
            smem, smem,
            pl.BlockSpec((1, D_HEAD, tq), lambda b, h, i: (b, h, i)),
            pl.BlockSpec((1, t, D_HEAD), lambda b, h, i: (h, b, 0)),
            pl.BlockSpec((1, D_HEAD, t), lambda b, h, i: (b, h, 0)),
            pl.BlockSpec((1, n_near, tk, tq), lambda b, h, i: (h, 0, 0, 0)),
            _const_spec((D_HEAD, 1)),
        ],
        out_specs=pl.BlockSpec((1, D_HEAD, tq), lambda b, h, i: (b, h, i)),
        out_shape=jax.ShapeDtypeStruct((n_batch, W_GROUP, t), BF16),
        scratch_shapes=[
            pltpu.VMEM((V_AUG, t), BF16),
            pltpu.VMEM((2, 1, tq), F32), pltpu.VMEM((2, V_AUG, tq), F32),
            pltpu.VMEM((2, 2, tk, tq), F32), pltpu.VMEM((2, 2, 1, tq), F32),
        ],
        compiler_params=_params("arbitrary", "arbitrary", "arbitrary"),
        name="attn_prompt",
    )(lp["c_far"], lp["lam"], qt, k_rows, vt, lp["bias_near"], lp["sub_g_col"])


def _outproj_ffn_p_kernel(x_ref, y_ref, yct_ref, wab_ref, wc_ref, wd_ref, g1_ref, b1_ref,
                          wgu_ref, wdn_ref, g2_ref, b2_ref, o_ref, *, d_ff, bounds, alpha):
    w = W_GROUP
    acc = jnp.dot(y_ref[:, 0:2 * w], wab_ref[...], preferred_element_type=F32)
    acc = acc + jnp.dot(y_ref[:, 2 * w:3 * w], wd_ref[...], preferred_element_type=F32)
    yc = yct_ref[0].astype(F32).T.astype(BF16)
    acc = acc + jnp.dot(yc, wc_ref[...], preferred_element_type=F32)
    x2 = _layer_norm(alpha * x_ref[...] + acc, g1_ref[...], b1_ref[...])
    o_ref[...] = _swiglu_ln(x2, wgu_ref, wdn_ref, g2_ref[...], b2_ref[...], d_ff, bounds, alpha)


def _outproj_ffn_prompt(x, y_abd, y_ct, w_out, w_gu, w_dn, layer, g1, b1, g2, b2, alpha):
    r, d = x.shape
    w = W_GROUP
    t = y_ct.shape[2]
    tm = min(ROW_TILE, t)
    tpb = t // tm
    d_ff = w_dn.shape[1]
    return pl.pallas_call(
        functools.partial(_outproj_ffn_p_kernel, d_ff=d_ff, bounds=_ffn_bounds(d_ff), alpha=alpha),
        grid=(r // tm,),
        in_specs=[
            pl.BlockSpec((tm, d), lambda i: (i, 0)),
            pl.BlockSpec((tm, 3 * W_GROUP), lambda i: (i, 0)),
            pl.BlockSpec((1, W_GROUP, tm), lambda i: (i // tpb, 0, i % tpb)),
            _layer_spec(w_out.shape, layer, rows=2 * w, row_block=0),
            _layer_spec(w_out.shape, layer, rows=w, row_block=2),
            _layer_spec(w_out.shape, layer, rows=w, row_block=3),
            _const_spec((1, d)), _const_spec((1, d)),
            _layer_spec(w_gu.shape, layer, single=True),
            _layer_spec(w_dn.shape, layer, single=True),
            _const_spec((1, d)), _const_spec((1, d)),
        ],
        out_specs=pl.BlockSpec((tm, d), lambda i: (i, 0)),
        out_shape=jax.ShapeDtypeStruct((r, d), F32),
        compiler_params=_params("arbitrary"),
        name="outproj_ffn_prompt",
    )(x, y_abd, y_ct, w_out, w_out, w_out, g1, b1, w_gu, w_dn, g2, b2)


def _outproj_s_kernel(x_ref, ya_ref, ybt_ref, yct_ref, yd_ref, w_ref, g_ref, b_ref, o_ref, *, alpha):
    w = W_GROUP
    parts = (ya_ref[...], ybt_ref[...].T, yct_ref[...].T, yd_ref[...])
    acc = None
    for j, part in enumerate(parts):
        d = jnp.dot(part.astype(BF16), w_ref[j * w:(j + 1) * w, :], preferred_element_type=F32)
        acc = d if acc is None else acc + d
    o_ref[...] = _layer_norm(alpha * x_ref[...] + acc, g_ref[...], b_ref[...])


def _outproj_sample(x, y_a, y_bt, y_ct, y_d, w_out, layer, g, b, alpha):
    n, d = x.shape
    w = W_GROUP
    return pl.pallas_call(
        functools.partial(_outproj_s_kernel, alpha=alpha),
        grid=(1,),
        in_specs=[
            _const_spec((n, d)), _const_spec((n, w)), _const_spec((w, n)), _const_spec((w, n)),
            _const_spec((n, w)), _layer_spec(w_out.shape, layer), _const_spec((1, d)), _const_spec((1, d)),
        ],
        out_specs=_const_spec((n, d)),
        out_shape=jax.ShapeDtypeStruct((n, d), F32),
        compiler_params=_params("arbitrary"),
        name="outproj_sample",
    )(x, y_a, y_bt, y_ct, y_d, w_out, g, b)


def _mix_s_kernel(pa_ref, pd_ref, pool_ref, poolw_ref, pscale_ref, sgg_ref, sgb_ref, w00_ref, b0_ref,
                  ya_ref, yd_ref, newpool_ref, vn_ref, *, cnts):
    a = pa_ref[...]
    n, w = a.shape
    grp = lax.broadcasted_iota(jnp.int32, (n, w), 1) >> 6
    acc = a
    means = []
    for j in range(1, POOL_BUF + 1):
        acc = acc + pool_ref[POOL_BUF - j]
        if j + 1 in POOL_WINDOWS:
            means.append(acc / cnts[POOL_WINDOWS.index(j + 1)])
    dpool = _group_select(grp, means) - a
    ya_ref[...] = jnp.dot(dpool.astype(BF16), poolw_ref[...], preferred_element_type=F32) * pscale_ref[...]
    for j in range(POOL_BUF - 1):
        newpool_ref[j] = pool_ref[j + 1]
    newpool_ref[POOL_BUF - 1] = a

    vn = _layer_norm(pd_ref[:, w:2 * w], sgg_ref[...], sgb_ref[...])
    vn_ref[...] = vn
    yd_ref[...] = pd_ref[:, 0:w] * (w00_ref[...] * vn + b0_ref[...])


def _mix_sample(p, pool, lp, past_len):
    n = p.shape[0]
    w = W_GROUP
    cnts = tuple(float(min(past_len + 1, win)) for win in POOL_WINDOWS)
    row = lambda: _const_spec((1, w))
    return pl.pallas_call(
        functools.partial(_mix_s_kernel, cnts=cnts),
        grid=(1,),
        in_specs=[
            pl.BlockSpec((n, w), lambda i: (0, 0)),
            pl.BlockSpec((n, 2 * w), lambda i: (0, 4)),
            _const_spec(pool.shape), _const_spec((w, w)), row(), row(), row(), row(), row(),
        ],
        out_specs=[_const_spec((n, w)), _const_spec((n, w)), _const_spec(pool.shape), _const_spec((n, w))],
        out_shape=[
            jax.ShapeDtypeStruct((n, w), F32), jax.ShapeDtypeStruct((n, w), F32),
            jax.ShapeDtypeStruct(pool.shape, F32), jax.ShapeDtypeStruct((n, w), F32),
        ],
        compiler_params=_params("arbitrary"),
        name="mix_sample",
    )(p, p, pool, lp["pool_w"], lp["pool_scale"], lp["sg_g"], lp["sg_b"], lp["sg_w00"], lp["sg_b0"])


def _hgrn_s_kernel(q_ref, f_ref, i_ref, g_ref, s_ref, loglb_ref, log1mlb_ref, omlb_ref, hgg_ref,
                   so_ref, yb_ref):
    z = f_ref[...]
    f = jnp.exp(_log_forget(z, loglb_ref[...], log1mlb_ref[...]))
    kin = omlb_ref[...] * jax.nn.sigmoid(-z)
    q = q_ref[...]
    v = i_ref[...]
    o = jnp.zeros_like(v)
    for d in range(D_HEAD):
        sn = f[d:d + 1] * s_ref[0, d] + kin[d:d + 1] * v
        so_ref[0, d] = sn
        o = o + q[d:d + 1] * sn
    ms = jnp.mean(o * o, axis=0, keepdims=True)
    yb_ref[...] = o * lax.rsqrt(ms + EPS) * hgg_ref[...] * _silu(g_ref[...])


def _hgrn_sample(pt, state, lp):
    n = pt.shape[1]
    dh = D_HEAD
    blk = lambda off: pl.BlockSpec((dh, n), lambda h: (off * N_HEADS + h, 0))
    col = lambda: pl.BlockSpec((dh, 1), lambda h: (h, 0))
    return pl.pallas_call(
        _hgrn_s_kernel,
        grid=(N_HEADS,),
        in_specs=[
            blk(1), blk(2), blk(3), blk(4),
            pl.BlockSpec((1, dh, dh, n), lambda h: (h, 0, 0, 0)),
            col(), col(), col(), _const_spec((dh, 1)),
        ],
        out_specs=[
            pl.BlockSpec((1, dh, dh, n), lambda h: (h, 0, 0, 0)),
            pl.BlockSpec((dh, n), lambda h: (h, 0)),
        ],
        out_shape=[
            jax.ShapeDtypeStruct(state.shape, F32),
            jax.ShapeDtypeStruct((W_GROUP, n), F32),
        ],
        compiler_params=_params("arbitrary"),
        name="hgrn_sample",
    )(pt, pt, pt, pt, state, lp["log_lb_col"], lp["log1m_lb_col"], lp["om_lb_col"], lp["hg_g_col"])


def _attn_s_one(n, k_pages, v_pages, lam, qt_ref, knt_ref, vnt_ref, bias_ref, bself_ref, n_pages, q_scale):
    w, ns = qt_ref.shape
    dh = D_HEAD
    sel = lax.broadcasted_iota(jnp.int32, (w, ns), 1) == n

    def column(ref):
        return jnp.sum(jnp.where(sel, ref[...], 0.0), axis=1, keepdims=True)

    qcol = column(qt_ref) * q_scale
    kcol = column(knt_ref)
    vcol = column(vnt_ref)

    n_grp = 2 * N_HEADS
    rows = ([], [])
    self_rows = ([], [])
    for h in range(N_HEADS):
        qh = qcol[h * dh:(h + 1) * dh]
        for j in range(n_pages):
            prod = k_pages[j][h] * qh
            rows[0].append(jnp.sum(prod[0:DA_HALF], axis=0, keepdims=True))
            rows[1].append(jnp.sum(prod[DA_HALF:dh], axis=0, keepdims=True))
        self_prod = qh * kcol[h * dh:(h + 1) * dh]
        for mi in range(2):
            s_self = (jnp.sum(self_prod[mi * DA_HALF:(mi + 1) * DA_HALF], axis=0, keepdims=True)
                      + bself_ref[h])
            self_rows[mi].append(jnp.broadcast_to(s_self, (n_pages, 1)))
    s = jnp.concatenate(rows[0] + rows[1], axis=0) + bias_ref[...]
    s_self = jnp.concatenate(self_rows[0] + self_rows[1], axis=0)

    def per_group(col, reduce):
        parts = [jnp.broadcast_to(reduce(col[g * n_pages:(g + 1) * n_pages], axis=0, keepdims=True),
                                  (n_pages, 1)) for g in range(n_grp)]
        return jnp.concatenate(parts, axis=0)

    m = jnp.maximum(per_group(jnp.max(s, axis=1, keepdims=True), jnp.max), s_self)
    p = jnp.exp(s - m)
    p_self = jnp.exp(s_self - m)
    inv_l = 1.0 / (per_group(jnp.sum(p, axis=1, keepdims=True), jnp.sum) + p_self)
    half = N_HEADS * n_pages
    a = (p * inv_l)[0:half] - lam * (p * inv_l)[half:2 * half]
    a_self = (p_self * inv_l)[0:half] - lam * (p_self * inv_l)[half:2 * half]
    weighted = []
    for h in range(N_HEADS):
        oh = v_pages[0][h] * a[h * n_pages:h * n_pages + 1]
        for j in range(1, n_pages):
            oh = oh + v_pages[j][h] * a[h * n_pages + j:h * n_pages + j + 1]
        weighted.append(oh)
    a_self_col = jnp.concatenate(
        [jnp.broadcast_to(a_self[h * n_pages:h * n_pages + 1], (dh, 1)) for h in range(N_HEADS)], axis=0)
    ocol = jnp.sum(jnp.concatenate(weighted, axis=0), axis=1, keepdims=True) + a_self_col * vcol
    return sel, ocol


def _attn_s_kernel(pt_ref, lam_ref, qt_ref, knt_ref, vnt_ref, bias_ref, bself_ref, subg_ref, *rest,
                   n_pages, sps, q_scale):
    o_ref = rest[2 * sps * n_pages]
    ot_ref = rest[2 * sps * n_pages + 1]
    del pt_ref
    dh = D_HEAD
    step = pl.program_id(0)
    ot = ot_ref[...]
    for u in range(sps):
        sel, ocol = _attn_s_one(step * sps + u, rest[u * n_pages:(u + 1) * n_pages],
                                rest[(sps + u) * n_pages:(sps + u + 1) * n_pages], lam_ref[0],
                                qt_ref, knt_ref, vnt_ref, bias_ref, bself_ref, n_pages, q_scale)
        ot = jnp.where(sel, ocol, ot)
    ot_ref[...] = ot

    @pl.when(step == pl.num_programs(0) - 1)
    def _():
        for h in range(N_HEADS):
            oh = ot_ref[h * dh:(h + 1) * dh, :]
            ms = jnp.mean(oh * oh, axis=0, keepdims=True)
            o_ref[h * dh:(h + 1) * dh, :] = oh * lax.rsqrt(ms + EPS) * subg_ref[...]


def _attn_sample(pt, cache_kt, cache_vt, page_table, layer, lp):
    n = pt.shape[1]
    n_pages = page_table.shape[1]
    w = W_GROUP
    dh = D_HEAD
    blk = lambda off: pl.BlockSpec((w, n), lambda i, tbl: (off, 0))
    sps = DEC_SAMPLES_PER_STEP
    page_specs = [
        pl.BlockSpec((None, None, N_HEADS, dh, PAGE),
                     functools.partial(lambda i, tbl, u, j: (layer, tbl[i * sps + u, j], 0, 0, 0), u=u, j=j))
        for u in range(sps) for j in range(n_pages)
    ]
    grid_spec = pltpu.PrefetchScalarGridSpec(
        num_scalar_prefetch=1,
        grid=(n // sps,),
        in_specs=[
            pl.BlockSpec(memory_space=pltpu.SMEM),
            blk(5), blk(6), blk(7),
            pl.BlockSpec((2 * N_HEADS * n_pages, PAGE), lambda i, tbl: (0, 0)),
            pl.BlockSpec((N_HEADS, 1, 1), lambda i, tbl: (0, 0, 0)),
            pl.BlockSpec((dh, 1), lambda i, tbl: (0, 0)),
        ] + page_specs + page_specs,
        out_specs=pl.BlockSpec((w, n), lambda i, tbl: (0, 0)),
        scratch_shapes=[pltpu.VMEM((w, n), F32)],
    )
    return pl.pallas_call(
        functools.partial(_attn_s_kernel, n_pages=n_pages, sps=sps, q_scale=DA_HALF ** -0.5),
        grid_spec=grid_spec,
        out_shape=jax.ShapeDtypeStruct((w, n), F32),
        compiler_params=_params("arbitrary"),
        name="attn_sample",
    )(page_table, lp["lam"], pt, pt, pt, lp["bias_past"], lp["bias_self"], lp["sub_g_col"],
      *([cache_kt] * (sps * n_pages)), *([cache_vt] * (sps * n_pages)))


def _rel_bucket(dist):
    n = jnp.maximum(dist, 0)
    max_exact = REL_BUCKETS // 2
    large = max_exact + (jnp.log(jnp.maximum(n, 1).astype(F32) / max_exact)
                         / math.log(REL_MAX_DIST / max_exact) * (REL_BUCKETS - max_exact)).astype(jnp.int32)
    large = jnp.minimum(large, REL_BUCKETS - 1)
    return jnp.where(n < max_exact, n, large)


def _layer_params(l, depth, tq, tk, past_len, n_pages, prm, lb_all):
    w = W_GROUP
    lam_init = 0.8 - 0.6 * math.exp(-0.3 * l)
    lam = (jnp.exp(jnp.sum(prm["diff_lam_q1"][l] * prm["diff_lam_k1"][l]))
           - jnp.exp(jnp.sum(prm["diff_lam_q2"][l] * prm["diff_lam_k2"][l])) + lam_init)
    rel_bias = prm["rel_bias"]

    def bias_of(dist):
        onehot = _rel_bucket(dist)[..., None] == jnp.arange(REL_BUCKETS, dtype=jnp.int32)
        table = rel_bias.T.reshape((N_HEADS,) + (1,) * dist.ndim + (REL_BUCKETS,))
        return jnp.sum(jnp.where(onehot[None], table, 0.0), axis=-1)

    span = tq + tk - 1
    jj = jnp.arange(span + 1, dtype=jnp.int32)
    tiles = []
    for r in range(-1, tq // tk):
        dist = jj - (tk - 1) - r * tk
        f = jnp.where((dist >= 0)[None], bias_of(dist), NEG)
        sheared = jnp.tile(f, (1, tk))[:, :tk * span].reshape(N_HEADS, tk, span)
        tiles.append(sheared[:, :, tk - 1:tk - 1 + tq])
    bias_near = jnp.stack(tiles, axis=1) * LOG2E
    kpos = jnp.arange(n_pages * PAGE, dtype=jnp.int32)
    bias_past = jnp.tile(bias_of(past_len - kpos).reshape(N_HEADS * n_pages, PAGE), (2, 1))
    lb = lb_all[l]
    tril = jnp.tril(jnp.ones((SG_CHUNK, SG_CHUNK), F32))
    sub_g = prm["diff_subln_g"][l] * (1.0 - lam_init)
    return {
        "ln_g": prm["ln_g"][l], "ln_b": prm["ln_b"][l],
        "pool_w": jax.scipy.linalg.block_diag(*prm["pool_w"][l]).astype(BF16),
        "pool_scale": prm["pool_scale"][l][None],
        "log_lb": jnp.log(lb)[None], "log1m_lb": jnp.log1p(-lb)[None], "om_lb": (1.0 - lb)[None],
        "log_lb_col": jnp.log(lb)[:, None], "log1m_lb_col": jnp.log1p(-lb)[:, None],
        "om_lb_col": (1.0 - lb)[:, None],
        "hg_g": jnp.tile(prm["hgrn_norm_g"][l], N_HEADS)[None], "hg_g_col": prm["hgrn_norm_g"][l][:, None],
        "sg_g": prm["sgu_ln_g"][l][None], "sg_b": prm["sgu_ln_b"][l][None],
        "sg_w": (prm["sgu_w"][l] * tril).reshape(N_HEADS * SG_CHUNK, SG_CHUNK).astype(BF16),
        "sg_bias": jnp.repeat(prm["sgu_b"][l].T, D_HEAD, axis=1),
        "sg_w00": jnp.repeat(prm["sgu_w"][l][:, 0, 0], D_HEAD)[None],
        "sg_b0": jnp.repeat(prm["sgu_b"][l][:, 0], D_HEAD)[None],
        "eones": jnp.kron(jnp.eye(N_HEADS, dtype=F32), jnp.ones((D_HEAD, D_HEAD), F32)).astype(BF16),
        "lam": lam.reshape(1), "c_far": rel_bias[REL_BUCKETS - 1] * LOG2E,
        "bias_near": bias_near,
        "bias_past": bias_past, "bias_self": rel_bias[0].reshape(N_HEADS, 1, 1),
        "sub_g_col": sub_g[:, None],
    }


def kernel(x_prompt, x_sample, state_pool, state_hgrn, cache_k, cache_v, page_table, rel_bias, ln_g, ln_b,
           ffn1_w_gu, ffn1_w_dn, ffn2_w_gu, ffn2_w_dn, w_in, w_out, pool_w, pool_scale, hgrn_lb,
           hgrn_norm_g, diff_lam_q1, diff_lam_k1, diff_lam_q2, diff_lam_k2, diff_subln_g, sgu_ln_g,
           sgu_ln_b, sgu_w, sgu_b):
    prm = dict(rel_bias=rel_bias, ln_g=ln_g, ln_b=ln_b, ffn1_w_gu=ffn1_w_gu, ffn1_w_dn=ffn1_w_dn,
               ffn2_w_gu=ffn2_w_gu, ffn2_w_dn=ffn2_w_dn, w_in=w_in, w_out=w_out, pool_w=pool_w,
               pool_scale=pool_scale, hgrn_norm_g=hgrn_norm_g, diff_lam_q1=diff_lam_q1,
               diff_lam_k1=diff_lam_k1, diff_lam_q2=diff_lam_q2, diff_lam_k2=diff_lam_k2,
               diff_subln_g=diff_subln_g, sgu_ln_g=sgu_ln_g, sgu_ln_b=sgu_ln_b, sgu_w=sgu_w, sgu_b=sgu_b)
    depth = w_in.shape[0]
    nb, t, d = x_prompt.shape
    ns = x_sample.shape[0]
    n_pages = page_table.shape[1]
    past_len = n_pages * PAGE
    alpha = (2.0 * depth) ** 0.25
    w = W_GROUP
    tq, tk = _attn_tiles(t)
    assert x_sample.shape[1] == 1 and t % SG_CHUNK == 0 and t % tq == 0 and tk >= REL_MAX_DIST

    lb_cum = jnp.cumsum(jax.nn.softmax(hgrn_lb.astype(F32), axis=0), axis=0)
    lb_all = jnp.maximum(lb_cum - lb_cum[:1], 0.0)

    cache_kt = jnp.transpose(cache_k, (0, 1, 3, 4, 2))
    cache_vt = jnp.transpose(cache_v, (0, 1, 3, 4, 2))
    hgrn_t = jnp.transpose(state_hgrn, (0, 2, 3, 4, 1))
    pool_t = jnp.transpose(state_pool, (0, 2, 1, 3))

    wb = {k: prm[k].astype(BF16) for k in ("ffn1_w_gu", "ffn1_w_dn", "ffn2_w_gu", "ffn2_w_dn", "w_in", "w_out")}
    w_int = jnp.swapaxes(prm["w_in"], 1, 2).astype(BF16)
    w_qkvt = w_int[:, 5 * w:8 * w]

    xp = x_prompt.reshape(nb * t, d)
    xs = x_sample.reshape(ns, d)
    outs = {k: [] for k in ("pool_p", "pool_s", "hgrn_p", "hgrn_s", "k_p", "k_s", "v_p", "v_s", "sgv_p", "sgv_s")}
    for l in range(depth):
        lp = _layer_params(l, depth, tq, tk, past_len, n_pages, prm, lb_all)
        g = [lp["ln_g"][i][None] for i in range(3)]
        b = [lp["ln_b"][i][None] for i in range(3)]

        xp, p_a, p_d, k_rows, q_t, k_t, v_t = _ffn_inproj_prompt(
            xp, wb["ffn1_w_gu"], wb["ffn1_w_dn"], g[0], b[0], wb["w_in"], w_qkvt, l, nb, alpha)
        y_abd, pool16, st, sgv = _mix_prompt(p_a, p_d, nb, lp)
        y_ct = _attn_prompt(q_t, k_rows, v_t, lp, nb)
        xp = _outproj_ffn_prompt(xp, y_abd, y_ct, wb["w_out"], wb["ffn2_w_gu"], wb["ffn2_w_dn"], l,
                                 g[1], b[1], g[2], b[2], alpha)
        outs["k_p"].append(jnp.transpose(k_t.reshape(nb, N_HEADS, D_HEAD, t), (0, 3, 1, 2)))
        outs["v_p"].append(jnp.transpose(v_t.reshape(nb, N_HEADS, D_HEAD, t), (0, 3, 1, 2)))
        outs["pool_p"].append(pool16[:, 1:])
        outs["hgrn_p"].append(jnp.stack(
            [jnp.swapaxes(st[:, h * D_HEAD:(h + 1) * D_HEAD, h * D_HEAD:(h + 1) * D_HEAD], 1, 2)
             for h in range(N_HEADS)], axis=1))
        outs["sgv_p"].append(sgv)

        xs = _ffn(xs, wb["ffn1_w_gu"], wb["ffn1_w_dn"], l, g[0], b[0], alpha)
        ps, pst = _inproj_sample(xs, wb["w_in"], w_int, l)
        y_a, y_d, new_pool, vn = _mix_sample(ps, pool_t[l], lp, past_len)
        new_state, y_bt = _hgrn_sample(pst, hgrn_t[l], lp)
        y_ct = _attn_sample(pst, cache_kt, cache_vt, page_table, l, lp)
        xs = _outproj_sample(xs, y_a, y_bt, y_ct, y_d, wb["w_out"], l, g[1], b[1], alpha)
        xs = _ffn(xs, wb["ffn2_w_gu"], wb["ffn2_w_dn"], l, g[2], b[2], alpha)
        outs["k_s"].append(jnp.transpose(pst[6 * w:7 * w].reshape(N_HEADS, D_HEAD, ns), (2, 0, 1))[:, None])
        outs["v_s"].append(jnp.transpose(pst[7 * w:8 * w].reshape(N_HEADS, D_HEAD, ns), (2, 0, 1))[:, None])
        outs["pool_s"].append(jnp.transpose(new_pool, (1, 0, 2)))
        outs["hgrn_s"].append(jnp.transpose(new_state, (3, 0, 1, 2)))
        outs["sgv_s"].append(vn[:, None])

    st = {k: jnp.stack(v, axis=0) for k, v in outs.items()}
    return (xp.reshape(nb, t, d), xs.reshape(ns, 1, d), st["pool_p"], st["pool_s"], st["hgrn_p"], st["hgrn_s"],
            st["k_p"], st["k_s"], st["v_p"], st["v_s"], st["sgv_p"], st["sgv_s"])
```

```python
import functools
import math

import jax
import jax.numpy as jnp
import jax.scipy.linalg
from jax import lax
from jax.experimental import pallas as pl
from jax.experimental.pallas import tpu as pltpu

F32 = jnp.float32
BF16 = jnp.bfloat16

N_MIX = 4
W_GROUP = 256
N_HEADS = 4
D_HEAD = 64
DA_HALF = 32
POOL_WINDOWS = (2, 4, 8, 16)
POOL_BUF = 15
PAGE = 128
SG_CHUNK = 128
HG_SUB = 16
V_AUG = 80
REL_BUCKETS = 32
REL_MAX_DIST = 128
EPS = 1e-5
NEG = -1e30
LOG2E = 1.0 / math.log(2.0)

VMEM_LIMIT = 56 * 1024 * 1024
MXU_DIM = 256
ROW_TILE = 512
FFN_CHUNKS = 2
ATT_TQ = 1024
ATT_TK = 512
DEC_SAMPLES_PER_STEP = 2
DEC_SLOTS = 3

NT_DIMS = (((1,), (1,)), ((), ()))


def _params(*sem):
    return pltpu.CompilerParams(dimension_semantics=sem, vmem_limit_bytes=VMEM_LIMIT)


def _const_spec(shape, single=False):
    nd = len(shape)
    kw = {"pipeline_mode": pl.Buffered(1)} if single else {}
    return pl.BlockSpec(shape, lambda *_: (0,) * nd, **kw)


def _layer_spec(shape, layer, single=False, rows=None, row_block=0):
    kw = {"pipeline_mode": pl.Buffered(1)} if single else {}
    block = (None, shape[1] if rows is None else rows) + tuple(shape[2:])
    return pl.BlockSpec(block, lambda *_: (layer, row_block) + (0,) * (len(shape) - 2), **kw)


def _layer_norm(y, g, b):
    mu = jnp.mean(y, axis=-1, keepdims=True)
    yc = y - mu
    var = jnp.mean(yc * yc, axis=-1, keepdims=True)
    return yc * lax.rsqrt(var + EPS) * g + b


def _silu(x):
    return x * jax.nn.sigmoid(x)


def _log_forget(z, log_lb, log1m_lb):
    log_sig = jnp.minimum(z, 0.0) - jnp.log(1.0 + jnp.exp(-jnp.abs(z)))
    b = log1m_lb + log_sig
    return jnp.maximum(log_lb, b) + jnp.log(1.0 + jnp.exp(-jnp.abs(log_lb - b)))


def _swiglu_ln(x, wgu_ref, wdn_ref, g, b, d_ff, bounds, alpha):
    xb = x.astype(BF16)
    acc = None
    for lo, hi in zip(bounds[:-1], bounds[1:]):
        gate = jnp.dot(xb, wgu_ref[:, lo:hi], preferred_element_type=F32)
        up = jnp.dot(xb, wgu_ref[:, d_ff + lo:d_ff + hi], preferred_element_type=F32)
        h = (_silu(gate) * up).astype(BF16)
        part = jnp.dot(h, wdn_ref[lo:hi, :], preferred_element_type=F32)
        acc = part if acc is None else acc + part
    return _layer_norm(alpha * x + 0.5 * acc, g, b)


def _ffn_kernel(x_ref, wgu_ref, wdn_ref, g_ref, b_ref, o_ref, *, d_ff, bounds, alpha):
    o_ref[...] = _swiglu_ln(x_ref[...], wgu_ref, wdn_ref, g_ref[...], b_ref[...], d_ff, bounds, alpha)


def _ffn_bounds(d_ff):
    n_tiles = d_ff // MXU_DIM
    cuts = [round(n_tiles * c / FFN_CHUNKS) * MXU_DIM for c in range(FFN_CHUNKS)]
    return tuple(cuts) + (d_ff,)


def _ffn(x, w_gu, w_dn, layer, g, b, alpha):
    r, d = x.shape
    d_ff = w_dn.shape[1]
    tm = min(ROW_TILE, r)
    bounds = _ffn_bounds(d_ff)
    return pl.pallas_call(
        functools.partial(_ffn_kernel, d_ff=d_ff, bounds=bounds, alpha=alpha),
        grid=(r // tm,),
        in_specs=[
            pl.BlockSpec((tm, d), lambda i: (i, 0)),
            _layer_spec(w_gu.shape, layer, single=True),
            _layer_spec(w_dn.shape, layer, single=True),
            _const_spec((1, d)),
            _const_spec((1, d)),
        ],
        out_specs=pl.BlockSpec((tm, d), lambda i: (i, 0)),
        out_shape=jax.ShapeDtypeStruct((r, d), F32),
        compiler_params=_params("arbitrary"),
        name="ffn",
    )(x, w_gu, w_dn, g, b)


def _ffn_inproj_p_kernel(x_ref, wgu_ref, wdn_ref, g_ref, b_ref, w_ref, wqkvt_ref,
                         x1_ref, pa_ref, pd_ref, k_ref, qt_ref, kt_ref, vt_ref, *, d_ff, bounds, alpha, q_scale):
    w = W_GROUP
    x1 = _swiglu_ln(x_ref[...], wgu_ref, wdn_ref, g_ref[...], b_ref[...], d_ff, bounds, alpha)
    x1_ref[...] = x1
    xb = x1.astype(BF16)
    pa_ref[...] = jnp.dot(xb, w_ref[:, 0:5 * w], preferred_element_type=F32)
    pd_ref[...] = jnp.dot(xb, w_ref[:, 8 * w:10 * w], preferred_element_type=F32)
    k = jnp.dot(xb, w_ref[:, 6 * w:7 * w], preferred_element_type=F32)
    for h in range(N_HEADS):
        k_ref[h] = k[:, h * D_HEAD:(h + 1) * D_HEAD].astype(BF16)
    qkvt = lax.dot_general(wqkvt_ref[...], xb, NT_DIMS, preferred_element_type=F32)
    qt_ref[0] = (qkvt[0:w] * q_scale).astype(BF16)
    kt_ref[0] = qkvt[w:2 * w]
    vt_ref[0] = qkvt[2 * w:3 * w]


def _ffn_inproj_prompt(x, w_gu, w_dn, g, b, w_in, w_qkvt, layer, n_batch, alpha):
    r, d = x.shape
    t = r // n_batch
    w = W_GROUP
    tm = min(ROW_TILE, t)
    tpb = t // tm
    d_ff = w_dn.shape[1]
    chan_major = lambda: pl.BlockSpec((1, w, tm), lambda i: (i // tpb, 0, i % tpb))
    return pl.pallas_call(
        functools.partial(_ffn_inproj_p_kernel, d_ff=d_ff, bounds=_ffn_bounds(d_ff), alpha=alpha,
                          q_scale=DA_HALF ** -0.5 * LOG2E),
        grid=(r // tm,),
        in_specs=[
            pl.BlockSpec((tm, d), lambda i: (i, 0)),
            _layer_spec(w_gu.shape, layer, single=True),
            _layer_spec(w_dn.shape, layer, single=True),
            _const_spec((1, d)), _const_spec((1, d)),
            _layer_spec(w_in.shape, layer, single=True),
            _layer_spec(w_qkvt.shape, layer, single=True),
        ],
        out_specs=[
            pl.BlockSpec((tm, d), lambda i: (i, 0)),
            pl.BlockSpec((tm, 5 * w), lambda i: (i, 0)),
            pl.BlockSpec((tm, 2 * w), lambda i: (i, 0)),
            pl.BlockSpec((N_HEADS, tm, D_HEAD), lambda i: (0, i, 0)),
            chan_major(), chan_major(), chan_major(),
        ],
        out_shape=[
            jax.ShapeDtypeStruct((r, d), F32),
            jax.ShapeDtypeStruct((r, 5 * w), F32),
            jax.ShapeDtypeStruct((r, 2 * w), F32),
            jax.ShapeDtypeStruct((N_HEADS, r, D_HEAD), BF16),
            jax.ShapeDtypeStruct((n_batch, w, t), BF16),
            jax.ShapeDtypeStruct((n_batch, w, t), F32),
            jax.ShapeDtypeStruct((n_batch, w, t), F32),
        ],
        compiler_params=_params("arbitrary"),
        name="ffn_inproj_prompt",
    )(x, w_gu, w_dn, g, b, w_in, w_qkvt)


def _inproj_s_kernel(x_ref, w_ref, wt_ref, p_ref, pt_ref):
    xb = x_ref[...].astype(BF16)
    p_ref[...] = jnp.dot(xb, w_ref[...], preferred_element_type=F32)
    pt_ref[...] = lax.dot_general(wt_ref[...], xb, NT_DIMS, preferred_element_type=F32)


def _inproj_sample(x, w_in, w_int, layer):
    n, d = x.shape
    d_in = w_in.shape[2]
    return pl.pallas_call(
        _inproj_s_kernel,
        grid=(1,),
        in_specs=[_const_spec((n, d)), _layer_spec(w_in.shape, layer), _layer_spec(w_int.shape, layer)],
        out_specs=[_const_spec((n, d_in)), _const_spec((d_in, n))],
        out_shape=[jax.ShapeDtypeStruct((n, d_in), F32), jax.ShapeDtypeStruct((d_in, n), F32)],
        compiler_params=_params("arbitrary"),
        name="inproj_sample",
    )(x, w_in, w_int)


def _group_select(grp, parts):
    out = parts[N_HEADS - 1]
    for g in range(N_HEADS - 2, -1, -1):
        out = jnp.where(grp == g, parts[g], out)
    return out


def _mix_p_kernel(pa_ref, pd_ref, poolw_ref, pscale_ref, loglb_ref, log1mlb_ref, omlb_ref, hgg_ref,
                  sgg_ref, sgb_ref, sgw_ref, sgbias_ref, eones_ref,
                  y_ref, pool_ref, st_out_ref, sgv_ref,
                  prev_ref, st_ref):
    c = SG_CHUNK
    w = W_GROUP
    t = pl.program_id(1)

    @pl.when(t == 0)
    def _():
        prev_ref[...] = jnp.zeros_like(prev_ref)
        st_ref[...] = jnp.zeros_like(st_ref)

    a = pa_ref[:, 0:w]
    hq = pa_ref[:, w:2 * w]
    hf = pa_ref[:, 2 * w:3 * w]
    hi = pa_ref[:, 3 * w:4 * w]
    hg = pa_ref[:, 4 * w:5 * w]
    su = pd_ref[:, 0:w]
    sv = pd_ref[:, w:2 * w]
    lane = lax.broadcasted_iota(jnp.int32, (c, w), 1)
    row = lax.broadcasted_iota(jnp.int32, (c, w), 0)
    grp = lane >> 6

    e = jnp.concatenate([prev_ref[...], a], axis=0)
    s2 = e + pltpu.roll(e, 1, 0)
    s4 = s2 + pltpu.roll(s2, 2, 0)
    s8 = s4 + pltpu.roll(s4, 4, 0)
    s16 = s8 + pltpu.roll(s8, 8, 0)
    wsum = _group_select(grp, [s2[16:], s4[16:], s8[16:], s16[16:]])
    win = _group_select(grp, [jnp.full((c, w), v, jnp.int32) for v in POOL_WINDOWS])
    cnt = jnp.minimum(t * c + row + 1, win).astype(F32)
    dpool = wsum / cnt - a
    ya = jnp.dot(dpool.astype(BF16), poolw_ref[...], preferred_element_type=F32) * pscale_ref[...]
    prev_ref[...] = a[c - 16:]
    pool_ref[0] = a[c - 16:]

    vn = _layer_norm(sv, sgg_ref[...], sgb_ref[...])
    sg = jnp.dot(sgw_ref[...], vn.astype(BF16), preferred_element_type=F32)
    s_gate = _group_select(grp, [sg[g * c:(g + 1) * c] for g in range(N_HEADS)]) + sgbias_ref[...]
    yd = su * s_gate
    sgv_ref[0] = vn

    n_sub = c // HG_SUB
    logf = _log_forget(hf, loglb_ref[...], log1mlb_ref[...])
    kin = omlb_ref[...] * jax.nn.sigmoid(-hf)
    r16 = row & (HG_SUB - 1)
    bl = logf
    rv = logf
    for sh in (1, 2, 4, 8):
        bl = bl + jnp.where(r16 >= sh, pltpu.roll(bl, sh, 0), 0.0)
        rv = rv + jnp.where(r16 + sh < HG_SUB, pltpu.roll(rv, c - sh, 0), 0.0)
    sub_row = row >> 4
    qtb = (hq * jnp.exp(bl)).astype(BF16)
    kt = kin * jnp.exp(rv - logf)
    dec = jnp.exp(rv)
    vtb = hi.T.astype(BF16)
    bi0 = lax.broadcasted_iota(jnp.int32, (w, w), 0) >> 6
    bi1 = lax.broadcasted_iota(jnp.int32, (w, w), 1) >> 6
    blockmask = bi0 == bi1
    st = st_ref[...]
    seen = []
    for i in range(n_sub):
        seen.append(st.astype(BF16))
        km = jnp.where(sub_row == i, kt, 0.0).astype(BF16)
        u = jnp.dot(vtb, km, preferred_element_type=F32)
        st = dec[i * HG_SUB:i * HG_SUB + 1] * st + jnp.where(blockmask, u, 0.0)
    st_ref[...] = st
    st_out_ref[0] = st
    qx = jnp.concatenate([jnp.where(sub_row == i, qtb, jnp.zeros_like(qtb)) for i in range(n_sub)], axis=1)
    o = lax.dot_general(qx, jnp.concatenate(seen, axis=1), NT_DIMS,
                        preferred_element_type=F32)

    rt = {lo: lax.broadcasted_iota(jnp.int32, (HG_SUB - lo, w), 0) + lo for lo in (0, 8)}
    bl2 = bl * LOG2E
    xs = []
    for i in range(n_sub):
        rows = slice(i * HG_SUB, (i + 1) * HG_SUB)
        bli, qi, ki = bl2[rows], hq[rows], kin[rows]
        for s in range(HG_SUB):
            lo = (s // 8) * 8
            ratio = jnp.exp2(jnp.where(rt[lo] >= s, bli[lo:] - bli[s:s + 1], NEG))
            x_s = ratio * qi[lo:] * ki[s:s + 1]
            if lo:
                x_s = jnp.concatenate([jnp.zeros((lo, w), F32), x_s], axis=0)
            xs.append(x_s.astype(BF16))
    x = jnp.concatenate(xs, axis=0)
    r = jnp.dot(x, eones_ref[...], preferred_element_type=F32)
    o_intra = []
    for i in range(n_sub):
        vi = hi[i * HG_SUB:(i + 1) * HG_SUB]
        base = i * HG_SUB * HG_SUB
        oi = r[base:base + HG_SUB] * vi[0:1]
        for s in range(1, HG_SUB):
            oi = oi + r[base + s * HG_SUB:base + (s + 1) * HG_SUB] * vi[s:s + 1]
        o_intra.append(oi)
    o = o + jnp.concatenate(o_intra, axis=0)
    ms = [jnp.mean(jnp.square(o[:, g * D_HEAD:(g + 1) * D_HEAD]), axis=-1, keepdims=True)
          for g in range(N_HEADS)]
    yb = o * lax.rsqrt(_group_select(grp, ms) + EPS) * hgg_ref[...] * _silu(hg)

    y_ref[:, 0:w] = ya.astype(BF16)
    y_ref[:, w:2 * w] = yb.astype(BF16)
    y_ref[:, 2 * w:3 * w] = yd.astype(BF16)


def _mix_prompt(p_a, p_d, n_batch, lp):
    r = p_a.shape[0]
    t = r // n_batch
    c = SG_CHUNK
    nt = t // c
    w = W_GROUP
    row = lambda: _const_spec((1, w))
    return pl.pallas_call(
        _mix_p_kernel,
        grid=(n_batch, nt),
        in_specs=[
            pl.BlockSpec((c, 5 * w), lambda b, i: (b * nt + i, 0)),
            pl.BlockSpec((c, 2 * w), lambda b, i: (b * nt + i, 0)),
            _const_spec((w, w)), row(), row(), row(), row(), row(), row(), row(),
            _const_spec((N_HEADS * c, c)), _const_spec((c, w)), _const_spec((w, w)),
        ],
        out_specs=[
            pl.BlockSpec((c, 3 * w), lambda b, i: (b * nt + i, 0)),
            pl.BlockSpec((1, 16, w), lambda b, i: (b, 0, 0)),
            pl.BlockSpec((1, w, w), lambda b, i: (b, 0, 0)),
            pl.BlockSpec((1, c, w), lambda b, i: (b, 0, 0)),
        ],
        out_shape=[
            jax.ShapeDtypeStruct((r, 3 * w), BF16),
            jax.ShapeDtypeStruct((n_batch, 16, w), F32),
            jax.ShapeDtypeStruct((n_batch, w, w), F32),
            jax.ShapeDtypeStruct((n_batch, c, w), F32),
        ],
        scratch_shapes=[pltpu.VMEM((16, w), F32), pltpu.VMEM((w, w), F32)],
        compiler_params=_params("arbitrary", "arbitrary"),
        name="mix_prompt",
    )(p_a, p_d, lp["pool_w"], lp["pool_scale"], lp["log_lb"], lp["log1m_lb"], lp["om_lb"], lp["hg_g"],
      lp["sg_g"], lp["sg_b"], lp["sg_w"], lp["sg_bias"], lp["eones"])


def _attn_p_kernel(cfar_ref, lam_ref, qt_ref, k_ref, vt_ref, near_ref, subg_ref, o_ref,
                   va_ref, m_ref, acc_ref, s_ref, mb_ref, *, tq, tk):
    h = pl.program_id(1)
    qi = pl.program_id(2)
    dh = D_HEAD

    @pl.when(qi == 0)
    def _():
        va_ref[0:dh, :] = vt_ref[0].astype(BF16)
        rows = lax.broadcasted_iota(jnp.int32, (V_AUG - dh, va_ref.shape[1]), 0)
        va_ref[dh:V_AUG, :] = jnp.where(rows == 0, 1.0, 0.0).astype(BF16)

    qt = qt_ref[0]
    sub = lax.broadcasted_iota(jnp.int32, qt.shape, 0)
    zero = jnp.zeros_like(qt)
    qm = (jnp.where(sub < DA_HALF, qt, zero), jnp.where(sub >= DA_HALF, qt, zero))
    m_ref[...] = jnp.full_like(m_ref, NEG)
    acc_ref[...] = jnp.zeros_like(acc_ref)

    def qk(ki, slot, near=None, q0=0):
        k0 = pl.multiple_of(ki * tk, tk)
        kblk = k_ref[0, pl.ds(k0, tk), :]
        for mi in range(2):
            s = jnp.dot(kblk, qm[mi][:, q0:], preferred_element_type=F32)
            if near is not None:
                s = s + near_ref[0, near, :, q0:]
            s_ref[slot, mi, :, q0:] = s
            mb_ref[slot, mi, :, q0:] = jnp.max(s, axis=0, keepdims=True)

    def softmax_pv(ki, slot, shift, q0=0):
        k0 = pl.multiple_of(ki * tk, tk)
        vblk = va_ref[:, pl.ds(k0, tk)]
        for mi in range(2):
            s = s_ref[slot, mi, :, q0:]
            m_blk = mb_ref[slot, mi, :, q0:]
            if shift is not None:
                m_blk = m_blk + shift
            m_old = m_ref[mi, :, q0:]
            m_new = jnp.maximum(m_old, m_blk)
            alpha = jnp.exp2(m_old - m_new)
            p = jnp.exp2(s - (m_new if shift is None else m_new - shift)).astype(BF16)
            acc_ref[mi, :, q0:] = alpha * acc_ref[mi, :, q0:] + jnp.dot(vblk, p, preferred_element_type=F32)
            m_ref[mi, :, q0:] = m_new

    cfar = cfar_ref[h]
    n_diag = tq // tk
    kb0 = qi * n_diag

    def near_tiles(first_tile, first_near, n, slot0):
        q0s = [max(first_near + j - 1, 0) * tk for j in range(n)]
        for j in range(n):
            if j + 1 < n:
                qk(first_tile + j + 1, (slot0 + j + 1) % 2, first_near + j + 1, q0s[j + 1])
            softmax_pv(first_tile + j, (slot0 + j) % 2, None, q0s[j])

    def far_pair(pi, carry):
        b = 2 * pi
        qk(b + 1, 1)
        softmax_pv(b, 0, cfar)
        qk(b + 2, 0)
        softmax_pv(b + 1, 1, cfar)
        return carry

    @pl.when(qi == 0)
    def _():
        qk(0, 0, 1)
        near_tiles(0, 1, n_diag, 0)

    @pl.when(qi >= 1)
    def _():
        n_far = kb0 - 1
        qk(0, 0)
        lax.fori_loop(0, (n_far - 1) // 2, far_pair, 0)

        def tail(rem):
            if rem == 2:
                qk(n_far - 1, 1)
                softmax_pv(n_far - 2, 0, cfar)
            last_slot = rem % 2
            qk(n_far, last_slot, 0)
            softmax_pv(n_far - 1, 1 - last_slot, cfar)
            near_tiles(n_far, 0, n_diag + 1, last_slot)

        if n_diag % 2 == 0:
            tail(1)
        else:
            pl.when(n_far % 2 == 1)(lambda: tail(1))
            pl.when(n_far % 2 == 0)(lambda: tail(2))

    a0 = acc_ref[0]
    a1 = acc_ref[1]
    o = a0[0:dh] / a0[dh:dh + 1] - lam_ref[0] * (a1[0:dh] / a1[dh:dh + 1])
    ms = jnp.mean(o * o, axis=0, keepdims=True)
    o_ref[0] = (o * lax.rsqrt(ms + EPS) * subg_ref[...]).astype(BF16)


def _attn_tiles(t):
    tq = min(ATT_TQ, t)
    return tq, min(ATT_TK, tq // 2)


def _attn_prompt(qt, k_rows, vt, lp, n_batch):
    t = qt.shape[2]
    tq, tk = _attn_tiles(t)
    nq = t // tq
    n_near = tq // tk + 1
    smem = pl.BlockSpec(memory_space=pltpu.SMEM)
    return pl.pallas_call(
        functools.partial(_attn_p_kernel, tq=tq, tk=tk),
        grid=(n_batch, N_HEADS, nq),
        in_specs=[
            smem, smem,
            pl.BlockSpec((1, D_HEAD, tq), lambda b, h, i: (b, h, i)),
            pl.BlockSpec((1, t, D_HEAD), lambda b, h, i: (h, b, 0)),
            pl.BlockSpec((1, D_HEAD, t), lambda b, h, i: (b, h, 0)),
            pl.BlockSpec((1, n_near, tk, tq), lambda b, h, i: (h, 0, 0, 0)),
            _const_spec((D_HEAD, 1)),
        ],
        out_specs=pl.BlockSpec((1, D_HEAD, tq), lambda b, h, i: (b, h, i)),
        out_shape=jax.ShapeDtypeStruct((n_batch, W_GROUP, t), BF16),
        scratch_shapes=[
            pltpu.VMEM((V_AUG, t), BF16),
            pltpu.VMEM((2, 1, tq), F32), pltpu.VMEM((2, V_AUG, tq), F32),
            pltpu.VMEM((2, 2, tk, tq), F32), pltpu.VMEM((2, 2, 1, tq), F32),
        ],
        compiler_params=_params("arbitrary", "arbitrary", "arbitrary"),
        name="attn_prompt",
    )(lp["c_far"], lp["lam"], qt, k_rows, vt, lp["bias_near"], lp["sub_g_col"])


def _outproj_ffn_p_kernel(x_ref, y_ref, yct_ref, wab_ref, wc_ref, wd_ref, g1_ref, b1_ref,
                          wgu_ref, wdn_ref, g2_ref, b2_ref, o_ref, *, d_ff, bounds, alpha):
    w = W_GROUP
    acc = jnp.dot(y_ref[:, 0:2 * w], wab_ref[...], preferred_element_type=F32)
    acc = acc + jnp.dot(y_ref[:, 2 * w:3 * w], wd_ref[...], preferred_element_type=F32)
    yc = yct_ref[0].astype(F32).T.astype(BF16)
    acc = acc + jnp.dot(yc, wc_ref[...], preferred_element_type=F32)
    x2 = _layer_norm(alpha * x_ref[...] + acc, g1_ref[...], b1_ref[...])
    o_ref[...] = _swiglu_ln(x2, wgu_ref, wdn_ref, g2_ref[...], b2_ref[...], d_ff, bounds, alpha)


def _outproj_ffn_prompt(x, y_abd, y_ct, w_out, w_gu, w_dn, layer, g1, b1, g2, b2, alpha):
    r, d = x.shape
    w = W_GROUP
    t = y_ct.shape[2]
    tm = min(ROW_TILE, t)
    tpb = t // tm
    d_ff = w_dn.shape[1]
    return pl.pallas_call(
        functools.partial(_outproj_ffn_p_kernel, d_ff=d_ff, bounds=_ffn_bounds(d_ff), alpha=alpha),
        grid=(r // tm,),
        in_specs=[
            pl.BlockSpec((tm, d), lambda i: (i, 0)),
            pl.BlockSpec((tm, 3 * W_GROUP), lambda i: (i, 0)),
            pl.BlockSpec((1, W_GROUP, tm), lambda i: (i // tpb, 0, i % tpb)),
            _layer_spec(w_out.shape, layer, rows=2 * w, row_block=0),
            _layer_spec(w_out.shape, layer, rows=w, row_block=2),
            _layer_spec(w_out.shape, layer, rows=w, row_block=3),
            _const_spec((1, d)), _const_spec((1, d)),
            _layer_spec(w_gu.shape, layer, single=True),
            _layer_spec(w_dn.shape, layer, single=True),
            _const_spec((1, d)), _const_spec((1, d)),
        ],
        out_specs=pl.BlockSpec((tm, d), lambda i: (i, 0)),
        out_shape=jax.ShapeDtypeStruct((r, d), F32),
        compiler_params=_params("arbitrary"),
        name="outproj_ffn_prompt",
    )(x, y_abd, y_ct, w_out, w_out, w_out, g1, b1, w_gu, w_dn, g2, b2)


def _outproj_s_kernel(x_ref, ya_ref, ybt_ref, yct_ref, yd_ref, w_ref, g_ref, b_ref, o_ref, *, alpha):
    w = W_GROUP
    parts = (ya_ref[...], ybt_ref[...].T, yct_ref[...].T, yd_ref[...])
    acc = None
    for j, part in enumerate(parts):
        d = jnp.dot(part.astype(BF16), w_ref[j * w:(j + 1) * w, :], preferred_element_type=F32)
        acc = d if acc is None else acc + d
    o_ref[...] = _layer_norm(alpha * x_ref[...] + acc, g_ref[...], b_ref[...])


def _outproj_sample(x, y_a, y_bt, y_ct, y_d, w_out, layer, g, b, alpha):
    n, d = x.shape
    w = W_GROUP
    return pl.pallas_call(
        functools.partial(_outproj_s_kernel, alpha=alpha),
        grid=(1,),
        in_specs=[
            _const_spec((n, d)), _const_spec((n, w)), _const_spec((w, n)), _const_spec((w, n)),
            _const_spec((n, w)), _layer_spec(w_out.shape, layer), _const_spec((1, d)), _const_spec((1, d)),
        ],
        out_specs=_const_spec((n, d)),
        out_shape=jax.ShapeDtypeStruct((n, d), F32),
        compiler_params=_params("arbitrary"),
        name="outproj_sample",
    )(x, y_a, y_bt, y_ct, y_d, w_out, g, b)


def _mix_s_kernel(pa_ref, pd_ref, pool_ref, poolw_ref, pscale_ref, sgg_ref, sgb_ref, w00_ref, b0_ref,
                  ya_ref, yd_ref, newpool_ref, vn_ref, *, cnts):
    a = pa_ref[...]
    n, w = a.shape
    grp = lax.broadcasted_iota(jnp.int32, (n, w), 1) >> 6
    acc = a
    means = []
    for j in range(1, POOL_BUF + 1):
        acc = acc + pool_ref[POOL_BUF - j]
        if j + 1 in POOL_WINDOWS:
            means.append(acc / cnts[POOL_WINDOWS.index(j + 1)])
    dpool = _group_select(grp, means) - a
    ya_ref[...] = jnp.dot(dpool.astype(BF16), poolw_ref[...], preferred_element_type=F32) * pscale_ref[...]
    for j in range(POOL_BUF - 1):
        newpool_ref[j] = pool_ref[j + 1]
    newpool_ref[POOL_BUF - 1] = a

    vn = _layer_norm(pd_ref[:, w:2 * w], sgg_ref[...], sgb_ref[...])
    vn_ref[...] = vn
    yd_ref[...] = pd_ref[:, 0:w] * (w00_ref[...] * vn + b0_ref[...])


def _mix_sample(p, pool, lp, past_len):
    n = p.shape[0]
    w = W_GROUP
    cnts = tuple(float(min(past_len + 1, win)) for win in POOL_WINDOWS)
    row = lambda: _const_spec((1, w))
    return pl.pallas_call(
        functools.partial(_mix_s_kernel, cnts=cnts),
        grid=(1,),
        in_specs=[
            pl.BlockSpec((n, w), lambda i: (0, 0)),
            pl.BlockSpec((n, 2 * w), lambda i: (0, 4)),
            _const_spec(pool.shape), _const_spec((w, w)), row(), row(), row(), row(), row(),
        ],
        out_specs=[_const_spec((n, w)), _const_spec((n, w)), _const_spec(pool.shape), _const_spec((n, w))],
        out_shape=[
            jax.ShapeDtypeStruct((n, w), F32), jax.ShapeDtypeStruct((n, w), F32),
            jax.ShapeDtypeStruct(pool.shape, F32), jax.ShapeDtypeStruct((n, w), F32),
        ],
        compiler_params=_params("arbitrary"),
        name="mix_sample",
    )(p, p, pool, lp["pool_w"], lp["pool_scale"], lp["sg_g"], lp["sg_b"], lp["sg_w00"], lp["sg_b0"])


def _hgrn_s_kernel(q_ref, f_ref, i_ref, g_ref, s_ref, loglb_ref, log1mlb_ref, omlb_ref, hgg_ref,
                   so_ref, yb_ref):
    z = f_ref[...]
    f = jnp.exp(_log_forget(z, loglb_ref[...], log1mlb_ref[...]))
    kin = omlb_ref[...] * jax.nn.sigmoid(-z)
    q = q_ref[...]
    v = i_ref[...]
    o = jnp.zeros_like(v)
    for d in range(D_HEAD):
        sn = f[d:d + 1] * s_ref[0, d] + kin[d:d + 1] * v
        so_ref[0, d] = sn
        o = o + q[d:d + 1] * sn
    ms = jnp.mean(o * o, axis=0, keepdims=True)
    yb_ref[...] = o * lax.rsqrt(ms + EPS) * hgg_ref[...] * _silu(g_ref[...])


def _hgrn_sample(pt, state, lp):
    n = pt.shape[1]
    dh = D_HEAD
    blk = lambda off: pl.BlockSpec((dh, n), lambda h: (off * N_HEADS + h, 0))
    col = lambda: pl.BlockSpec((dh, 1), lambda h: (h, 0))
    return pl.pallas_call(
        _hgrn_s_kernel,
        grid=(N_HEADS,),
        in_specs=[
            blk(1), blk(2), blk(3), blk(4),
            pl.BlockSpec((1, dh, dh, n), lambda h: (h, 0, 0, 0)),
            col(), col(), col(), _const_spec((dh, 1)),
        ],
        out_specs=[
            pl.BlockSpec((1, dh, dh, n), lambda h: (h, 0, 0, 0)),
            pl.BlockSpec((dh, n), lambda h: (h, 0)),
        ],
        out_shape=[
            jax.ShapeDtypeStruct(state.shape, F32),
            jax.ShapeDtypeStruct((W_GROUP, n), F32),
        ],
        compiler_params=_params("arbitrary"),
        name="hgrn_sample",
    )(pt, pt, pt, pt, state, lp["log_lb_col"], lp["log1m_lb_col"], lp["om_lb_col"], lp["hg_g_col"])


def _attn_s_one(n, k_pages, v_pages, lam, qt_ref, knt_ref, vnt_ref, bias_ref, bself_ref, n_pages, q_scale):
    w, ns = qt_ref.shape
    dh = D_HEAD
    sel = lax.broadcasted_iota(jnp.int32, (w, ns), 1) == n

    def column(ref):
        return jnp.sum(jnp.where(sel, ref[...], 0.0), axis=1, keepdims=True)

    qcol = column(qt_ref) * q_scale
    kcol = column(knt_ref)
    vcol = column(vnt_ref)

    n_grp = 2 * N_HEADS
    rows = ([], [])
    self_rows = ([], [])
    for h in range(N_HEADS):
        qh = qcol[h * dh:(h + 1) * dh]
        for j in range(n_pages):
            prod = k_pages[j][h] * qh
            rows[0].append(jnp.sum(prod[0:DA_HALF], axis=0, keepdims=True))
            rows[1].append(jnp.sum(prod[DA_HALF:dh], axis=0, keepdims=True))
        self_prod = qh * kcol[h * dh:(h + 1) * dh]
        for mi in range(2):
            s_self = (jnp.sum(self_prod[mi * DA_HALF:(mi + 1) * DA_HALF], axis=0, keepdims=True)
                      + bself_ref[h])
            self_rows[mi].append(jnp.broadcast_to(s_self, (n_pages, 1)))
    s = jnp.concatenate(rows[0] + rows[1], axis=0) + bias_ref[...]
    s_self = jnp.concatenate(self_rows[0] + self_rows[1], axis=0)

    def per_group(col, reduce):
        parts = [jnp.broadcast_to(reduce(col[g * n_pages:(g + 1) * n_pages], axis=0, keepdims=True),
                                  (n_pages, 1)) for g in range(n_grp)]
        return jnp.concatenate(parts, axis=0)

    m = jnp.maximum(per_group(jnp.max(s, axis=1, keepdims=True), jnp.max), s_self)
    p = jnp.exp(s - m)
    p_self = jnp.exp(s_self - m)
    inv_l = 1.0 / (per_group(jnp.sum(p, axis=1, keepdims=True), jnp.sum) + p_self)
    half = N_HEADS * n_pages
    a = (p * inv_l)[0:half] - lam * (p * inv_l)[half:2 * half]
    a_self = (p_self * inv_l)[0:half] - lam * (p_self * inv_l)[half:2 * half]
    weighted = []
    for h in range(N_HEADS):
        oh = v_pages[0][h] * a[h * n_pages:h * n_pages + 1]
        for j in range(1, n_pages):
            oh = oh + v_pages[j][h] * a[h * n_pages + j:h * n_pages + j + 1]
        weighted.append(oh)
    a_self_col = jnp.concatenate(
        [jnp.broadcast_to(a_self[h * n_pages:h * n_pages + 1], (dh, 1)) for h in range(N_HEADS)], axis=0)
    ocol = jnp.sum(jnp.concatenate(weighted, axis=0), axis=1, keepdims=True) + a_self_col * vcol
    return sel, ocol


def _attn_s_kernel(pt_ref, lam_ref, qt_ref, knt_ref, vnt_ref, bias_ref, bself_ref, subg_ref, ck_ref, cv_ref,
                   o_ref, ot_ref, kbuf_ref, vbuf_ref, sem_ref, *, n_pages, sps, layer, q_scale):
    dh = D_HEAD
    step = pl.program_id(0)
    n_steps = pl.num_programs(0)

    def page_copies(s, slot):
        copies = []
        for u in range(sps):
            for j in range(n_pages):
                page = pt_ref[s * sps + u, j]
                copies.append(pltpu.make_async_copy(ck_ref.at[layer, page], kbuf_ref.at[slot, u * n_pages + j],
                                                    sem_ref.at[0, slot]))
                copies.append(pltpu.make_async_copy(cv_ref.at[layer, page], vbuf_ref.at[slot, u * n_pages + j],
                                                    sem_ref.at[1, slot]))
        return copies

    @pl.when(step == 0)
    def _():
        for s in range(DEC_SLOTS - 1):
            @pl.when(s < n_steps)
            def _():
                for c in page_copies(s, s):
                    c.start()

    slot = step % DEC_SLOTS
    for c in page_copies(step, slot):
        c.wait()

    ot = ot_ref[...]
    for u in range(sps):
        k_pages = [kbuf_ref.at[slot, u * n_pages + j] for j in range(n_pages)]
        v_pages = [vbuf_ref.at[slot, u * n_pages + j] for j in range(n_pages)]
        sel, ocol = _attn_s_one(step * sps + u, k_pages, v_pages, lam_ref[0],
                                qt_ref, knt_ref, vnt_ref, bias_ref, bself_ref, n_pages, q_scale)
        ot = jnp.where(sel, ocol, ot)
    ot_ref[...] = ot

    ahead = step + DEC_SLOTS - 1

    @pl.when(ahead < n_steps)
    def _():
        for c in page_copies(ahead, ahead % DEC_SLOTS):
            c.start()

    @pl.when(step == n_steps - 1)
    def _():
        for h in range(N_HEADS):
            oh = ot_ref[h * dh:(h + 1) * dh, :]
            ms = jnp.mean(oh * oh, axis=0, keepdims=True)
            o_ref[h * dh:(h + 1) * dh, :] = oh * lax.rsqrt(ms + EPS) * subg_ref[...]


def _attn_sample(pt, cache_kt, cache_vt, page_table, layer, lp):
    n = pt.shape[1]
    n_pages = page_table.shape[1]
    w = W_GROUP
    dh = D_HEAD
    blk = lambda off: pl.BlockSpec((w, n), lambda i, tbl: (off, 0))
    sps = DEC_SAMPLES_PER_STEP
    ring = (DEC_SLOTS, sps * n_pages, N_HEADS, dh, PAGE)
    grid_spec = pltpu.PrefetchScalarGridSpec(
        num_scalar_prefetch=1,
        grid=(n // sps,),
        in_specs=[
            pl.BlockSpec(memory_space=pltpu.SMEM),
            blk(5), blk(6), blk(7),
            pl.BlockSpec((2 * N_HEADS * n_pages, PAGE), lambda i, tbl: (0, 0)),
            pl.BlockSpec((N_HEADS, 1, 1), lambda i, tbl: (0, 0, 0)),
            pl.BlockSpec((dh, 1), lambda i, tbl: (0, 0)),
            pl.BlockSpec(memory_space=pl.ANY), pl.BlockSpec(memory_space=pl.ANY),
        ],
        out_specs=pl.BlockSpec((w, n), lambda i, tbl: (0, 0)),
        scratch_shapes=[
            pltpu.VMEM((w, n), F32), pltpu.VMEM(ring, F32), pltpu.VMEM(ring, F32),
            pltpu.SemaphoreType.DMA((2, DEC_SLOTS)),
        ],
    )
    return pl.pallas_call(
        functools.partial(_attn_s_kernel, n_pages=n_pages, sps=sps, layer=layer, q_scale=DA_HALF ** -0.5),
        grid_spec=grid_spec,
        out_shape=jax.ShapeDtypeStruct((w, n), F32),
        compiler_params=_params("arbitrary"),
        name="attn_sample",
    )(page_table, lp["lam"], pt, pt, pt, lp["bias_past"], lp["bias_self"], lp["sub_g_col"],
      cache_kt, cache_vt)


def _rel_bucket(dist):
    n = jnp.maximum(dist, 0)
    max_exact = REL_BUCKETS // 2
    large = max_exact + (jnp.log(jnp.maximum(n, 1).astype(F32) / max_exact)
                         / math.log(REL_MAX_DIST / max_exact) * (REL_BUCKETS - max_exact)).astype(jnp.int32)
    large = jnp.minimum(large, REL_BUCKETS - 1)
    return jnp.where(n < max_exact, n, large)


def _layer_params(l, depth, tq, tk, past_len, n_pages, prm, lb_all):
    w = W_GROUP
    lam_init = 0.8 - 0.6 * math.exp(-0.3 * l)
    lam = (jnp.exp(jnp.sum(prm["diff_lam_q1"][l] * prm["diff_lam_k1"][l]))
           - jnp.exp(jnp.sum(prm["diff_lam_q2"][l] * prm["diff_lam_k2"][l])) + lam_init)
    rel_bias = prm["rel_bias"]

    def bias_of(dist):
        onehot = _rel_bucket(dist)[..., None] == jnp.arange(REL_BUCKETS, dtype=jnp.int32)
        table = rel_bias.T.reshape((N_HEADS,) + (1,) * dist.ndim + (REL_BUCKETS,))
        return jnp.sum(jnp.where(onehot[None], table, 0.0), axis=-1)

    blk = REL_MAX_DIST
    k0 = jnp.arange(blk, dtype=jnp.int32)[:, None]
    q0 = jnp.arange(blk, dtype=jnp.int32)[None, :]
    blocks = {}

    def block(e):
        if e not in blocks:
            dist = blk * e + q0 - k0
            blocks[e] = jnp.where((dist >= 0)[None], bias_of(dist), NEG) * LOG2E
        return blocks[e]

    tiles = []
    for r in range(-1, tq // tk):
        rows = [jnp.concatenate([block(q1 - k1 - r * (tk // blk)) for q1 in range(tq // blk)], axis=-1)
                for k1 in range(tk // blk)]
        tiles.append(jnp.concatenate(rows, axis=-2))
    bias_near = jnp.stack(tiles, axis=1)
    kpos = jnp.arange(n_pages * PAGE, dtype=jnp.int32)
    bias_past = jnp.tile(bias_of(past_len - kpos).reshape(N_HEADS * n_pages, PAGE), (2, 1))
    lb = lb_all[l]
    tril = jnp.tril(jnp.ones((SG_CHUNK, SG_CHUNK), F32))
    sub_g = prm["diff_subln_g"][l] * (1.0 - lam_init)
    return {
        "ln_g": prm["ln_g"][l], "ln_b": prm["ln_b"][l],
        "pool_w": jax.scipy.linalg.block_diag(*prm["pool_w"][l]).astype(BF16),
        "pool_scale": prm["pool_scale"][l][None],
        "log_lb": jnp.log(lb)[None], "log1m_lb": jnp.log1p(-lb)[None], "om_lb": (1.0 - lb)[None],
        "log_lb_col": jnp.log(lb)[:, None], "log1m_lb_col": jnp.log1p(-lb)[:, None],
        "om_lb_col": (1.0 - lb)[:, None],
        "hg_g": jnp.tile(prm["hgrn_norm_g"][l], N_HEADS)[None], "hg_g_col": prm["hgrn_norm_g"][l][:, None],
        "sg_g": prm["sgu_ln_g"][l][None], "sg_b": prm["sgu_ln_b"][l][None],
        "sg_w": (prm["sgu_w"][l] * tril).reshape(N_HEADS * SG_CHUNK, SG_CHUNK).astype(BF16),
        "sg_bias": jnp.repeat(prm["sgu_b"][l].T, D_HEAD, axis=1),
        "sg_w00": jnp.repeat(prm["sgu_w"][l][:, 0, 0], D_HEAD)[None],
        "sg_b0": jnp.repeat(prm["sgu_b"][l][:, 0], D_HEAD)[None],
        "eones": jnp.kron(jnp.eye(N_HEADS, dtype=F32), jnp.ones((D_HEAD, D_HEAD), F32)).astype(BF16),
        "lam": lam.reshape(1), "c_far": rel_bias[REL_BUCKETS - 1] * LOG2E,
        "bias_near": bias_near,
        "bias_past": bias_past, "bias_self": rel_bias[0].reshape(N_HEADS, 1, 1),
        "sub_g_col": sub_g[:, None],
    }


def kernel(x_prompt, x_sample, state_pool, state_hgrn, cache_k, cache_v, page_table, rel_bias, ln_g, ln_b,
           ffn1_w_gu, ffn1_w_dn, ffn2_w_gu, ffn2_w_dn, w_in, w_out, pool_w, pool_scale, hgrn_lb,
           hgrn_norm_g, diff_lam_q1, diff_lam_k1, diff_lam_q2, diff_lam_k2, diff_subln_g, sgu_ln_g,
           sgu_ln_b, sgu_w, sgu_b):
    prm = dict(rel_bias=rel_bias, ln_g=ln_g, ln_b=ln_b, ffn1_w_gu=ffn1_w_gu, ffn1_w_dn=ffn1_w_dn,
               ffn2_w_gu=ffn2_w_gu, ffn2_w_dn=ffn2_w_dn, w_in=w_in, w_out=w_out, pool_w=pool_w,
               pool_scale=pool_scale, hgrn_norm_g=hgrn_norm_g, diff_lam_q1=diff_lam_q1,
               diff_lam_k1=diff_lam_k1, diff_lam_q2=diff_lam_q2, diff_lam_k2=diff_lam_k2,
               diff_subln_g=diff_subln_g, sgu_ln_g=sgu_ln_g, sgu_ln_b=sgu_ln_b, sgu_w=sgu_w, sgu_b=sgu_b)
    depth = w_in.shape[0]
    nb, t, d = x_prompt.shape
    ns = x_sample.shape[0]
    n_pages = page_table.shape[1]
    past_len = n_pages * PAGE
    alpha = (2.0 * depth) ** 0.25
    w = W_GROUP
    tq, tk = _attn_tiles(t)
    assert x_sample.shape[1] == 1 and t % SG_CHUNK == 0 and t % tq == 0 and tk >= REL_MAX_DIST

    lb_cum = jnp.cumsum(jax.nn.softmax(hgrn_lb.astype(F32), axis=0), axis=0)
    lb_all = jnp.maximum(lb_cum - lb_cum[:1], 0.0)

    cache_kt = jnp.transpose(cache_k, (0, 1, 3, 4, 2))
    cache_vt = jnp.transpose(cache_v, (0, 1, 3, 4, 2))
    hgrn_t = jnp.transpose(state_hgrn, (0, 2, 3, 4, 1))
    pool_t = jnp.transpose(state_pool, (0, 2, 1, 3))

    wb = {k: prm[k].astype(BF16) for k in ("ffn1_w_gu", "ffn1_w_dn", "ffn2_w_gu", "ffn2_w_dn", "w_in", "w_out")}
    w_int = jnp.swapaxes(prm["w_in"], 1, 2).astype(BF16)
    w_qkvt = w_int[:, 5 * w:8 * w]

    xp = x_prompt.reshape(nb * t, d)
    xs = x_sample.reshape(ns, d)
    outs = {k: [] for k in ("pool_p", "pool_s", "hgrn_p", "hgrn_s", "k_p", "k_s", "v_p", "v_s", "sgv_p", "sgv_s")}
    for l in range(depth):
        lp = _layer_params(l, depth, tq, tk, past_len, n_pages, prm, lb_all)
        g = [lp["ln_g"][i][None] for i in range(3)]
        b = [lp["ln_b"][i][None] for i in range(3)]

        xp, p_a, p_d, k_rows, q_t, k_t, v_t = _ffn_inproj_prompt(
            xp, wb["ffn1_w_gu"], wb["ffn1_w_dn"], g[0], b[0], wb["w_in"], w_qkvt, l, nb, alpha)
        y_abd, pool16, st, sgv = _mix_prompt(p_a, p_d, nb, lp)
        y_ct = _attn_prompt(q_t, k_rows, v_t, lp, nb)
        xp = _outproj_ffn_prompt(xp, y_abd, y_ct, wb["w_out"], wb["ffn2_w_gu"], wb["ffn2_w_dn"], l,
                                 g[1], b[1], g[2], b[2], alpha)
        outs["k_p"].append(jnp.transpose(k_t.reshape(nb, N_HEADS, D_HEAD, t), (0, 3, 1, 2)))
        outs["v_p"].append(jnp.transpose(v_t.reshape(nb, N_HEADS, D_HEAD, t), (0, 3, 1, 2)))
        outs["pool_p"].append(pool16[:, 1:])
        outs["hgrn_p"].append(jnp.stack(
            [jnp.swapaxes(st[:, h * D_HEAD:(h + 1) * D_HEAD, h * D_HEAD:(h + 1) * D_HEAD], 1, 2)
             for h in range(N_HEADS)], axis=1))
        outs["sgv_p"].append(sgv)

        xs = _ffn(xs, wb["ffn1_w_gu"], wb["ffn1_w_dn"], l, g[0], b[0], alpha)
        ps, pst = _inproj_sample(xs, wb["w_in"], w_int, l)
        y_a, y_d, new_pool, vn = _mix_sample(ps, pool_t[l], lp, past_len)
        new_state, y_bt = _hgrn_sample(pst, hgrn_t[l], lp)
        y_ct = _attn_sample(pst, cache_kt, cache_vt, page_table, l, lp)
        xs = _outproj_sample(xs, y_a, y_bt, y_ct, y_d, wb["w_out"], l, g[1], b[1], alpha)
        xs = _ffn(xs, wb["ffn2_w_gu"], wb["ffn2_w_dn"], l, g[2], b[2], alpha)
        outs["k_s"].append(jnp.transpose(pst[6 * w:7 * w].reshape(N_HEADS, D_HEAD, ns), (2, 0, 1))[:, None])
        outs["v_s"].append(jnp.transpose(pst[7 * w:8 * w].reshape(N_HEADS, D_HEAD, ns), (2, 0, 1))[:, None])
        outs["pool_s"].append(jnp.transpose(new_pool, (1, 0, 2)))
        outs["hgrn_s"].append(jnp.transpose(new_state, (3, 0, 1, 2)))
        outs["sgv_s"].append(vn[:, None])

    st = {k: jnp.stack(v, axis=0) for k, v in outs.items()}
    return (xp.reshape(nb, t, d), xs.reshape(ns, 1, d), st["pool_p"], st["pool_s"], st["hgrn_p"], st["hgrn_s"],
            st["k_p"], st["k_s"], st["v_p"], st["v_s"], st["sgv_p"], st["sgv_s"])
```

```python
import functools
import math

import jax
import jax.numpy as jnp
import jax.scipy.linalg
from jax import lax
from jax.experimental import pallas as pl
from jax.experimental.pallas import tpu as pltpu

F32 = jnp.float32
BF16 = jnp.bfloat16

N_MIX = 4
W_GROUP = 256
N_HEADS = 4
D_HEAD = 64
DA_HALF = 32
POOL_WINDOWS = (2, 4, 8, 16)
POOL_BUF = 15
PAGE = 128
SG_CHUNK = 128
HG_SUB = 16
V_AUG = 80
REL_BUCKETS = 32
REL_MAX_DIST = 128
EPS = 1e-5
NEG = -1e30
LOG2E = 1.0 / math.log(2.0)

VMEM_LIMIT = 56 * 1024 * 1024
MXU_DIM = 256
ROW_TILE = 512
FFN_CHUNKS = 2
ATT_TQ = 1024
ATT_TK = 512
DEC_SAMPLES_PER_STEP = 2
DEC_SLOTS = 3

NT_DIMS = (((1,), (1,)), ((), ()))


def _params(*sem):
    return pltpu.CompilerParams(dimension_semantics=sem, vmem_limit_bytes=VMEM_LIMIT)


def _const_spec(shape, single=False):
    nd = len(shape)
    kw = {"pipeline_mode": pl.Buffered(1)} if single else {}
    return pl.BlockSpec(shape, lambda *_: (0,) * nd, **kw)


def _layer_spec(shape, layer, single=False, rows=None, row_block=0):
    kw = {"pipeline_mode": pl.Buffered(1)} if single else {}
    block = (None, shape[1] if rows is None else rows) + tuple(shape[2:])
    return pl.BlockSpec(block, lambda *_: (layer, row_block) + (0,) * (len(shape) - 2), **kw)


def _layer_norm(y, g, b):
    mu = jnp.mean(y, axis=-1, keepdims=True)
    yc = y - mu
    var = jnp.mean(yc * yc, axis=-1, keepdims=True)
    return yc * lax.rsqrt(var + EPS) * g + b


def _silu(x):
    return x * jax.nn.sigmoid(x)


def _log_forget(z, log_lb, log1m_lb):
    log_sig = jnp.minimum(z, 0.0) - jnp.log(1.0 + jnp.exp(-jnp.abs(z)))
    b = log1m_lb + log_sig
    return jnp.maximum(log_lb, b) + jnp.log(1.0 + jnp.exp(-jnp.abs(log_lb - b)))


def _swiglu_ln(x, wgu_ref, wdn_ref, g, b, d_ff, bounds, alpha):
    xb = x.astype(BF16)
    acc = None
    for lo, hi in zip(bounds[:-1], bounds[1:]):
        gate = jnp.dot(xb, wgu_ref[:, lo:hi], preferred_element_type=F32)
        up = jnp.dot(xb, wgu_ref[:, d_ff + lo:d_ff + hi], preferred_element_type=F32)
        h = (_silu(gate) * up).astype(BF16)
        part = jnp.dot(h, wdn_ref[lo:hi, :], preferred_element_type=F32)
        acc = part if acc is None else acc + part
    return _layer_norm(alpha * x + 0.5 * acc, g, b)


def _ffn_kernel(x_ref, wgu_ref, wdn_ref, g_ref, b_ref, o_ref, *, d_ff, bounds, alpha):
    o_ref[...] = _swiglu_ln(x_ref[...], wgu_ref, wdn_ref, g_ref[...], b_ref[...], d_ff, bounds, alpha)


def _ffn_bounds(d_ff):
    n_tiles = d_ff // MXU_DIM
    cuts = [round(n_tiles * c / FFN_CHUNKS) * MXU_DIM for c in range(FFN_CHUNKS)]
    return tuple(cuts) + (d_ff,)


def _ffn(x, w_gu, w_dn, layer, g, b, alpha):
    r, d = x.shape
    d_ff = w_dn.shape[1]
    tm = min(ROW_TILE, r)
    bounds = _ffn_bounds(d_ff)
    return pl.pallas_call(
        functools.partial(_ffn_kernel, d_ff=d_ff, bounds=bounds, alpha=alpha),
        grid=(r // tm,),
        in_specs=[
            pl.BlockSpec((tm, d), lambda i: (i, 0)),
            _layer_spec(w_gu.shape, layer, single=True),
            _layer_spec(w_dn.shape, layer, single=True),
            _const_spec((1, d)),
            _const_spec((1, d)),
        ],
        out_specs=pl.BlockSpec((tm, d), lambda i: (i, 0)),
        out_shape=jax.ShapeDtypeStruct((r, d), F32),
        compiler_params=_params("arbitrary"),
        name="ffn",
    )(x, w_gu, w_dn, g, b)


def _ffn_inproj_p_kernel(x_ref, wgu_ref, wdn_ref, g_ref, b_ref, w_ref,
                         x1_ref, pa_ref, pd_ref, k_ref, qt_ref, kt_ref, vt_ref, *, d_ff, bounds, alpha, q_scale):
    w = W_GROUP
    x1 = _swiglu_ln(x_ref[...], wgu_ref, wdn_ref, g_ref[...], b_ref[...], d_ff, bounds, alpha)
    x1_ref[...] = x1
    xb = x1.astype(BF16)
    pa_ref[...] = jnp.dot(xb, w_ref[:, 0:5 * w], preferred_element_type=F32)
    pd_ref[...] = jnp.dot(xb, w_ref[:, 8 * w:10 * w], preferred_element_type=F32)
    qkv = jnp.dot(xb, w_ref[:, 5 * w:8 * w], preferred_element_type=F32)
    for h in range(N_HEADS):
        k_ref[h] = qkv[:, w + h * D_HEAD:w + (h + 1) * D_HEAD].astype(BF16)
    qkvt = qkv.T
    qt_ref[0] = (qkvt[0:w] * q_scale).astype(BF16)
    kt_ref[0] = qkvt[w:2 * w]
    vt_ref[0] = qkvt[2 * w:3 * w]


def _ffn_inproj_prompt(x, w_gu, w_dn, g, b, w_in, layer, n_batch, alpha):
    r, d = x.shape
    t = r // n_batch
    w = W_GROUP
    tm = min(ROW_TILE, t)
    tpb = t // tm
    d_ff = w_dn.shape[1]
    chan_major = lambda: pl.BlockSpec((1, w, tm), lambda i: (i // tpb, 0, i % tpb))
    return pl.pallas_call(
        functools.partial(_ffn_inproj_p_kernel, d_ff=d_ff, bounds=_ffn_bounds(d_ff), alpha=alpha,
                          q_scale=DA_HALF ** -0.5 * LOG2E),
        grid=(r // tm,),
        in_specs=[
            pl.BlockSpec((tm, d), lambda i: (i, 0)),
            _layer_spec(w_gu.shape, layer, single=True),
            _layer_spec(w_dn.shape, layer, single=True),
            _const_spec((1, d)), _const_spec((1, d)),
            _layer_spec(w_in.shape, layer, single=True),
        ],
        out_specs=[
            pl.BlockSpec((tm, d), lambda i: (i, 0)),
            pl.BlockSpec((tm, 5 * w), lambda i: (i, 0)),
            pl.BlockSpec((tm, 2 * w), lambda i: (i, 0)),
            pl.BlockSpec((N_HEADS, tm, D_HEAD), lambda i: (0, i, 0)),
            chan_major(), chan_major(), chan_major(),
        ],
        out_shape=[
            jax.ShapeDtypeStruct((r, d), F32),
            jax.ShapeDtypeStruct((r, 5 * w), F32),
            jax.ShapeDtypeStruct((r, 2 * w), F32),
            jax.ShapeDtypeStruct((N_HEADS, r, D_HEAD), BF16),
            jax.ShapeDtypeStruct((n_batch, w, t), BF16),
            jax.ShapeDtypeStruct((n_batch, w, t), F32),
            jax.ShapeDtypeStruct((n_batch, w, t), F32),
        ],
        compiler_params=_params("arbitrary"),
        name="ffn_inproj_prompt",
    )(x, w_gu, w_dn, g, b, w_in)


def _inproj_s_kernel(x_ref, w_ref, p_ref, pt_ref):
    p = jnp.dot(x_ref[...].astype(BF16), w_ref[...], preferred_element_type=F32)
    p_ref[...] = p
    pt_ref[...] = p.T


def _inproj_sample(x, w_in, layer):
    n, d = x.shape
    d_in = w_in.shape[2]
    return pl.pallas_call(
        _inproj_s_kernel,
        grid=(1,),
        in_specs=[_const_spec((n, d)), _layer_spec(w_in.shape, layer)],
        out_specs=[_const_spec((n, d_in)), _const_spec((d_in, n))],
        out_shape=[jax.ShapeDtypeStruct((n, d_in), F32), jax.ShapeDtypeStruct((d_in, n), F32)],
        compiler_params=_params("arbitrary"),
        name="inproj_sample",
    )(x, w_in)


def _group_select(grp, parts):
    out = parts[N_HEADS - 1]
    for g in range(N_HEADS - 2, -1, -1):
        out = jnp.where(grp == g, parts[g], out)
    return out


def _mix_p_kernel(pa_ref, pd_ref, poolw_ref, pscale_ref, loglb_ref, log1mlb_ref, omlb_ref, hgg_ref,
                  sgg_ref, sgb_ref, sgw_ref, sgbias_ref, eones_ref,
                  y_ref, pool_ref, st_out_ref, sgv_ref,
                  prev_ref, st_ref):
    c = SG_CHUNK
    w = W_GROUP
    t = pl.program_id(1)

    @pl.when(t == 0)
    def _():
        prev_ref[...] = jnp.zeros_like(prev_ref)
        st_ref[...] = jnp.zeros_like(st_ref)

    a = pa_ref[:, 0:w]
    hq = pa_ref[:, w:2 * w]
    hf = pa_ref[:, 2 * w:3 * w]
    hi = pa_ref[:, 3 * w:4 * w]
    hg = pa_ref[:, 4 * w:5 * w]
    su = pd_ref[:, 0:w]
    sv = pd_ref[:, w:2 * w]
    lane = lax.broadcasted_iota(jnp.int32, (c, w), 1)
    row = lax.broadcasted_iota(jnp.int32, (c, w), 0)
    grp = lane >> 6

    e = jnp.concatenate([prev_ref[...], a], axis=0)
    s2 = e + pltpu.roll(e, 1, 0)
    s4 = s2 + pltpu.roll(s2, 2, 0)
    s8 = s4 + pltpu.roll(s4, 4, 0)
    s16 = s8 + pltpu.roll(s8, 8, 0)
    wsum = _group_select(grp, [s2[16:], s4[16:], s8[16:], s16[16:]])
    win = _group_select(grp, [jnp.full((c, w), v, jnp.int32) for v in POOL_WINDOWS])
    cnt = jnp.minimum(t * c + row + 1, win).astype(F32)
    dpool = wsum / cnt - a
    ya = jnp.dot(dpool.astype(BF16), poolw_ref[...], preferred_element_type=F32) * pscale_ref[...]
    prev_ref[...] = a[c - 16:]
    pool_ref[0] = a[c - 16:]

    vn = _layer_norm(sv, sgg_ref[...], sgb_ref[...])
    sg = jnp.dot(sgw_ref[...], vn.astype(BF16), preferred_element_type=F32)
    s_gate = _group_select(grp, [sg[g * c:(g + 1) * c] for g in range(N_HEADS)]) + sgbias_ref[...]
    yd = su * s_gate
    sgv_ref[0] = vn

    n_sub = c // HG_SUB
    logf = _log_forget(hf, loglb_ref[...], log1mlb_ref[...])
    kin = omlb_ref[...] * jax.nn.sigmoid(-hf)
    r16 = row & (HG_SUB - 1)
    bl = logf
    rv = logf
    for sh in (1, 2, 4, 8):
        bl = bl + jnp.where(r16 >= sh, pltpu.roll(bl, sh, 0), 0.0)
        rv = rv + jnp.where(r16 + sh < HG_SUB, pltpu.roll(rv, c - sh, 0), 0.0)
    sub_row = row >> 4
    qtb = (hq * jnp.exp(bl)).astype(BF16)
    kt = kin * jnp.exp(rv - logf)
    dec = jnp.exp(rv)
    vtb = hi.T.astype(BF16)
    bi0 = lax.broadcasted_iota(jnp.int32, (w, w), 0) >> 6
    bi1 = lax.broadcasted_iota(jnp.int32, (w, w), 1) >> 6
    blockmask = bi0 == bi1
    st = st_ref[...]
    seen = []
    for i in range(n_sub):
        seen.append(st.astype(BF16))
        km = jnp.where(sub_row == i, kt, 0.0).astype(BF16)
        u = jnp.dot(vtb, km, preferred_element_type=F32)
        st = dec[i * HG_SUB:i * HG_SUB + 1] * st + jnp.where(blockmask, u, 0.0)
    st_ref[...] = st
    st_out_ref[0] = st
    qx = jnp.concatenate([jnp.where(sub_row == i, qtb, jnp.zeros_like(qtb)) for i in range(n_sub)], axis=1)
    o = lax.dot_general(qx, jnp.concatenate(seen, axis=1), NT_DIMS,
                        preferred_element_type=F32)

    rt = {lo: lax.broadcasted_iota(jnp.int32, (HG_SUB - lo, w), 0) + lo for lo in (0, 8)}
    bl2 = bl * LOG2E
    xs = []
    for i in range(n_sub):
        rows = slice(i * HG_SUB, (i + 1) * HG_SUB)
        bli, qi, ki = bl2[rows], hq[rows], kin[rows]
        for s in range(HG_SUB):
            lo = (s // 8) * 8
            ratio = jnp.exp2(jnp.where(rt[lo] >= s, bli[lo:] - bli[s:s + 1], NEG))
            x_s = ratio * qi[lo:] * ki[s:s + 1]
            if lo:
                x_s = jnp.concatenate([jnp.zeros((lo, w), F32), x_s], axis=0)
            xs.append(x_s.astype(BF16))
    x = jnp.concatenate(xs, axis=0)
    r = jnp.dot(x, eones_ref[...], preferred_element_type=F32)
    o_intra = []
    for i in range(n_sub):
        vi = hi[i * HG_SUB:(i + 1) * HG_SUB]
        base = i * HG_SUB * HG_SUB
        oi = r[base:base + HG_SUB] * vi[0:1]
        for s in range(1, HG_SUB):
            oi = oi + r[base + s * HG_SUB:base + (s + 1) * HG_SUB] * vi[s:s + 1]
        o_intra.append(oi)
    o = o + jnp.concatenate(o_intra, axis=0)
    ms = [jnp.mean(jnp.square(o[:, g * D_HEAD:(g + 1) * D_HEAD]), axis=-1, keepdims=True)
          for g in range(N_HEADS)]
    yb = o * lax.rsqrt(_group_select(grp, ms) + EPS) * hgg_ref[...] * _silu(hg)

    y_ref[:, 0:w] = ya.astype(BF16)
    y_ref[:, w:2 * w] = yb.astype(BF16)
    y_ref[:, 2 * w:3 * w] = yd.astype(BF16)


def _mix_prompt(p_a, p_d, n_batch, lp):
    r = p_a.shape[0]
    t = r // n_batch
    c = SG_CHUNK
    nt = t // c
    w = W_GROUP
    row = lambda: _const_spec((1, w))
    return pl.pallas_call(
        _mix_p_kernel,
        grid=(n_batch, nt),
        in_specs=[
            pl.BlockSpec((c, 5 * w), lambda b, i: (b * nt + i, 0)),
            pl.BlockSpec((c, 2 * w), lambda b, i: (b * nt + i, 0)),
            _const_spec((w, w)), row(), row(), row(), row(), row(), row(), row(),
            _const_spec((N_HEADS * c, c)), _const_spec((c, w)), _const_spec((w, w)),
        ],
        out_specs=[
            pl.BlockSpec((c, 3 * w), lambda b, i: (b * nt + i, 0)),
            pl.BlockSpec((1, 16, w), lambda b, i: (b, 0, 0)),
            pl.BlockSpec((1, w, w), lambda b, i: (b, 0, 0)),
            pl.BlockSpec((1, c, w), lambda b, i: (b, 0, 0)),
        ],
        out_shape=[
            jax.ShapeDtypeStruct((r, 3 * w), BF16),
            jax.ShapeDtypeStruct((n_batch, 16, w), F32),
            jax.ShapeDtypeStruct((n_batch, w, w), F32),
            jax.ShapeDtypeStruct((n_batch, c, w), F32),
        ],
        scratch_shapes=[pltpu.VMEM((16, w), F32), pltpu.VMEM((w, w), F32)],
        compiler_params=_params("arbitrary", "arbitrary"),
        name="mix_prompt",
    )(p_a, p_d, lp["pool_w"], lp["pool_scale"], lp["log_lb"], lp["log1m_lb"], lp["om_lb"], lp["hg_g"],
      lp["sg_g"], lp["sg_b"], lp["sg_w"], lp["sg_bias"], lp["eones"])


def _attn_p_kernel(cfar_ref, lam_ref, qt_ref, k_ref, vt_ref, near_ref, subg_ref, o_ref,
                   va_ref, m_ref, acc_ref, s_ref, mb_ref, *, tq, tk):
    h = pl.program_id(1)
    qi = pl.program_id(2)
    dh = D_HEAD

    @pl.when(qi == 0)
    def _():
        va_ref[0:dh, :] = vt_ref[0].astype(BF16)
        rows = lax.broadcasted_iota(jnp.int32, (V_AUG - dh, va_ref.shape[1]), 0)
        va_ref[dh:V_AUG, :] = jnp.where(rows == 0, 1.0, 0.0).astype(BF16)

    qt = qt_ref[0]
    sub = lax.broadcasted_iota(jnp.int32, qt.shape, 0)
    zero = jnp.zeros_like(qt)
    qm = (jnp.where(sub < DA_HALF, qt, zero), jnp.where(sub >= DA_HALF, qt, zero))
    m_ref[...] = jnp.full_like(m_ref, NEG)
    acc_ref[...] = jnp.zeros_like(acc_ref)

    def qk(ki, slot, near=None, q0=0):
        k0 = pl.multiple_of(ki * tk, tk)
        kblk = k_ref[0, pl.ds(k0, tk), :]
        for mi in range(2):
            s = jnp.dot(kblk, qm[mi][:, q0:], preferred_element_type=F32)
            if near is not None:
                s = s + near_ref[0, near, :, q0:]
            s_ref[slot, mi, :, q0:] = s
            mb_ref[slot, mi, :, q0:] = jnp.max(s, axis=0, keepdims=True)

    def softmax_pv(ki, slot, shift, q0=0):
        k0 = pl.multiple_of(ki * tk, tk)
        vblk = va_ref[:, pl.ds(k0, tk)]
        for mi in range(2):
            s = s_ref[slot, mi, :, q0:]
            m_blk = mb_ref[slot, mi, :, q0:]
            if shift is not None:
                m_blk = m_blk + shift
            m_old = m_ref[mi, :, q0:]
            m_new = jnp.maximum(m_old, m_blk)
            alpha = jnp.exp2(m_old - m_new)
            p = jnp.exp2(s - (m_new if shift is None else m_new - shift)).astype(BF16)
            acc_ref[mi, :, q0:] = alpha * acc_ref[mi, :, q0:] + jnp.dot(vblk, p, preferred_element_type=F32)
            m_ref[mi, :, q0:] = m_new

    cfar = cfar_ref[h]
    n_diag = tq // tk
    kb0 = qi * n_diag

    def near_tiles(first_tile, first_near, n, slot0):
        q0s = [max(first_near + j - 1, 0) * tk for j in range(n)]
        for j in range(n):
            if j + 1 < n:
                qk(first_tile + j + 1, (slot0 + j + 1) % 2, first_near + j + 1, q0s[j + 1])
            softmax_pv(first_tile + j, (slot0 + j) % 2, None, q0s[j])

    def far_pair(pi, carry):
        b = 2 * pi
        qk(b + 1, 1)
        softmax_pv(b, 0, cfar)
        qk(b + 2, 0)
        softmax_pv(b + 1, 1, cfar)
        return carry

    @pl.when(qi == 0)
    def _():
        qk(0, 0, 1)
        near_tiles(0, 1, n_diag, 0)

    @pl.when(qi >= 1)
    def _():
        n_far = kb0 - 1
        qk(0, 0)
        lax.fori_loop(0, (n_far - 1) // 2, far_pair, 0)

        def tail(rem):
            if rem == 2:
                qk(n_far - 1, 1)
                softmax_pv(n_far - 2, 0, cfar)
            last_slot = rem % 2
            qk(n_far, last_slot, 0)
            softmax_pv(n_far - 1, 1 - last_slot, cfar)
            near_tiles(n_far, 0, n_diag + 1, last_slot)

        if n_diag % 2 == 0:
            tail(1)
        else:
            pl.when(n_far % 2 == 1)(lambda: tail(1))
            pl.when(n_far % 2 == 0)(lambda: tail(2))

    a0 = acc_ref[0]
    a1 = acc_ref[1]
    o = a0[0:dh] / a0[dh:dh + 1] - lam_ref[0] * (a1[0:dh] / a1[dh:dh + 1])
    ms = jnp.mean(o * o, axis=0, keepdims=True)
    o_ref[0] = (o * lax.rsqrt(ms + EPS) * subg_ref[...]).astype(BF16)


def _attn_tiles(t):
    tq = min(ATT_TQ, t)
    return tq, min(ATT_TK, tq // 2)


def _attn_prompt(qt, k_rows, vt, lp, n_batch):
    t = qt.shape[2]
    tq, tk = _attn_tiles(t)
    nq = t // tq
    n_near = tq // tk + 1
    smem = pl.BlockSpec(memory_space=pltpu.SMEM)
    return pl.pallas_call(
        functools.partial(_attn_p_kernel, tq=tq, tk=tk),
        grid=(n_batch, N_HEADS, nq),
        in_specs=[
            smem, smem,
            pl.BlockSpec((1, D_HEAD, tq), lambda b, h, i: (b, h, i)),
            pl.BlockSpec((1, t, D_HEAD), lambda b, h, i: (h, b, 0)),
            pl.BlockSpec((1, D_HEAD, t), lambda b, h, i: (b, h, 0)),
            pl.BlockSpec((1, n_near, tk, tq), lambda b, h, i: (h, 0, 0, 0)),
            _const_spec((D_HEAD, 1)),
        ],
        out_specs=pl.BlockSpec((1, D_HEAD, tq), lambda b, h, i: (b, h, i)),
        out_shape=jax.ShapeDtypeStruct((n_batch, W_GROUP, t), BF16),
        scratch_shapes=[
            pltpu.VMEM((V_AUG, t), BF16),
            pltpu.VMEM((2, 1, tq), F32), pltpu.VMEM((2, V_AUG, tq), F32),
            pltpu.VMEM((2, 2, tk, tq), F32), pltpu.VMEM((2, 2, 1, tq), F32),
        ],
        compiler_params=_params("arbitrary", "arbitrary", "arbitrary"),
        name="attn_prompt",
    )(lp["c_far"], lp["lam"], qt, k_rows, vt, lp["bias_near"], lp["sub_g_col"])


def _outproj_ffn_p_kernel(x_ref, y_ref, yct_ref, wab_ref, wc_ref, wd_ref, g1_ref, b1_ref,
                          wgu_ref, wdn_ref, g2_ref, b2_ref, o_ref, *, d_ff, bounds, alpha):
    w = W_GROUP
    acc = jnp.dot(y_ref[:, 0:2 * w], wab_ref[...], preferred_element_type=F32)
    acc = acc + jnp.dot(y_ref[:, 2 * w:3 * w], wd_ref[...], preferred_element_type=F32)
    yc = yct_ref[0].astype(F32).T.astype(BF16)
    acc = acc + jnp.dot(yc, wc_ref[...], preferred_element_type=F32)
    x2 = _layer_norm(alpha * x_ref[...] + acc, g1_ref[...], b1_ref[...])
    o_ref[...] = _swiglu_ln(x2, wgu_ref, wdn_ref, g2_ref[...], b2_ref[...], d_ff, bounds, alpha)


def _outproj_ffn_prompt(x, y_abd, y_ct, w_out, w_gu, w_dn, layer, g1, b1, g2, b2, alpha):
    r, d = x.shape
    w = W_GROUP
    t = y_ct.shape[2]
    tm = min(ROW_TILE, t)
    tpb = t // tm
    d_ff = w_dn.shape[1]
    return pl.pallas_call(
        functools.partial(_outproj_ffn_p_kernel, d_ff=d_ff, bounds=_ffn_bounds(d_ff), alpha=alpha),
        grid=(r // tm,),
        in_specs=[
            pl.BlockSpec((tm, d), lambda i: (i, 0)),
            pl.BlockSpec((tm, 3 * W_GROUP), lambda i: (i, 0)),
            pl.BlockSpec((1, W_GROUP, tm), lambda i: (i // tpb, 0, i % tpb)),
            _layer_spec(w_out.shape, layer, rows=2 * w, row_block=0),
            _layer_spec(w_out.shape, layer, rows=w, row_block=2),
            _layer_spec(w_out.shape, layer, rows=w, row_block=3),
            _const_spec((1, d)), _const_spec((1, d)),
            _layer_spec(w_gu.shape, layer, single=True),
            _layer_spec(w_dn.shape, layer, single=True),
            _const_spec((1, d)), _const_spec((1, d)),
        ],
        out_specs=pl.BlockSpec((tm, d), lambda i: (i, 0)),
        out_shape=jax.ShapeDtypeStruct((r, d), F32),
        compiler_params=_params("arbitrary"),
        name="outproj_ffn_prompt",
    )(x, y_abd, y_ct, w_out, w_out, w_out, g1, b1, w_gu, w_dn, g2, b2)


def _outproj_s_kernel(x_ref, ya_ref, ybt_ref, yct_ref, yd_ref, w_ref, g_ref, b_ref, o_ref, *, alpha):
    w = W_GROUP
    parts = (ya_ref[...], ybt_ref[...].T, yct_ref[...].T, yd_ref[...])
    acc = None
    for j, part in enumerate(parts):
        d = jnp.dot(part.astype(BF16), w_ref[j * w:(j + 1) * w, :], preferred_element_type=F32)
        acc = d if acc is None else acc + d
    o_ref[...] = _layer_norm(alpha * x_ref[...] + acc, g_ref[...], b_ref[...])


def _outproj_sample(x, y_a, y_bt, y_ct, y_d, w_out, layer, g, b, alpha):
    n, d = x.shape
    w = W_GROUP
    return pl.pallas_call(
        functools.partial(_outproj_s_kernel, alpha=alpha),
        grid=(1,),
        in_specs=[
            _const_spec((n, d)), _const_spec((n, w)), _const_spec((w, n)), _const_spec((w, n)),
            _const_spec((n, w)), _layer_spec(w_out.shape, layer), _const_spec((1, d)), _const_spec((1, d)),
        ],
        out_specs=_const_spec((n, d)),
        out_shape=jax.ShapeDtypeStruct((n, d), F32),
        compiler_params=_params("arbitrary"),
        name="outproj_sample",
    )(x, y_a, y_bt, y_ct, y_d, w_out, g, b)


def _mix_s_kernel(pa_ref, pd_ref, pool_ref, poolw_ref, pscale_ref, sgg_ref, sgb_ref, w00_ref, b0_ref,
                  ya_ref, yd_ref, newpool_ref, vn_ref, *, cnts):
    a = pa_ref[...]
    n, w = a.shape
    grp = lax.broadcasted_iota(jnp.int32, (n, w), 1) >> 6
    acc = a
    means = []
    for j in range(1, POOL_BUF + 1):
        acc = acc + pool_ref[POOL_BUF - j]
        if j + 1 in POOL_WINDOWS:
            means.append(acc / cnts[POOL_WINDOWS.index(j + 1)])
    dpool = _group_select(grp, means) - a
    ya_ref[...] = jnp.dot(dpool.astype(BF16), poolw_ref[...], preferred_element_type=F32) * pscale_ref[...]
    for j in range(POOL_BUF - 1):
        newpool_ref[j] = pool_ref[j + 1]
    newpool_ref[POOL_BUF - 1] = a

    vn = _layer_norm(pd_ref[:, w:2 * w], sgg_ref[...], sgb_ref[...])
    vn_ref[...] = vn
    yd_ref[...] = pd_ref[:, 0:w] * (w00_ref[...] * vn + b0_ref[...])


def _mix_sample(p, pool, lp, past_len):
    n = p.shape[0]
    w = W_GROUP
    cnts = tuple(float(min(past_len + 1, win)) for win in POOL_WINDOWS)
    row = lambda: _const_spec((1, w))
    return pl.pallas_call(
        functools.partial(_mix_s_kernel, cnts=cnts),
        grid=(1,),
        in_specs=[
            pl.BlockSpec((n, w), lambda i: (0, 0)),
            pl.BlockSpec((n, 2 * w), lambda i: (0, 4)),
            _const_spec(pool.shape), _const_spec((w, w)), row(), row(), row(), row(), row(),
        ],
        out_specs=[_const_spec((n, w)), _const_spec((n, w)), _const_spec(pool.shape), _const_spec((n, w))],
        out_shape=[
            jax.ShapeDtypeStruct((n, w), F32), jax.ShapeDtypeStruct((n, w), F32),
            jax.ShapeDtypeStruct(pool.shape, F32), jax.ShapeDtypeStruct((n, w), F32),
        ],
        compiler_params=_params("arbitrary"),
        name="mix_sample",
    )(p, p, pool, lp["pool_w"], lp["pool_scale"], lp["sg_g"], lp["sg_b"], lp["sg_w00"], lp["sg_b0"])


def _hgrn_s_kernel(q_ref, f_ref, i_ref, g_ref, s_ref, loglb_ref, log1mlb_ref, omlb_ref, hgg_ref,
                   so_ref, yb_ref):
    z = f_ref[...]
    f = jnp.exp(_log_forget(z, loglb_ref[...], log1mlb_ref[...]))
    kin = omlb_ref[...] * jax.nn.sigmoid(-z)
    q = q_ref[...]
    v = i_ref[...]
    o = jnp.zeros_like(v)
    for d in range(D_HEAD):
        sn = f[d:d + 1] * s_ref[0, d] + kin[d:d + 1] * v
        so_ref[0, d] = sn
        o = o + q[d:d + 1] * sn
    ms = jnp.mean(o * o, axis=0, keepdims=True)
    yb_ref[...] = o * lax.rsqrt(ms + EPS) * hgg_ref[...] * _silu(g_ref[...])


def _hgrn_sample(pt, state, lp):
    n = pt.shape[1]
    dh = D_HEAD
    blk = lambda off: pl.BlockSpec((dh, n), lambda h: (off * N_HEADS + h, 0))
    col = lambda: pl.BlockSpec((dh, 1), lambda h: (h, 0))
    return pl.pallas_call(
        _hgrn_s_kernel,
        grid=(N_HEADS,),
        in_specs=[
            blk(1), blk(2), blk(3), blk(4),
            pl.BlockSpec((1, dh, dh, n), lambda h: (h, 0, 0, 0)),
            col(), col(), col(), _const_spec((dh, 1)),
        ],
        out_specs=[
            pl.BlockSpec((1, dh, dh, n), lambda h: (h, 0, 0, 0)),
            pl.BlockSpec((dh, n), lambda h: (h, 0)),
        ],
        out_shape=[
            jax.ShapeDtypeStruct(state.shape, F32),
            jax.ShapeDtypeStruct((W_GROUP, n), F32),
        ],
        compiler_params=_params("arbitrary"),
        name="hgrn_sample",
    )(pt, pt, pt, pt, state, lp["log_lb_col"], lp["log1m_lb_col"], lp["om_lb_col"], lp["hg_g_col"])


def _attn_s_one(n, k_pages, v_pages, lam, qt_ref, knt_ref, vnt_ref, bias_ref, bself_ref, n_pages, q_scale):
    w, ns = qt_ref.shape
    dh = D_HEAD
    sel = lax.broadcasted_iota(jnp.int32, (w, ns), 1) == n

    def column(ref):
        return jnp.sum(jnp.where(sel, ref[...], 0.0), axis=1, keepdims=True)

    qcol = column(qt_ref) * q_scale
    kcol = column(knt_ref)
    vcol = column(vnt_ref)

    n_grp = 2 * N_HEADS
    rows = ([], [])
    self_rows = ([], [])
    for h in range(N_HEADS):
        qh = qcol[h * dh:(h + 1) * dh]
        for j in range(n_pages):
            prod = k_pages[j][h] * qh
            rows[0].append(jnp.sum(prod[0:DA_HALF], axis=0, keepdims=True))
            rows[1].append(jnp.sum(prod[DA_HALF:dh], axis=0, keepdims=True))
        self_prod = qh * kcol[h * dh:(h + 1) * dh]
        for mi in range(2):
            s_self = (jnp.sum(self_prod[mi * DA_HALF:(mi + 1) * DA_HALF], axis=0, keepdims=True)
                      + bself_ref[h])
            self_rows[mi].append(jnp.broadcast_to(s_self, (n_pages, 1)))
    s = jnp.concatenate(rows[0] + rows[1], axis=0) + bias_ref[...]
    s_self = jnp.concatenate(self_rows[0] + self_rows[1], axis=0)

    def per_group(col, reduce):
        parts = [jnp.broadcast_to(reduce(col[g * n_pages:(g + 1) * n_pages], axis=0, keepdims=True),
                                  (n_pages, 1)) for g in range(n_grp)]
        return jnp.concatenate(parts, axis=0)

    m = jnp.maximum(per_group(jnp.max(s, axis=1, keepdims=True), jnp.max), s_self)
    p = jnp.exp(s - m)
    p_self = jnp.exp(s_self - m)
    inv_l = 1.0 / (per_group(jnp.sum(p, axis=1, keepdims=True), jnp.sum) + p_self)
    half = N_HEADS * n_pages
    a = (p * inv_l)[0:half] - lam * (p * inv_l)[half:2 * half]
    a_self = (p_self * inv_l)[0:half] - lam * (p_self * inv_l)[half:2 * half]
    weighted = []
    for h in range(N_HEADS):
        oh = v_pages[0][h] * a[h * n_pages:h * n_pages + 1]
        for j in range(1, n_pages):
            oh = oh + v_pages[j][h] * a[h * n_pages + j:h * n_pages + j + 1]
        weighted.append(oh)
    a_self_col = jnp.concatenate(
        [jnp.broadcast_to(a_self[h * n_pages:h * n_pages + 1], (dh, 1)) for h in range(N_HEADS)], axis=0)
    ocol = jnp.sum(jnp.concatenate(weighted, axis=0), axis=1, keepdims=True) + a_self_col * vcol
    return sel, ocol


def _attn_s_kernel(pt_ref, lam_ref, qt_ref, knt_ref, vnt_ref, bias_ref, bself_ref, subg_ref, ck_ref, cv_ref,
                   o_ref, ot_ref, kbuf_ref, vbuf_ref, sem_ref, *, n_pages, sps, layer, q_scale):
    dh = D_HEAD
    step = pl.program_id(0)
    n_steps = pl.num_programs(0)

    def page_copies(s, slot):
        copies = []
        for u in range(sps):
            for j in range(n_pages):
                page = pt_ref[s * sps + u, j]
                copies.append(pltpu.make_async_copy(ck_ref.at[layer, page], kbuf_ref.at[slot, u * n_pages + j],
                                                    sem_ref.at[0, slot]))
                copies.append(pltpu.make_async_copy(cv_ref.at[layer, page], vbuf_ref.at[slot, u * n_pages + j],
                                                    sem_ref.at[1, slot]))
        return copies

    @pl.when(step == 0)
    def _():
        for s in range(DEC_SLOTS - 1):
            @pl.when(s < n_steps)
            def _():
                for c in page_copies(s, s):
                    c.start()

    slot = step % DEC_SLOTS
    for c in page_copies(step, slot):
        c.wait()

    ot = ot_ref[...]
    for u in range(sps):
        k_pages = [kbuf_ref.at[slot, u * n_pages + j] for j in range(n_pages)]
        v_pages = [vbuf_ref.at[slot, u * n_pages + j] for j in range(n_pages)]
        sel, ocol = _attn_s_one(step * sps + u, k_pages, v_pages, lam_ref[0],
                                qt_ref, knt_ref, vnt_ref, bias_ref, bself_ref, n_pages, q_scale)
        ot = jnp.where(sel, ocol, ot)
    ot_ref[...] = ot

    ahead = step + DEC_SLOTS - 1

    @pl.when(ahead < n_steps)
    def _():
        for c in page_copies(ahead, ahead % DEC_SLOTS):
            c.start()

    @pl.when(step == n_steps - 1)
    def _():
        for h in range(N_HEADS):
            oh = ot_ref[h * dh:(h + 1) * dh, :]
            ms = jnp.mean(oh * oh, axis=0, keepdims=True)
            o_ref[h * dh:(h + 1) * dh, :] = oh * lax.rsqrt(ms + EPS) * subg_ref[...]


def _attn_sample(pt, cache_kt, cache_vt, page_table, layer, lp):
    n = pt.shape[1]
    n_pages = page_table.shape[1]
    w = W_GROUP
    dh = D_HEAD
    blk = lambda off: pl.BlockSpec((w, n), lambda i, tbl: (off, 0))
    sps = DEC_SAMPLES_PER_STEP
    ring = (DEC_SLOTS, sps * n_pages, N_HEADS, dh, PAGE)
    grid_spec = pltpu.PrefetchScalarGridSpec(
        num_scalar_prefetch=1,
        grid=(n // sps,),
        in_specs=[
            pl.BlockSpec(memory_space=pltpu.SMEM),
            blk(5), blk(6), blk(7),
            pl.BlockSpec((2 * N_HEADS * n_pages, PAGE), lambda i, tbl: (0, 0)),
            pl.BlockSpec((N_HEADS, 1, 1), lambda i, tbl: (0, 0, 0)),
            pl.BlockSpec((dh, 1), lambda i, tbl: (0, 0)),
            pl.BlockSpec(memory_space=pl.ANY), pl.BlockSpec(memory_space=pl.ANY),
        ],
        out_specs=pl.BlockSpec((w, n), lambda i, tbl: (0, 0)),
        scratch_shapes=[
            pltpu.VMEM((w, n), F32), pltpu.VMEM(ring, F32), pltpu.VMEM(ring, F32),
            pltpu.SemaphoreType.DMA((2, DEC_SLOTS)),
        ],
    )
    return pl.pallas_call(
        functools.partial(_attn_s_kernel, n_pages=n_pages, sps=sps, layer=layer, q_scale=DA_HALF ** -0.5),
        grid_spec=grid_spec,
        out_shape=jax.ShapeDtypeStruct((w, n), F32),
        compiler_params=_params("arbitrary"),
        name="attn_sample",
    )(page_table, lp["lam"], pt, pt, pt, lp["bias_past"], lp["bias_self"], lp["sub_g_col"],
      cache_kt, cache_vt)


def _rel_bucket(dist):
    n = jnp.maximum(dist, 0)
    max_exact = REL_BUCKETS // 2
    large = max_exact + (jnp.log(jnp.maximum(n, 1).astype(F32) / max_exact)
                         / math.log(REL_MAX_DIST / max_exact) * (REL_BUCKETS - max_exact)).astype(jnp.int32)
    large = jnp.minimum(large, REL_BUCKETS - 1)
    return jnp.where(n < max_exact, n, large)


def _layer_params(l, depth, tq, tk, past_len, n_pages, prm, lb_all):
    w = W_GROUP
    lam_init = 0.8 - 0.6 * math.exp(-0.3 * l)
    lam = (jnp.exp(jnp.sum(prm["diff_lam_q1"][l] * prm["diff_lam_k1"][l]))
           - jnp.exp(jnp.sum(prm["diff_lam_q2"][l] * prm["diff_lam_k2"][l])) + lam_init)
    rel_bias = prm["rel_bias"]

    def bias_of(dist):
        onehot = _rel_bucket(dist)[..., None] == jnp.arange(REL_BUCKETS, dtype=jnp.int32)
        table = rel_bias.T.reshape((N_HEADS,) + (1,) * dist.ndim + (REL_BUCKETS,))
        return jnp.sum(jnp.where(onehot[None], table, 0.0), axis=-1)

    blk = REL_MAX_DIST
    k0 = jnp.arange(blk, dtype=jnp.int32)[:, None]
    q0 = jnp.arange(blk, dtype=jnp.int32)[None, :]
    blocks = {}

    def block(e):
        e = max(min(e, 2), -1)
        if e not in blocks:
            dist = blk * e + q0 - k0
            blocks[e] = jnp.where((dist >= 0)[None], bias_of(dist), NEG) * LOG2E
        return blocks[e]

    tiles = []
    for r in range(-1, tq // tk):
        rows = [jnp.concatenate([block(q1 - k1 - r * (tk // blk)) for q1 in range(tq // blk)], axis=-1)
                for k1 in range(tk // blk)]
        tiles.append(jnp.concatenate(rows, axis=-2))
    bias_near = jnp.stack(tiles, axis=1)
    kpos = jnp.arange(n_pages * PAGE, dtype=jnp.int32)
    bias_past = jnp.tile(bias_of(past_len - kpos).reshape(N_HEADS * n_pages, PAGE), (2, 1))
    lb = lb_all[l]
    tril = jnp.tril(jnp.ones((SG_CHUNK, SG_CHUNK), F32))
    sub_g = prm["diff_subln_g"][l] * (1.0 - lam_init)
    return {
        "ln_g": prm["ln_g"][l], "ln_b": prm["ln_b"][l],
        "pool_w": jax.scipy.linalg.block_diag(*prm["pool_w"][l]).astype(BF16),
        "pool_scale": prm["pool_scale"][l][None],
        "log_lb": jnp.log(lb)[None], "log1m_lb": jnp.log1p(-lb)[None], "om_lb": (1.0 - lb)[None],
        "log_lb_col": jnp.log(lb)[:, None], "log1m_lb_col": jnp.log1p(-lb)[:, None],
        "om_lb_col": (1.0 - lb)[:, None],
        "hg_g": jnp.tile(prm["hgrn_norm_g"][l], N_HEADS)[None], "hg_g_col": prm["hgrn_norm_g"][l][:, None],
        "sg_g": prm["sgu_ln_g"][l][None], "sg_b": prm["sgu_ln_b"][l][None],
        "sg_w": (prm["sgu_w"][l] * tril).reshape(N_HEADS * SG_CHUNK, SG_CHUNK).astype(BF16),
        "sg_bias": jnp.repeat(prm["sgu_b"][l].T, D_HEAD, axis=1),
        "sg_w00": jnp.repeat(prm["sgu_w"][l][:, 0, 0], D_HEAD)[None],
        "sg_b0": jnp.repeat(prm["sgu_b"][l][:, 0], D_HEAD)[None],
        "eones": jnp.kron(jnp.eye(N_HEADS, dtype=F32), jnp.ones((D_HEAD, D_HEAD), F32)).astype(BF16),
        "lam": lam.reshape(1), "c_far": rel_bias[REL_BUCKETS - 1] * LOG2E,
        "bias_near": bias_near,
        "bias_past": bias_past, "bias_self": rel_bias[0].reshape(N_HEADS, 1, 1),
        "sub_g_col": sub_g[:, None],
    }


def kernel(x_prompt, x_sample, state_pool, state_hgrn, cache_k, cache_v, page_table, rel_bias, ln_g, ln_b,
           ffn1_w_gu, ffn1_w_dn, ffn2_w_gu, ffn2_w_dn, w_in, w_out, pool_w, pool_scale, hgrn_lb,
           hgrn_norm_g, diff_lam_q1, diff_lam_k1, diff_lam_q2, diff_lam_k2, diff_subln_g, sgu_ln_g,
           sgu_ln_b, sgu_w, sgu_b):
    prm = dict(rel_bias=rel_bias, ln_g=ln_g, ln_b=ln_b, ffn1_w_gu=ffn1_w_gu, ffn1_w_dn=ffn1_w_dn,
               ffn2_w_gu=ffn2_w_gu, ffn2_w_dn=ffn2_w_dn, w_in=w_in, w_out=w_out, pool_w=pool_w,
               pool_scale=pool_scale, hgrn_norm_g=hgrn_norm_g, diff_lam_q1=diff_lam_q1,
               diff_lam_k1=diff_lam_k1, diff_lam_q2=diff_lam_q2, diff_lam_k2=diff_lam_k2,
               diff_subln_g=diff_subln_g, sgu_ln_g=sgu_ln_g, sgu_ln_b=sgu_ln_b, sgu_w=sgu_w, sgu_b=sgu_b)
    depth = w_in.shape[0]
    nb, t, d = x_prompt.shape
    ns = x_sample.shape[0]
    n_pages = page_table.shape[1]
    past_len = n_pages * PAGE
    alpha = (2.0 * depth) ** 0.25
    w = W_GROUP
    tq, tk = _attn_tiles(t)
    assert x_sample.shape[1] == 1 and t % SG_CHUNK == 0 and t % tq == 0 and tk >= REL_MAX_DIST

    lb_cum = jnp.cumsum(jax.nn.softmax(hgrn_lb.astype(F32), axis=0), axis=0)
    lb_all = jnp.maximum(lb_cum - lb_cum[:1], 0.0)

    cache_kt = jnp.transpose(cache_k, (0, 1, 3, 4, 2))
    cache_vt = jnp.transpose(cache_v, (0, 1, 3, 4, 2))
    hgrn_t = jnp.transpose(state_hgrn, (0, 2, 3, 4, 1))
    pool_t = jnp.transpose(state_pool, (0, 2, 1, 3))

    wb = {k: prm[k].astype(BF16) for k in ("ffn1_w_gu", "ffn1_w_dn", "ffn2_w_gu", "ffn2_w_dn", "w_in", "w_out")}

    xp = x_prompt.reshape(nb * t, d)
    xs = x_sample.reshape(ns, d)
    outs = {k: [] for k in ("pool_p", "pool_s", "hgrn_p", "hgrn_s", "k_p", "k_s", "v_p", "v_s", "sgv_p", "sgv_s")}
    for l in range(depth):
        lp = _layer_params(l, depth, tq, tk, past_len, n_pages, prm, lb_all)
        g = [lp["ln_g"][i][None] for i in range(3)]
        b = [lp["ln_b"][i][None] for i in range(3)]

        xp, p_a, p_d, k_rows, q_t, k_t, v_t = _ffn_inproj_prompt(
            xp, wb["ffn1_w_gu"], wb["ffn1_w_dn"], g[0], b[0], wb["w_in"], l, nb, alpha)
        y_abd, pool16, st, sgv = _mix_prompt(p_a, p_d, nb, lp)
        y_ct = _attn_prompt(q_t, k_rows, v_t, lp, nb)
        xp = _outproj_ffn_prompt(xp, y_abd, y_ct, wb["w_out"], wb["ffn2_w_gu"], wb["ffn2_w_dn"], l,
                                 g[1], b[1], g[2], b[2], alpha)
        outs["k_p"].append(jnp.transpose(k_t.reshape(nb, N_HEADS, D_HEAD, t), (0, 3, 1, 2)))
        outs["v_p"].append(jnp.transpose(v_t.reshape(nb, N_HEADS, D_HEAD, t), (0, 3, 1, 2)))
        outs["pool_p"].append(pool16[:, 1:])
        outs["hgrn_p"].append(jnp.stack(
            [jnp.swapaxes(st[:, h * D_HEAD:(h + 1) * D_HEAD, h * D_HEAD:(h + 1) * D_HEAD], 1, 2)
             for h in range(N_HEADS)], axis=1))
        outs["sgv_p"].append(sgv)

        xs = _ffn(xs, wb["ffn1_w_gu"], wb["ffn1_w_dn"], l, g[0], b[0], alpha)
        ps, pst = _inproj_sample(xs, wb["w_in"], l)
        y_a, y_d, new_pool, vn = _mix_sample(ps, pool_t[l], lp, past_len)
        new_state, y_bt = _hgrn_sample(pst, hgrn_t[l], lp)
        y_ct = _attn_sample(pst, cache_kt, cache_vt, page_table, l, lp)
        xs = _outproj_sample(xs, y_a, y_bt, y_ct, y_d, wb["w_out"], l, g[1], b[1], alpha)
        xs = _ffn(xs, wb["ffn2_w_gu"], wb["ffn2_w_dn"], l, g[2], b[2], alpha)
        outs["k_s"].append(jnp.transpose(pst[6 * w:7 * w].reshape(N_HEADS, D_HEAD, ns), (2, 0, 1))[:, None])
        outs["v_s"].append(jnp.transpose(pst[7 * w:8 * w].reshape(N_HEADS, D_HEAD, ns), (2, 0, 1))[:, None])
        outs["pool_s"].append(jnp.transpose(new_pool, (1, 0, 2)))
        outs["hgrn_s"].append(jnp.transpose(new_state, (3, 0, 1, 2)))
        outs["sgv_s"].append(vn[:, None])

    st = {k: jnp.stack(v, axis=0) for k, v in outs.items()}
    return (xp.reshape(nb, t, d), xs.reshape(ns, 1, d), st["pool_p"], st["pool_s"], st["hgrn_p"], st["hgrn_s"],
            st["k_p"], st["k_s"], st["v_p"], st["v_s"], st["sgv_p"], st["sgv_s"])
```

```python
import functools
import math

import jax
import jax.numpy as jnp
from jax import lax
from jax.experimental import pallas as pl
from jax.experimental.pallas import tpu as pltpu

F32 = jnp.float32
BF16 = jnp.bfloat16

W_GROUP = 256
N_HEADS = 4
D_HEAD = 64
DA_HALF = 32
POOL_WINDOWS = (2, 4, 8, 16)
POOL_BUF = 15
PAGE = 128
SG_CHUNK = 128
HG_SUB = 16
V_AUG = 80
REL_BUCKETS = 32
REL_MAX_DIST = 128
EPS = 1e-5
NEG = -1e30
LOG2E = 1.0 / math.log(2.0)

VMEM_LIMIT = 56 * 1024 * 1024
MXU_DIM = 256
ROW_TILE = 512
FFN_CHUNKS = 2
ATT_TQ = 1024
ATT_TK = 512
DEC_SAMPLES_PER_STEP = 2
DEC_SLOTS = 3

NT_DIMS = (((1,), (1,)), ((), ()))


def _params(*sem):
    return pltpu.CompilerParams(dimension_semantics=sem, vmem_limit_bytes=VMEM_LIMIT)


def _const_spec(shape, single=False):
    nd = len(shape)
    kw = {"pipeline_mode": pl.Buffered(1)} if single else {}
    return pl.BlockSpec(shape, lambda *_: (0,) * nd, **kw)


def _layer_spec(shape, layer, single=False, rows=None, row_block=0):
    kw = {"pipeline_mode": pl.Buffered(1)} if single else {}
    block = (None, shape[1] if rows is None else rows) + tuple(shape[2:])
    return pl.BlockSpec(block, lambda *_: (layer, row_block) + (0,) * (len(shape) - 2), **kw)


def _layer_norm(y, g, b):
    mu = jnp.mean(y, axis=-1, keepdims=True)
    yc = y - mu
    var = jnp.mean(yc * yc, axis=-1, keepdims=True)
    return yc * lax.rsqrt(var + EPS) * g + b


def _silu(x):
    return x * jax.nn.sigmoid(x)


def _log_forget(z, log_lb, log1m_lb):
    log_sig = jnp.minimum(z, 0.0) - jnp.log(1.0 + jnp.exp(-jnp.abs(z)))
    b = log1m_lb + log_sig
    return jnp.maximum(log_lb, b) + jnp.log(1.0 + jnp.exp(-jnp.abs(log_lb - b)))


def _swiglu_ln(x, wgu_ref, wdn_ref, g, b, d_ff, bounds, alpha):
    xb = x.astype(BF16)
    acc = None
    for lo, hi in zip(bounds[:-1], bounds[1:]):
        gate = jnp.dot(xb, wgu_ref[:, lo:hi], preferred_element_type=F32)
        up = jnp.dot(xb, wgu_ref[:, d_ff + lo:d_ff + hi], preferred_element_type=F32)
        h = (_silu(gate) * up).astype(BF16)
        part = jnp.dot(h, wdn_ref[lo:hi, :], preferred_element_type=F32)
        acc = part if acc is None else acc + part
    return _layer_norm(alpha * x + 0.5 * acc, g, b)


def _ffn_kernel(x_ref, wgu_ref, wdn_ref, g_ref, b_ref, o_ref, *, d_ff, bounds, alpha):
    o_ref[...] = _swiglu_ln(x_ref[...], wgu_ref, wdn_ref, g_ref[...], b_ref[...], d_ff, bounds, alpha)


def _ffn_bounds(d_ff):
    n_tiles = d_ff // MXU_DIM
    cuts = [round(n_tiles * c / FFN_CHUNKS) * MXU_DIM for c in range(FFN_CHUNKS)]
    return tuple(cuts) + (d_ff,)


def _ffn(x, w_gu, w_dn, layer, ln_g, ln_b, ln_i, alpha):
    r, d = x.shape
    d_ff = w_dn.shape[1]
    tm = min(ROW_TILE, r)
    bounds = _ffn_bounds(d_ff)
    return pl.pallas_call(
        functools.partial(_ffn_kernel, d_ff=d_ff, bounds=bounds, alpha=alpha),
        grid=(r // tm,),
        in_specs=[
            pl.BlockSpec((tm, d), lambda i: (i, 0)),
            _layer_spec(w_gu.shape, layer, single=True),
            _layer_spec(w_dn.shape, layer, single=True),
            _layer_spec(ln_g.shape, 3 * layer + ln_i),
            _layer_spec(ln_b.shape, 3 * layer + ln_i),
        ],
        out_specs=pl.BlockSpec((tm, d), lambda i: (i, 0)),
        out_shape=jax.ShapeDtypeStruct((r, d), F32),
        compiler_params=_params("arbitrary"),
        name="ffn",
    )(x, w_gu, w_dn, ln_g, ln_b)


def _ffn_inproj_p_kernel(x_ref, wgu_ref, wdn_ref, g_ref, b_ref, w_ref,
                         x1_ref, pa_ref, pd_ref, k_ref, qt_ref, kt_ref, vt_ref, *, d_ff, bounds, alpha, q_scale):
    w = W_GROUP
    x1 = _swiglu_ln(x_ref[...], wgu_ref, wdn_ref, g_ref[...], b_ref[...], d_ff, bounds, alpha)
    x1_ref[...] = x1
    xb = x1.astype(BF16)
    pa_ref[...] = jnp.dot(xb, w_ref[:, 0:5 * w], preferred_element_type=F32)
    pd_ref[...] = jnp.dot(xb, w_ref[:, 8 * w:10 * w], preferred_element_type=F32)
    qkv = jnp.dot(xb, w_ref[:, 5 * w:8 * w], preferred_element_type=F32)
    for h in range(N_HEADS):
        k_ref[h] = qkv[:, w + h * D_HEAD:w + (h + 1) * D_HEAD].astype(BF16)
    qkvt = qkv.T
    qt_ref[0] = (qkvt[0:w] * q_scale).astype(BF16)
    kt_ref[0] = qkvt[w:2 * w]
    vt_ref[0] = qkvt[2 * w:3 * w]


def _ffn_inproj_prompt(x, w_gu, w_dn, ln_g, ln_b, w_in, layer, n_batch, alpha):
    r, d = x.shape
    t = r // n_batch
    w = W_GROUP
    tm = min(ROW_TILE, t)
    tpb = t // tm
    d_ff = w_dn.shape[1]
    chan_major = lambda: pl.BlockSpec((1, w, tm), lambda i: (i // tpb, 0, i % tpb))
    return pl.pallas_call(
        functools.partial(_ffn_inproj_p_kernel, d_ff=d_ff, bounds=_ffn_bounds(d_ff), alpha=alpha,
                          q_scale=DA_HALF ** -0.5 * LOG2E),
        grid=(r // tm,),
        in_specs=[
            pl.BlockSpec((tm, d), lambda i: (i, 0)),
            _layer_spec(w_gu.shape, layer, single=True),
            _layer_spec(w_dn.shape, layer, single=True),
            _layer_spec(ln_g.shape, 3 * layer), _layer_spec(ln_b.shape, 3 * layer),
            _layer_spec(w_in.shape, layer, single=True),
        ],
        out_specs=[
            pl.BlockSpec((tm, d), lambda i: (i, 0)),
            pl.BlockSpec((tm, 5 * w), lambda i: (i, 0)),
            pl.BlockSpec((tm, 2 * w), lambda i: (i, 0)),
            pl.BlockSpec((N_HEADS, tm, D_HEAD), lambda i: (0, i, 0)),
            chan_major(), chan_major(), chan_major(),
        ],
        out_shape=[
            jax.ShapeDtypeStruct((r, d), F32),
            jax.ShapeDtypeStruct((r, 5 * w), F32),
            jax.ShapeDtypeStruct((r, 2 * w), F32),
            jax.ShapeDtypeStruct((N_HEADS, r, D_HEAD), BF16),
            jax.ShapeDtypeStruct((n_batch, w, t), BF16),
            jax.ShapeDtypeStruct((n_batch, w, t), F32),
            jax.ShapeDtypeStruct((n_batch, w, t), F32),
        ],
        compiler_params=_params("arbitrary"),
        name="ffn_inproj_prompt",
    )(x, w_gu, w_dn, ln_g, ln_b, w_in)


def _inproj_s_kernel(x_ref, w_ref, p_ref, pt_ref):
    p = jnp.dot(x_ref[...].astype(BF16), w_ref[...], preferred_element_type=F32)
    p_ref[...] = p
    pt_ref[...] = p.T


def _inproj_sample(x, w_in, layer):
    n, d = x.shape
    d_in = w_in.shape[2]
    return pl.pallas_call(
        _inproj_s_kernel,
        grid=(1,),
        in_specs=[_const_spec((n, d)), _layer_spec(w_in.shape, layer)],
        out_specs=[_const_spec((n, d_in)), _const_spec((d_in, n))],
        out_shape=[jax.ShapeDtypeStruct((n, d_in), F32), jax.ShapeDtypeStruct((d_in, n), F32)],
        compiler_params=_params("arbitrary"),
        name="inproj_sample",
    )(x, w_in)


def _group_select(grp, parts):
    out = parts[N_HEADS - 1]
    for g in range(N_HEADS - 2, -1, -1):
        out = jnp.where(grp == g, parts[g], out)
    return out


def _mix_p_kernel(pa_ref, pd_ref, poolw_ref, pscale_ref, loglb_ref, log1mlb_ref, omlb_ref, hgg_ref,
                  sgg_ref, sgb_ref, sgw_ref, sgbias_ref, eones_ref,
                  y_ref, pool_ref, st_out_ref, sgv_ref,
                  prev_ref, st_ref):
    c = SG_CHUNK
    w = W_GROUP
    t = pl.program_id(1)

    @pl.when(t == 0)
    def _():
        prev_ref[...] = jnp.zeros_like(prev_ref)
        st_ref[...] = jnp.zeros_like(st_ref)

    a = pa_ref[:, 0:w]
    hq = pa_ref[:, w:2 * w]
    hf = pa_ref[:, 2 * w:3 * w]
    hi = pa_ref[:, 3 * w:4 * w]
    hg = pa_ref[:, 4 * w:5 * w]
    su = pd_ref[:, 0:w]
    sv = pd_ref[:, w:2 * w]
    lane = lax.broadcasted_iota(jnp.int32, (c, w), 1)
    row = lax.broadcasted_iota(jnp.int32, (c, w), 0)
    grp = lane >> 6

    e = jnp.concatenate([prev_ref[...], a], axis=0)
    s2 = e + pltpu.roll(e, 1, 0)
    s4 = s2 + pltpu.roll(s2, 2, 0)
    s8 = s4 + pltpu.roll(s4, 4, 0)
    s16 = s8 + pltpu.roll(s8, 8, 0)
    wsum = _group_select(grp, [s2[16:], s4[16:], s8[16:], s16[16:]])
    win = _group_select(grp, [jnp.full((c, w), v, jnp.int32) for v in POOL_WINDOWS])
    cnt = jnp.minimum(t * c + row + 1, win).astype(F32)
    dpool = wsum / cnt - a
    ya = jnp.dot(dpool.astype(BF16), poolw_ref[...], preferred_element_type=F32) * pscale_ref[...]
    prev_ref[...] = a[c - 16:]
    pool_ref[0] = a[c - 16:]

    vn = _layer_norm(sv, sgg_ref[...], sgb_ref[...])
    sg = jnp.dot(sgw_ref[...], vn.astype(BF16), preferred_element_type=F32)
    s_gate = _group_select(grp, [sg[g * c:(g + 1) * c] for g in range(N_HEADS)]) + sgbias_ref[...]
    yd = su * s_gate
    sgv_ref[0] = vn

    n_sub = c // HG_SUB
    logf = _log_forget(hf, loglb_ref[...], log1mlb_ref[...])
    kin = omlb_ref[...] * jax.nn.sigmoid(-hf)
    r16 = row & (HG_SUB - 1)
    bl = logf
    rv = logf
    for sh in (1, 2, 4, 8):
        bl = bl + jnp.where(r16 >= sh, pltpu.roll(bl, sh, 0), 0.0)
        rv = rv + jnp.where(r16 + sh < HG_SUB, pltpu.roll(rv, c - sh, 0), 0.0)
    sub_row = row >> 4
    qtb = (hq * jnp.exp(bl)).astype(BF16)
    kt = kin * jnp.exp(rv - logf)
    dec = jnp.exp(rv)
    vtb = hi.T.astype(BF16)
    bi0 = lax.broadcasted_iota(jnp.int32, (w, w), 0) >> 6
    bi1 = lax.broadcasted_iota(jnp.int32, (w, w), 1) >> 6
    blockmask = bi0 == bi1
    st = st_ref[...]
    seen = []
    for i in range(n_sub):
        seen.append(st.astype(BF16))
        km = jnp.where(sub_row == i, kt, 0.0).astype(BF16)
        u = jnp.dot(vtb, km, preferred_element_type=F32)
        st = dec[i * HG_SUB:i * HG_SUB + 1] * st + jnp.where(blockmask, u, 0.0)
    st_ref[...] = st
    st_out_ref[0] = st
    qx = jnp.concatenate([jnp.where(sub_row == i, qtb, jnp.zeros_like(qtb)) for i in range(n_sub)], axis=1)
    o = lax.dot_general(qx, jnp.concatenate(seen, axis=1), NT_DIMS,
                        preferred_element_type=F32)

    rt = {lo: lax.broadcasted_iota(jnp.int32, (HG_SUB - lo, w), 0) + lo for lo in (0, 8)}
    bl2 = bl * LOG2E
    xs = []
    for i in range(n_sub):
        rows = slice(i * HG_SUB, (i + 1) * HG_SUB)
        bli, qi, ki = bl2[rows], hq[rows], kin[rows]
        for s in range(HG_SUB):
            lo = (s // 8) * 8
            ratio = jnp.exp2(jnp.where(rt[lo] >= s, bli[lo:] - bli[s:s + 1], NEG))
            x_s = ratio * qi[lo:] * ki[s:s + 1]
            if lo:
                x_s = jnp.concatenate([jnp.zeros((lo, w), F32), x_s], axis=0)
            xs.append(x_s.astype(BF16))
    x = jnp.concatenate(xs, axis=0)
    r = jnp.dot(x, eones_ref[...], preferred_element_type=F32)
    o_intra = []
    for i in range(n_sub):
        vi = hi[i * HG_SUB:(i + 1) * HG_SUB]
        base = i * HG_SUB * HG_SUB
        oi = r[base:base + HG_SUB] * vi[0:1]
        for s in range(1, HG_SUB):
            oi = oi + r[base + s * HG_SUB:base + (s + 1) * HG_SUB] * vi[s:s + 1]
        o_intra.append(oi)
    o = o + jnp.concatenate(o_intra, axis=0)
    ms = [jnp.mean(jnp.square(o[:, g * D_HEAD:(g + 1) * D_HEAD]), axis=-1, keepdims=True)
          for g in range(N_HEADS)]
    yb = o * lax.rsqrt(_group_select(grp, ms) + EPS) * hgg_ref[...] * _silu(hg)

    y_ref[:, 0:w] = ya.astype(BF16)
    y_ref[:, w:2 * w] = yb.astype(BF16)
    y_ref[:, 2 * w:3 * w] = yd.astype(BF16)


def _mix_prompt(p_a, p_d, n_batch, lp, layer):
    r = p_a.shape[0]
    t = r // n_batch
    c = SG_CHUNK
    nt = t // c
    w = W_GROUP
    per_layer = lambda name: _layer_spec(lp[name].shape, layer)
    return pl.pallas_call(
        _mix_p_kernel,
        grid=(n_batch, nt),
        in_specs=[
            pl.BlockSpec((c, 5 * w), lambda b, i: (b * nt + i, 0)),
            pl.BlockSpec((c, 2 * w), lambda b, i: (b * nt + i, 0)),
            per_layer("pool_w"), per_layer("pool_scale"), per_layer("log_lb"), per_layer("log1m_lb"),
            per_layer("om_lb"), per_layer("hg_g"), per_layer("sg_g"), per_layer("sg_b"),
            per_layer("sg_w"), per_layer("sg_bias"), _const_spec((w, w)),
        ],
        out_specs=[
            pl.BlockSpec((c, 3 * w), lambda b, i: (b * nt + i, 0)),
            pl.BlockSpec((1, 16, w), lambda b, i: (b, 0, 0)),
            pl.BlockSpec((1, w, w), lambda b, i: (b, 0, 0)),
            pl.BlockSpec((1, c, w), lambda b, i: (b, 0, 0)),
        ],
        out_shape=[
            jax.ShapeDtypeStruct((r, 3 * w), BF16),
            jax.ShapeDtypeStruct((n_batch, 16, w), F32),
            jax.ShapeDtypeStruct((n_batch, w, w), F32),
            jax.ShapeDtypeStruct((n_batch, c, w), F32),
        ],
        scratch_shapes=[pltpu.VMEM((16, w), F32), pltpu.VMEM((w, w), F32)],
        compiler_params=_params("arbitrary", "arbitrary"),
        name="mix_prompt",
    )(p_a, p_d, lp["pool_w"], lp["pool_scale"], lp["log_lb"], lp["log1m_lb"], lp["om_lb"], lp["hg_g"],
      lp["sg_g"], lp["sg_b"], lp["sg_w"], lp["sg_bias"], lp["eones"])


def _attn_p_kernel(cfar_ref, lam_ref, qt_ref, k_ref, vt_ref, near_ref, subg_ref, o_ref,
                   va_ref, m_ref, acc_ref, s_ref, mb_ref, *, tq, tk, layer):
    h = pl.program_id(1)
    qi = pl.program_id(2)
    dh = D_HEAD

    @pl.when(qi == 0)
    def _():
        va_ref[0:dh, :] = vt_ref[0].astype(BF16)
        rows = lax.broadcasted_iota(jnp.int32, (V_AUG - dh, va_ref.shape[1]), 0)
        va_ref[dh:V_AUG, :] = jnp.where(rows == 0, 1.0, 0.0).astype(BF16)

    qt = qt_ref[0]
    sub = lax.broadcasted_iota(jnp.int32, qt.shape, 0)
    zero = jnp.zeros_like(qt)
    qm = (jnp.where(sub < DA_HALF, qt, zero), jnp.where(sub >= DA_HALF, qt, zero))
    m_ref[...] = jnp.full_like(m_ref, NEG)
    acc_ref[...] = jnp.zeros_like(acc_ref)

    def qk(ki, slot, near=None, q0=0):
        k0 = pl.multiple_of(ki * tk, tk)
        kblk = k_ref[0, pl.ds(k0, tk), :]
        for mi in range(2):
            s = jnp.dot(kblk, qm[mi][:, q0:], preferred_element_type=F32)
            if near is not None:
                s = s + near_ref[0, near, :, q0:]
            s_ref[slot, mi, :, q0:] = s
            mb_ref[slot, mi, :, q0:] = jnp.max(s, axis=0, keepdims=True)

    def softmax_pv(ki, slot, shift, q0=0):
        k0 = pl.multiple_of(ki * tk, tk)
        vblk = va_ref[:, pl.ds(k0, tk)]
        for mi in range(2):
            s = s_ref[slot, mi, :, q0:]
            m_blk = mb_ref[slot, mi, :, q0:]
            if shift is not None:
                m_blk = m_blk + shift
            m_old = m_ref[mi, :, q0:]
            m_new = jnp.maximum(m_old, m_blk)
            alpha = jnp.exp2(m_old - m_new)
            p = jnp.exp2(s - (m_new if shift is None else m_new - shift)).astype(BF16)
            acc_ref[mi, :, q0:] = alpha * acc_ref[mi, :, q0:] + jnp.dot(vblk, p, preferred_element_type=F32)
            m_ref[mi, :, q0:] = m_new

    cfar = cfar_ref[h]
    n_diag = tq // tk
    kb0 = qi * n_diag

    def near_tiles(first_tile, first_near, n, slot0):
        q0s = [max(first_near + j - 1, 0) * tk for j in range(n)]
        for j in range(n):
            if j + 1 < n:
                qk(first_tile + j + 1, (slot0 + j + 1) % 2, first_near + j + 1, q0s[j + 1])
            softmax_pv(first_tile + j, (slot0 + j) % 2, None, q0s[j])

    def far_pair(pi, carry):
        b = 2 * pi
        qk(b + 1, 1)
        softmax_pv(b, 0, cfar)
        qk(b + 2, 0)
        softmax_pv(b + 1, 1, cfar)
        return carry

    @pl.when(qi == 0)
    def _():
        qk(0, 0, 1)
        near_tiles(0, 1, n_diag, 0)

    @pl.when(qi >= 1)
    def _():
        n_far = kb0 - 1
        qk(0, 0)
        lax.fori_loop(0, (n_far - 1) // 2, far_pair, 0)

        def tail(rem):
            if rem == 2:
                qk(n_far - 1, 1)
                softmax_pv(n_far - 2, 0, cfar)
            last_slot = rem % 2
            qk(n_far, last_slot, 0)
            softmax_pv(n_far - 1, 1 - last_slot, cfar)
            near_tiles(n_far, 0, n_diag + 1, last_slot)

        if n_diag % 2 == 0:
            tail(1)
        else:
            pl.when(n_far % 2 == 1)(lambda: tail(1))
            pl.when(n_far % 2 == 0)(lambda: tail(2))

    a0 = acc_ref[0]
    a1 = acc_ref[1]
    o = a0[0:dh] / a0[dh:dh + 1] - lam_ref[layer] * (a1[0:dh] / a1[dh:dh + 1])
    ms = jnp.mean(o * o, axis=0, keepdims=True)
    o_ref[0] = (o * lax.rsqrt(ms + EPS) * subg_ref[...]).astype(BF16)


def _attn_tiles(t):
    tq = min(ATT_TQ, t)
    return tq, min(ATT_TK, tq // 2)


def _attn_prompt(qt, k_rows, vt, lp, layer, n_batch):
    t = qt.shape[2]
    tq, tk = _attn_tiles(t)
    nq = t // tq
    n_near = tq // tk + 1
    smem = pl.BlockSpec(memory_space=pltpu.SMEM)
    return pl.pallas_call(
        functools.partial(_attn_p_kernel, tq=tq, tk=tk, layer=layer),
        grid=(n_batch, N_HEADS, nq),
        in_specs=[
            smem, smem,
            pl.BlockSpec((1, D_HEAD, tq), lambda b, h, i: (b, h, i)),
            pl.BlockSpec((1, t, D_HEAD), lambda b, h, i: (h, b, 0)),
            pl.BlockSpec((1, D_HEAD, t), lambda b, h, i: (b, h, 0)),
            pl.BlockSpec((1, n_near, tk, tq), lambda b, h, i: (h, 0, 0, 0)),
            _layer_spec(lp["sub_g_col"].shape, layer),
        ],
        out_specs=pl.BlockSpec((1, D_HEAD, tq), lambda b, h, i: (b, h, i)),
        out_shape=jax.ShapeDtypeStruct((n_batch, W_GROUP, t), BF16),
        scratch_shapes=[
            pltpu.VMEM((V_AUG, t), BF16),
            pltpu.VMEM((2, 1, tq), F32), pltpu.VMEM((2, V_AUG, tq), F32),
            pltpu.VMEM((2, 2, tk, tq), F32), pltpu.VMEM((2, 2, 1, tq), F32),
        ],
        compiler_params=_params("arbitrary", "arbitrary", "arbitrary"),
        name="attn_prompt",
    )(lp["c_far"], lp["lam"], qt, k_rows, vt, lp["bias_near"], lp["sub_g_col"])


def _outproj_ffn_p_kernel(x_ref, y_ref, yct_ref, wab_ref, wc_ref, wd_ref, g1_ref, b1_ref,
                          wgu_ref, wdn_ref, g2_ref, b2_ref, o_ref, *, d_ff, bounds, alpha):
    w = W_GROUP
    acc = jnp.dot(y_ref[:, 0:2 * w], wab_ref[...], preferred_element_type=F32)
    acc = acc + jnp.dot(y_ref[:, 2 * w:3 * w], wd_ref[...], preferred_element_type=F32)
    yc = yct_ref[0].astype(F32).T.astype(BF16)
    acc = acc + jnp.dot(yc, wc_ref[...], preferred_element_type=F32)
    x2 = _layer_norm(alpha * x_ref[...] + acc, g1_ref[...], b1_ref[...])
    o_ref[...] = _swiglu_ln(x2, wgu_ref, wdn_ref, g2_ref[...], b2_ref[...], d_ff, bounds, alpha)


def _outproj_ffn_prompt(x, y_abd, y_ct, w_out, w_gu, w_dn, layer, ln_g, ln_b, alpha):
    r, d = x.shape
    w = W_GROUP
    t = y_ct.shape[2]
    tm = min(ROW_TILE, t)
    tpb = t // tm
    d_ff = w_dn.shape[1]
    return pl.pallas_call(
        functools.partial(_outproj_ffn_p_kernel, d_ff=d_ff, bounds=_ffn_bounds(d_ff), alpha=alpha),
        grid=(r // tm,),
        in_specs=[
            pl.BlockSpec((tm, d), lambda i: (i, 0)),
            pl.BlockSpec((tm, 3 * W_GROUP), lambda i: (i, 0)),
            pl.BlockSpec((1, W_GROUP, tm), lambda i: (i // tpb, 0, i % tpb)),
            _layer_spec(w_out.shape, layer, rows=2 * w, row_block=0),
            _layer_spec(w_out.shape, layer, rows=w, row_block=2),
            _layer_spec(w_out.shape, layer, rows=w, row_block=3),
            _layer_spec(ln_g.shape, 3 * layer + 1), _layer_spec(ln_b.shape, 3 * layer + 1),
            _layer_spec(w_gu.shape, layer, single=True),
            _layer_spec(w_dn.shape, layer, single=True),
            _layer_spec(ln_g.shape, 3 * layer + 2), _layer_spec(ln_b.shape, 3 * layer + 2),
        ],
        out_specs=pl.BlockSpec((tm, d), lambda i: (i, 0)),
        out_shape=jax.ShapeDtypeStruct((r, d), F32),
        compiler_params=_params("arbitrary"),
        name="outproj_ffn_prompt",
    )(x, y_abd, y_ct, w_out, w_out, w_out, ln_g, ln_b, w_gu, w_dn, ln_g, ln_b)


def _outproj_s_kernel(x_ref, ya_ref, ybt_ref, yct_ref, yd_ref, w_ref, g_ref, b_ref, o_ref, *, alpha):
    w = W_GROUP
    parts = (ya_ref[...], ybt_ref[...].T, yct_ref[...].T, yd_ref[...])
    acc = None
    for j, part in enumerate(parts):
        d = jnp.dot(part.astype(BF16), w_ref[j * w:(j + 1) * w, :], preferred_element_type=F32)
        acc = d if acc is None else acc + d
    o_ref[...] = _layer_norm(alpha * x_ref[...] + acc, g_ref[...], b_ref[...])


def _outproj_sample(x, y_a, y_bt, y_ct, y_d, w_out, layer, ln_g, ln_b, alpha):
    n, d = x.shape
    w = W_GROUP
    return pl.pallas_call(
        functools.partial(_outproj_s_kernel, alpha=alpha),
        grid=(1,),
        in_specs=[
            _const_spec((n, d)), _const_spec((n, w)), _const_spec((w, n)), _const_spec((w, n)),
            _const_spec((n, w)), _layer_spec(w_out.shape, layer),
            _layer_spec(ln_g.shape, 3 * layer + 1), _layer_spec(ln_b.shape, 3 * layer + 1),
        ],
        out_specs=_const_spec((n, d)),
        out_shape=jax.ShapeDtypeStruct((n, d), F32),
        compiler_params=_params("arbitrary"),
        name="outproj_sample",
    )(x, y_a, y_bt, y_ct, y_d, w_out, ln_g, ln_b)


def _mix_s_kernel(pa_ref, pd_ref, pool_ref, poolw_ref, pscale_ref, sgg_ref, sgb_ref, w00_ref, b0_ref,
                  ya_ref, yd_ref, newpool_ref, vn_ref, *, cnts):
    a = pa_ref[...]
    n, w = a.shape
    grp = lax.broadcasted_iota(jnp.int32, (n, w), 1) >> 6
    acc = a
    means = []
    for j in range(1, POOL_BUF + 1):
        acc = acc + pool_ref[POOL_BUF - j]
        if j + 1 in POOL_WINDOWS:
            means.append(acc / cnts[POOL_WINDOWS.index(j + 1)])
    dpool = _group_select(grp, means) - a
    ya_ref[...] = jnp.dot(dpool.astype(BF16), poolw_ref[...], preferred_element_type=F32) * pscale_ref[...]
    for j in range(POOL_BUF - 1):
        newpool_ref[j] = pool_ref[j + 1]
    newpool_ref[POOL_BUF - 1] = a

    vn = _layer_norm(pd_ref[:, w:2 * w], sgg_ref[...], sgb_ref[...])
    vn_ref[...] = vn
    yd_ref[...] = pd_ref[:, 0:w] * (w00_ref[...] * vn + b0_ref[...])


def _mix_sample(p, pool, lp, layer, past_len):
    n = p.shape[0]
    w = W_GROUP
    cnts = tuple(float(min(past_len + 1, win)) for win in POOL_WINDOWS)
    per_layer = lambda name: _layer_spec(lp[name].shape, layer)
    return pl.pallas_call(
        functools.partial(_mix_s_kernel, cnts=cnts),
        grid=(1,),
        in_specs=[
            pl.BlockSpec((n, w), lambda i: (0, 0)),
            pl.BlockSpec((n, 2 * w), lambda i: (0, 4)),
            _layer_spec(pool.shape, layer), per_layer("pool_w"), per_layer("pool_scale"), per_layer("sg_g"),
            per_layer("sg_b"), per_layer("sg_w00"), per_layer("sg_b0"),
        ],
        out_specs=[_const_spec((n, w)), _const_spec((n, w)), _const_spec(pool.shape[1:]), _const_spec((n, w))],
        out_shape=[
            jax.ShapeDtypeStruct((n, w), F32), jax.ShapeDtypeStruct((n, w), F32),
            jax.ShapeDtypeStruct(pool.shape[1:], F32), jax.ShapeDtypeStruct((n, w), F32),
        ],
        compiler_params=_params("arbitrary"),
        name="mix_sample",
    )(p, p, pool, lp["pool_w"], lp["pool_scale"], lp["sg_g"], lp["sg_b"], lp["sg_w00"], lp["sg_b0"])


def _hgrn_s_kernel(q_ref, f_ref, i_ref, g_ref, s_ref, loglb_ref, log1mlb_ref, omlb_ref, hgg_ref,
                   so_ref, yb_ref):
    z = f_ref[...]
    f = jnp.exp(_log_forget(z, loglb_ref[...], log1mlb_ref[...]))
    kin = omlb_ref[...] * jax.nn.sigmoid(-z)
    q = q_ref[...]
    v = i_ref[...]
    o = jnp.zeros_like(v)
    for d in range(D_HEAD):
        sn = f[d:d + 1] * s_ref[0, d] + kin[d:d + 1] * v
        so_ref[0, d] = sn
        o = o + q[d:d + 1] * sn
    ms = jnp.mean(o * o, axis=0, keepdims=True)
    yb_ref[...] = o * lax.rsqrt(ms + EPS) * hgg_ref[...] * _silu(g_ref[...])


def _hgrn_sample(pt, state, lp, layer):
    n = pt.shape[1]
    dh = D_HEAD
    blk = lambda off: pl.BlockSpec((dh, n), lambda h: (off * N_HEADS + h, 0))
    col = lambda: pl.BlockSpec((None, dh, 1), lambda h: (layer, h, 0))
    return pl.pallas_call(
        _hgrn_s_kernel,
        grid=(N_HEADS,),
        in_specs=[
            blk(1), blk(2), blk(3), blk(4),
            pl.BlockSpec((None, 1, dh, dh, n), lambda h: (layer, h, 0, 0, 0)),
            col(), col(), col(), _layer_spec(lp["hg_g_col"].shape, layer),
        ],
        out_specs=[
            pl.BlockSpec((1, dh, dh, n), lambda h: (h, 0, 0, 0)),
            pl.BlockSpec((dh, n), lambda h: (h, 0)),
        ],
        out_shape=[
            jax.ShapeDtypeStruct(state.shape[1:], F32),
            jax.ShapeDtypeStruct((W_GROUP, n), F32),
        ],
        compiler_params=_params("arbitrary"),
        name="hgrn_sample",
    )(pt, pt, pt, pt, state, lp["log_lb_col"], lp["log1m_lb_col"], lp["om_lb_col"], lp["hg_g_col"])


def _attn_s_one(n, k_pages, v_pages, lam, qt_ref, knt_ref, vnt_ref, bias_ref, bself_ref, n_pages, q_scale):
    w, ns = qt_ref.shape
    dh = D_HEAD
    sel = lax.broadcasted_iota(jnp.int32, (w, ns), 1) == n

    def column(ref):
        return jnp.sum(jnp.where(sel, ref[...], 0.0), axis=1, keepdims=True)

    qcol = column(qt_ref) * q_scale
    kcol = column(knt_ref)
    vcol = column(vnt_ref)

    n_grp = 2 * N_HEADS
    rows = ([], [])
    self_rows = ([], [])
    for h in range(N_HEADS):
        qh = qcol[h * dh:(h + 1) * dh]
        for j in range(n_pages):
            prod = k_pages[j][h] * qh
            rows[0].append(jnp.sum(prod[0:DA_HALF], axis=0, keepdims=True))
            rows[1].append(jnp.sum(prod[DA_HALF:dh], axis=0, keepdims=True))
        self_prod = qh * kcol[h * dh:(h + 1) * dh]
        for mi in range(2):
            s_self = (jnp.sum(self_prod[mi * DA_HALF:(mi + 1) * DA_HALF], axis=0, keepdims=True)
                      + bself_ref[h])
            self_rows[mi].append(jnp.broadcast_to(s_self, (n_pages, 1)))
    s = jnp.concatenate(rows[0] + rows[1], axis=0) + bias_ref[...]
    s_self = jnp.concatenate(self_rows[0] + self_rows[1], axis=0)

    def per_group(col, reduce):
        parts = [jnp.broadcast_to(reduce(col[g * n_pages:(g + 1) * n_pages], axis=0, keepdims=True),
                                  (n_pages, 1)) for g in range(n_grp)]
        return jnp.concatenate(parts, axis=0)

    m = jnp.maximum(per_group(jnp.max(s, axis=1, keepdims=True), jnp.max), s_self)
    p = jnp.exp(s - m)
    p_self = jnp.exp(s_self - m)
    inv_l = 1.0 / (per_group(jnp.sum(p, axis=1, keepdims=True), jnp.sum) + p_self)
    half = N_HEADS * n_pages
    a = (p * inv_l)[0:half] - lam * (p * inv_l)[half:2 * half]
    a_self = (p_self * inv_l)[0:half] - lam * (p_self * inv_l)[half:2 * half]
    weighted = []
    for h in range(N_HEADS):
        oh = v_pages[0][h] * a[h * n_pages:h * n_pages + 1]
        for j in range(1, n_pages):
            oh = oh + v_pages[j][h] * a[h * n_pages + j:h * n_pages + j + 1]
        weighted.append(oh)
    a_self_col = jnp.concatenate(
        [jnp.broadcast_to(a_self[h * n_pages:h * n_pages + 1], (dh, 1)) for h in range(N_HEADS)], axis=0)
    ocol = jnp.sum(jnp.concatenate(weighted, axis=0), axis=1, keepdims=True) + a_self_col * vcol
    return sel, ocol


def _attn_s_kernel(pt_ref, lam_ref, qt_ref, knt_ref, vnt_ref, bias_ref, bself_ref, subg_ref, ck_ref, cv_ref,
                   o_ref, ot_ref, kbuf_ref, vbuf_ref, sem_ref, *, n_pages, sps, layer, q_scale):
    dh = D_HEAD
    step = pl.program_id(0)
    n_steps = pl.num_programs(0)

    def page_copies(s, slot):
        copies = []
        for u in range(sps):
            for j in range(n_pages):
                page = pt_ref[s * sps + u, j]
                copies.append(pltpu.make_async_copy(ck_ref.at[layer, page], kbuf_ref.at[slot, u * n_pages + j],
                                                    sem_ref.at[0, slot]))
                copies.append(pltpu.make_async_copy(cv_ref.at[layer, page], vbuf_ref.at[slot, u * n_pages + j],
                                                    sem_ref.at[1, slot]))
        return copies

    @pl.when(step == 0)
    def _():
        for s in range(DEC_SLOTS - 1):
            @pl.when(s < n_steps)
            def _():
                for c in page_copies(s, s):
                    c.start()

    slot = step % DEC_SLOTS
    for c in page_copies(step, slot):
        c.wait()

    ot = ot_ref[...]
    for u in range(sps):
        k_pages = [kbuf_ref.at[slot, u * n_pages + j] for j in range(n_pages)]
        v_pages = [vbuf_ref.at[slot, u * n_pages + j] for j in range(n_pages)]
        sel, ocol = _attn_s_one(step * sps + u, k_pages, v_pages, lam_ref[layer],
                                qt_ref, knt_ref, vnt_ref, bias_ref, bself_ref, n_pages, q_scale)
        ot = jnp.where(sel, ocol, ot)
    ot_ref[...] = ot

    ahead = step + DEC_SLOTS - 1

    @pl.when(ahead < n_steps)
    def _():
        for c in page_copies(ahead, ahead % DEC_SLOTS):
            c.start()

    @pl.when(step == n_steps - 1)
    def _():
        for h in range(N_HEADS):
            oh = ot_ref[h * dh:(h + 1) * dh, :]
            ms = jnp.mean(oh * oh, axis=0, keepdims=True)
            o_ref[h * dh:(h + 1) * dh, :] = oh * lax.rsqrt(ms + EPS) * subg_ref[...]


def _attn_sample(pt, cache_kt, cache_vt, page_table, layer, lp):
    n = pt.shape[1]
    n_pages = page_table.shape[1]
    w = W_GROUP
    dh = D_HEAD
    blk = lambda off: pl.BlockSpec((w, n), lambda i, tbl: (off, 0))
    sps = DEC_SAMPLES_PER_STEP
    ring = (DEC_SLOTS, sps * n_pages, N_HEADS, dh, PAGE)
    grid_spec = pltpu.PrefetchScalarGridSpec(
        num_scalar_prefetch=1,
        grid=(n // sps,),
        in_specs=[
            pl.BlockSpec(memory_space=pltpu.SMEM),
            blk(5), blk(6), blk(7),
            pl.BlockSpec((2 * N_HEADS * n_pages, PAGE), lambda i, tbl: (0, 0)),
            pl.BlockSpec((N_HEADS, 1, 1), lambda i, tbl: (0, 0, 0)),
            pl.BlockSpec((None, dh, 1), lambda i, tbl: (layer, 0, 0)),
            pl.BlockSpec(memory_space=pl.ANY), pl.BlockSpec(memory_space=pl.ANY),
        ],
        out_specs=pl.BlockSpec((w, n), lambda i, tbl: (0, 0)),
        scratch_shapes=[
            pltpu.VMEM((w, n), F32), pltpu.VMEM(ring, F32), pltpu.VMEM(ring, F32),
            pltpu.SemaphoreType.DMA((2, DEC_SLOTS)),
        ],
    )
    return pl.pallas_call(
        functools.partial(_attn_s_kernel, n_pages=n_pages, sps=sps, layer=layer, q_scale=DA_HALF ** -0.5),
        grid_spec=grid_spec,
        out_shape=jax.ShapeDtypeStruct((w, n), F32),
        compiler_params=_params("arbitrary"),
        name="attn_sample",
    )(page_table, lp["lam"], pt, pt, pt, lp["bias_past"], lp["bias_self"], lp["sub_g_col"],
      cache_kt, cache_vt)


def _rel_bucket(dist):
    n = jnp.maximum(dist, 0)
    max_exact = REL_BUCKETS // 2
    large = max_exact + (jnp.log(jnp.maximum(n, 1).astype(F32) / max_exact)
                         / math.log(REL_MAX_DIST / max_exact) * (REL_BUCKETS - max_exact)).astype(jnp.int32)
    large = jnp.minimum(large, REL_BUCKETS - 1)
    return jnp.where(n < max_exact, n, large)


def _all_params(depth, tq, tk, past_len, n_pages, prm, lb_all):
    lam_init = jnp.asarray([0.8 - 0.6 * math.exp(-0.3 * l) for l in range(depth)], F32)
    lam = (jnp.exp(jnp.sum(prm["diff_lam_q1"] * prm["diff_lam_k1"], axis=-1))
           - jnp.exp(jnp.sum(prm["diff_lam_q2"] * prm["diff_lam_k2"], axis=-1)) + lam_init)
    rel_bias = prm["rel_bias"]

    def bias_of(dist):
        onehot = _rel_bucket(dist)[..., None] == jnp.arange(REL_BUCKETS, dtype=jnp.int32)
        table = rel_bias.T.reshape((N_HEADS,) + (1,) * dist.ndim + (REL_BUCKETS,))
        return jnp.sum(jnp.where(onehot[None], table, 0.0), axis=-1)

    blk = REL_MAX_DIST
    k0 = jnp.arange(blk, dtype=jnp.int32)[:, None]
    q0 = jnp.arange(blk, dtype=jnp.int32)[None, :]
    blocks = {}

    def block(e):
        e = max(min(e, 2), -1)
        if e not in blocks:
            dist = blk * e + q0 - k0
            blocks[e] = jnp.where((dist >= 0)[None], bias_of(dist), NEG) * LOG2E
        return blocks[e]

    tiles = []
    for r in range(-1, tq // tk):
        rows = [jnp.concatenate([block(q1 - k1 - r * (tk // blk)) for q1 in range(tq // blk)], axis=-1)
                for k1 in range(tk // blk)]
        tiles.append(jnp.concatenate(rows, axis=-2))
    bias_near = jnp.stack(tiles, axis=1)
    kpos = jnp.arange(n_pages * PAGE, dtype=jnp.int32)
    bias_past = jnp.tile(bias_of(past_len - kpos).reshape(N_HEADS * n_pages, PAGE), (2, 1))
    tril = jnp.tril(jnp.ones((SG_CHUNK, SG_CHUNK), F32))
    sub_g = prm["diff_subln_g"] * (1.0 - lam_init)[:, None]
    eye = jnp.eye(N_HEADS, dtype=F32)
    pool_w = jnp.einsum("gh,lgce->lgche", eye, prm["pool_w"]).reshape(depth, W_GROUP, W_GROUP)
    row = lambda a: a[:, None, :]
    col = lambda a: a[:, :, None]
    d_model = prm["ln_g"].shape[-1]
    return {
        "ln_g": prm["ln_g"].reshape(depth * 3, 1, d_model), "ln_b": prm["ln_b"].reshape(depth * 3, 1, d_model),
        "pool_w": pool_w.astype(BF16), "pool_scale": row(prm["pool_scale"]),
        "log_lb": row(jnp.log(lb_all)), "log1m_lb": row(jnp.log1p(-lb_all)), "om_lb": row(1.0 - lb_all),
        "log_lb_col": col(jnp.log(lb_all)), "log1m_lb_col": col(jnp.log1p(-lb_all)), "om_lb_col": col(1.0 - lb_all),
        "hg_g": row(jnp.tile(prm["hgrn_norm_g"], (1, N_HEADS))), "hg_g_col": col(prm["hgrn_norm_g"]),
        "sg_g": row(prm["sgu_ln_g"]), "sg_b": row(prm["sgu_ln_b"]),
        "sg_w": (prm["sgu_w"] * tril).reshape(depth, N_HEADS * SG_CHUNK, SG_CHUNK).astype(BF16),
        "sg_bias": jnp.repeat(jnp.swapaxes(prm["sgu_b"], 1, 2), D_HEAD, axis=2),
        "sg_w00": row(jnp.repeat(prm["sgu_w"][:, :, 0, 0], D_HEAD, axis=1)),
        "sg_b0": row(jnp.repeat(prm["sgu_b"][:, :, 0], D_HEAD, axis=1)),
        "eones": jnp.kron(eye, jnp.ones((D_HEAD, D_HEAD), F32)).astype(BF16),
        "lam": lam, "c_far": rel_bias[REL_BUCKETS - 1] * LOG2E,
        "bias_near": bias_near,
        "bias_past": bias_past, "bias_self": rel_bias[0].reshape(N_HEADS, 1, 1),
        "sub_g_col": col(sub_g),
    }


def kernel(x_prompt, x_sample, state_pool, state_hgrn, cache_k, cache_v, page_table, rel_bias, ln_g, ln_b,
           ffn1_w_gu, ffn1_w_dn, ffn2_w_gu, ffn2_w_dn, w_in, w_out, pool_w, pool_scale, hgrn_lb,
           hgrn_norm_g, diff_lam_q1, diff_lam_k1, diff_lam_q2, diff_lam_k2, diff_subln_g, sgu_ln_g,
           sgu_ln_b, sgu_w, sgu_b):
    prm = dict(rel_bias=rel_bias, ln_g=ln_g, ln_b=ln_b, ffn1_w_gu=ffn1_w_gu, ffn1_w_dn=ffn1_w_dn,
               ffn2_w_gu=ffn2_w_gu, ffn2_w_dn=ffn2_w_dn, w_in=w_in, w_out=w_out, pool_w=pool_w,
               pool_scale=pool_scale, hgrn_norm_g=hgrn_norm_g, diff_lam_q1=diff_lam_q1,
               diff_lam_k1=diff_lam_k1, diff_lam_q2=diff_lam_q2, diff_lam_k2=diff_lam_k2,
               diff_subln_g=diff_subln_g, sgu_ln_g=sgu_ln_g, sgu_ln_b=sgu_ln_b, sgu_w=sgu_w, sgu_b=sgu_b)
    depth = w_in.shape[0]
    nb, t, d = x_prompt.shape
    ns = x_sample.shape[0]
    n_pages = page_table.shape[1]
    past_len = n_pages * PAGE
    alpha = (2.0 * depth) ** 0.25
    w = W_GROUP
    tq, tk = _attn_tiles(t)
    assert x_sample.shape[1] == 1 and t % SG_CHUNK == 0 and t % tq == 0 and tk >= REL_MAX_DIST

    lb_cum = jnp.cumsum(jax.nn.softmax(hgrn_lb.astype(F32), axis=0), axis=0)
    lb_all = jnp.maximum(lb_cum - lb_cum[:1], 0.0)

    cache_kt = jnp.transpose(cache_k, (0, 1, 3, 4, 2))
    cache_vt = jnp.transpose(cache_v, (0, 1, 3, 4, 2))
    hgrn_t = jnp.transpose(state_hgrn, (0, 2, 3, 4, 1))
    pool_t = jnp.transpose(state_pool, (0, 2, 1, 3))

    wb = {k: prm[k].astype(BF16) for k in ("ffn1_w_gu", "ffn1_w_dn", "ffn2_w_gu", "ffn2_w_dn", "w_in", "w_out")}

    xp = x_prompt.reshape(nb * t, d)
    xs = x_sample.reshape(ns, d)
    lp = _all_params(depth, tq, tk, past_len, n_pages, prm, lb_all)
    ln_g, ln_b = lp["ln_g"], lp["ln_b"]
    outs = {k: [] for k in ("pool_p", "pool_s", "hgrn_p", "hgrn_s", "k_p", "k_s", "v_p", "v_s", "sgv_p", "sgv_s")}
    for l in range(depth):
        xp, p_a, p_d, k_rows, q_t, k_t, v_t = _ffn_inproj_prompt(
            xp, wb["ffn1_w_gu"], wb["ffn1_w_dn"], ln_g, ln_b, wb["w_in"], l, nb, alpha)
        y_abd, pool16, st, sgv = _mix_prompt(p_a, p_d, nb, lp, l)
        y_ct = _attn_prompt(q_t, k_rows, v_t, lp, l, nb)
        xp = _outproj_ffn_prompt(xp, y_abd, y_ct, wb["w_out"], wb["ffn2_w_gu"], wb["ffn2_w_dn"], l,
                                 ln_g, ln_b, alpha)
        outs["k_p"].append(jnp.transpose(k_t.reshape(nb, N_HEADS, D_HEAD, t), (0, 3, 1, 2)))
        outs["v_p"].append(jnp.transpose(v_t.reshape(nb, N_HEADS, D_HEAD, t), (0, 3, 1, 2)))
        outs["pool_p"].append(pool16[:, 1:])
        outs["hgrn_p"].append(jnp.stack(
            [jnp.swapaxes(st[:, h * D_HEAD:(h + 1) * D_HEAD, h * D_HEAD:(h + 1) * D_HEAD], 1, 2)
             for h in range(N_HEADS)], axis=1))
        outs["sgv_p"].append(sgv)

        xs = _ffn(xs, wb["ffn1_w_gu"], wb["ffn1_w_dn"], l, ln_g, ln_b, 0, alpha)
        ps, pst = _inproj_sample(xs, wb["w_in"], l)
        y_a, y_d, new_pool, vn = _mix_sample(ps, pool_t, lp, l, past_len)
        new_state, y_bt = _hgrn_sample(pst, hgrn_t, lp, l)
        y_ct = _attn_sample(pst, cache_kt, cache_vt, page_table, l, lp)
        xs = _outproj_sample(xs, y_a, y_bt, y_ct, y_d, wb["w_out"], l, ln_g, ln_b, alpha)
        xs = _ffn(xs, wb["ffn2_w_gu"], wb["ffn2_w_dn"], l, ln_g, ln_b, 2, alpha)
        outs["k_s"].append(jnp.transpose(pst[6 * w:7 * w].reshape(N_HEADS, D_HEAD, ns), (2, 0, 1))[:, None])
        outs["v_s"].append(jnp.transpose(pst[7 * w:8 * w].reshape(N_HEADS, D_HEAD, ns), (2, 0, 1))[:, None])
        outs["pool_s"].append(jnp.transpose(new_pool, (1, 0, 2)))
        outs["hgrn_s"].append(jnp.transpose(new_state, (3, 0, 1, 2)))
        outs["sgv_s"].append(vn[:, None])

    st = {k: jnp.stack(v, axis=0) for k, v in outs.items()}
    return (xp.reshape(nb, t, d), xs.reshape(ns, 1, d), st["pool_p"], st["pool_s"], st["hgrn_p"], st["hgrn_s"],
            st["k_p"], st["k_s"], st["v_p"], st["v_s"], st["sgv_p"], st["sgv_s"])
```

```python
import functools
import math

import jax
import jax.numpy as jnp
from jax import lax
from jax.experimental import pallas as pl
from jax.experimental.pallas import tpu as pltpu

F32 = jnp.float32
BF16 = jnp.bfloat16

W_GROUP = 256
N_HEADS = 4
D_HEAD = 64
DA_HALF = 32
POOL_WINDOWS = (2, 4, 8, 16)
POOL_BUF = 15
PAGE = 128
SG_CHUNK = 128
HG_SUB = 16
V_AUG = 80
REL_BUCKETS = 32
REL_MAX_DIST = 128
EPS = 1e-5
NEG = -1e30
LOG2E = 1.0 / math.log(2.0)

VMEM_LIMIT = 56 * 1024 * 1024
MXU_DIM = 256
ROW_TILE = 512
FFN_CHUNKS = 2
ATT_TQ = 1024
ATT_TK = 512
DEC_SAMPLES_PER_STEP = 2
DEC_SLOTS = 3

NT_DIMS = (((1,), (1,)), ((), ()))


def _params(*sem):
    return pltpu.CompilerParams(dimension_semantics=sem, vmem_limit_bytes=VMEM_LIMIT)


def _const_spec(shape, single=False):
    nd = len(shape)
    kw = {"pipeline_mode": pl.Buffered(1)} if single else {}
    return pl.BlockSpec(shape, lambda *_: (0,) * nd, **kw)


def _layer_spec(shape, layer, single=False, rows=None, row_block=0):
    kw = {"pipeline_mode": pl.Buffered(1)} if single else {}
    block = (None, shape[1] if rows is None else rows) + tuple(shape[2:])
    return pl.BlockSpec(block, lambda *_: (layer, row_block) + (0,) * (len(shape) - 2), **kw)


def _layer_norm(y, g, b):
    mu = jnp.mean(y, axis=-1, keepdims=True)
    yc = y - mu
    var = jnp.mean(yc * yc, axis=-1, keepdims=True)
    return yc * lax.rsqrt(var + EPS) * g + b


def _silu(x):
    return x * jax.nn.sigmoid(x)


def _log_forget(z, log_lb, log1m_lb):
    log_sig = jnp.minimum(z, 0.0) - jnp.log(1.0 + jnp.exp(-jnp.abs(z)))
    b = log1m_lb + log_sig
    return jnp.maximum(log_lb, b) + jnp.log(1.0 + jnp.exp(-jnp.abs(log_lb - b)))


def _swiglu_ln_steps(x, wgu_ref, wdn_ref, g, b, d_ff, bounds, alpha, out):
    xb = x.astype(BF16)
    acc = None
    for lo, hi in zip(bounds[:-1], bounds[1:]):
        gate = jnp.dot(xb, wgu_ref[:, lo:hi], preferred_element_type=F32)
        up = jnp.dot(xb, wgu_ref[:, d_ff + lo:d_ff + hi], preferred_element_type=F32)
        yield
        h = (_silu(gate) * up).astype(BF16)
        part = jnp.dot(h, wdn_ref[lo:hi, :], preferred_element_type=F32)
        acc = part if acc is None else acc + part
        yield
    out(_layer_norm(alpha * x + 0.5 * acc, g, b))


def _interleave(*gens):
    live = list(gens)
    while live:
        for gen in list(live):
            try:
                next(gen)
            except StopIteration:
                live.remove(gen)


def _row_halves(tm):
    return [slice(0, tm)] if tm < 256 else [slice(0, tm // 2), slice(tm // 2, tm)]


def _ffn_kernel(x_ref, wgu_ref, wdn_ref, g_ref, b_ref, o_ref, *, d_ff, bounds, alpha):
    def rows_of(rs):
        def out(v):
            o_ref[rs, :] = v
        return _swiglu_ln_steps(x_ref[rs, :], wgu_ref, wdn_ref, g_ref[...], b_ref[...], d_ff, bounds, alpha, out)

    _interleave(*[rows_of(rs) for rs in _row_halves(x_ref.shape[0])])


def _ffn_bounds(d_ff):
    n_tiles = d_ff // MXU_DIM
    cuts = [round(n_tiles * c / FFN_CHUNKS) * MXU_DIM for c in range(FFN_CHUNKS)]
    return tuple(cuts) + (d_ff,)


def _ffn(x, w_gu, w_dn, layer, ln_g, ln_b, ln_i, alpha):
    r, d = x.shape
    d_ff = w_dn.shape[1]
    tm = min(ROW_TILE, r)
    bounds = _ffn_bounds(d_ff)
    return pl.pallas_call(
        functools.partial(_ffn_kernel, d_ff=d_ff, bounds=bounds, alpha=alpha),
        grid=(r // tm,),
        in_specs=[
            pl.BlockSpec((tm, d), lambda i: (i, 0)),
            _layer_spec(w_gu.shape, layer, single=True),
            _layer_spec(w_dn.shape, layer, single=True),
            _layer_spec(ln_g.shape, 3 * layer + ln_i),
            _layer_spec(ln_b.shape, 3 * layer + ln_i),
        ],
        out_specs=pl.BlockSpec((tm, d), lambda i: (i, 0)),
        out_shape=jax.ShapeDtypeStruct((r, d), F32),
        compiler_params=_params("arbitrary"),
        name="ffn",
    )(x, w_gu, w_dn, ln_g, ln_b)


def _ffn_inproj_p_kernel(x_ref, wgu_ref, wdn_ref, g_ref, b_ref, w_ref,
                         x1_ref, pa_ref, pd_ref, k_ref, qt_ref, kt_ref, vt_ref, *, d_ff, bounds, alpha, q_scale):
    w = W_GROUP

    def rows_of(rs):
        got = []
        yield from _swiglu_ln_steps(x_ref[rs, :], wgu_ref, wdn_ref, g_ref[...], b_ref[...], d_ff, bounds, alpha,
                                    got.append)
        x1 = got[0]
        x1_ref[rs, :] = x1
        xb = x1.astype(BF16)
        pa_ref[rs, :] = jnp.dot(xb, w_ref[:, 0:5 * w], preferred_element_type=F32)
        pd_ref[rs, :] = jnp.dot(xb, w_ref[:, 8 * w:10 * w], preferred_element_type=F32)
        qkv = jnp.dot(xb, w_ref[:, 5 * w:8 * w], preferred_element_type=F32)
        yield
        for h in range(N_HEADS):
            k_ref[h, rs, :] = qkv[:, w + h * D_HEAD:w + (h + 1) * D_HEAD].astype(BF16)
        qkvt = qkv.T
        qt_ref[0, :, rs] = (qkvt[0:w] * q_scale).astype(BF16)
        kt_ref[0, :, rs] = qkvt[w:2 * w]
        vt_ref[0, :, rs] = qkvt[2 * w:3 * w]

    _interleave(*[rows_of(rs) for rs in _row_halves(x_ref.shape[0])])


def _ffn_inproj_prompt(x, w_gu, w_dn, ln_g, ln_b, w_in, layer, n_batch, alpha):
    r, d = x.shape
    t = r // n_batch
    w = W_GROUP
    tm = min(ROW_TILE, t)
    tpb = t // tm
    d_ff = w_dn.shape[1]
    chan_major = lambda: pl.BlockSpec((1, w, tm), lambda i: (i // tpb, 0, i % tpb))
    return pl.pallas_call(
        functools.partial(_ffn_inproj_p_kernel, d_ff=d_ff, bounds=_ffn_bounds(d_ff), alpha=alpha,
                          q_scale=DA_HALF ** -0.5 * LOG2E),
        grid=(r // tm,),
        in_specs=[
            pl.BlockSpec((tm, d), lambda i: (i, 0)),
            _layer_spec(w_gu.shape, layer, single=True),
            _layer_spec(w_dn.shape, layer, single=True),
            _layer_spec(ln_g.shape, 3 * layer), _layer_spec(ln_b.shape, 3 * layer),
            _layer_spec(w_in.shape, layer, single=True),
        ],
        out_specs=[
            pl.BlockSpec((tm, d), lambda i: (i, 0)),
            pl.BlockSpec((tm, 5 * w), lambda i: (i, 0)),
            pl.BlockSpec((tm, 2 * w), lambda i: (i, 0)),
            pl.BlockSpec((N_HEADS, tm, D_HEAD), lambda i: (0, i, 0)),
            chan_major(), chan_major(), chan_major(),
        ],
        out_shape=[
            jax.ShapeDtypeStruct((r, d), F32),
            jax.ShapeDtypeStruct((r, 5 * w), F32),
            jax.ShapeDtypeStruct((r, 2 * w), F32),
            jax.ShapeDtypeStruct((N_HEADS, r, D_HEAD), BF16),
            jax.ShapeDtypeStruct((n_batch, w, t), BF16),
            jax.ShapeDtypeStruct((n_batch, w, t), F32),
            jax.ShapeDtypeStruct((n_batch, w, t), F32),
        ],
        compiler_params=_params("arbitrary"),
        name="ffn_inproj_prompt",
    )(x, w_gu, w_dn, ln_g, ln_b, w_in)


def _inproj_s_kernel(x_ref, w_ref, p_ref, pt_ref):
    p = jnp.dot(x_ref[...].astype(BF16), w_ref[...], preferred_element_type=F32)
    p_ref[...] = p
    pt_ref[...] = p.T


def _inproj_sample(x, w_in, layer):
    n, d = x.shape
    d_in = w_in.shape[2]
    return pl.pallas_call(
        _inproj_s_kernel,
        grid=(1,),
        in_specs=[_const_spec((n, d)), _layer_spec(w_in.shape, layer)],
        out_specs=[_const_spec((n, d_in)), _const_spec((d_in, n))],
        out_shape=[jax.ShapeDtypeStruct((n, d_in), F32), jax.ShapeDtypeStruct((d_in, n), F32)],
        compiler_params=_params("arbitrary"),
        name="inproj_sample",
    )(x, w_in)


def _group_select(grp, parts):
    out = parts[N_HEADS - 1]
    for g in range(N_HEADS - 2, -1, -1):
        out = jnp.where(grp == g, parts[g], out)
    return out


def _mix_p_kernel(pa_ref, pd_ref, poolw_ref, pscale_ref, loglb_ref, log1mlb_ref, omlb_ref, hgg_ref,
                  sgg_ref, sgb_ref, sgw_ref, sgbias_ref, eones_ref,
                  y_ref, pool_ref, st_out_ref, sgv_ref,
                  prev_ref, st_ref):
    c = SG_CHUNK
    w = W_GROUP
    t = pl.program_id(1)

    @pl.when(t == 0)
    def _():
        prev_ref[...] = jnp.zeros_like(prev_ref)
        st_ref[...] = jnp.zeros_like(st_ref)

    a = pa_ref[:, 0:w]
    hq = pa_ref[:, w:2 * w]
    hf = pa_ref[:, 2 * w:3 * w]
    hi = pa_ref[:, 3 * w:4 * w]
    hg = pa_ref[:, 4 * w:5 * w]
    su = pd_ref[:, 0:w]
    sv = pd_ref[:, w:2 * w]
    lane = lax.broadcasted_iota(jnp.int32, (c, w), 1)
    row = lax.broadcasted_iota(jnp.int32, (c, w), 0)
    grp = lane >> 6

    e = jnp.concatenate([prev_ref[...], a], axis=0)
    s2 = e + pltpu.roll(e, 1, 0)
    s4 = s2 + pltpu.roll(s2, 2, 0)
    s8 = s4 + pltpu.roll(s4, 4, 0)
    s16 = s8 + pltpu.roll(s8, 8, 0)
    wsum = _group_select(grp, [s2[16:], s4[16:], s8[16:], s16[16:]])
    win = _group_select(grp, [jnp.full((c, w), v, jnp.int32) for v in POOL_WINDOWS])
    cnt = jnp.minimum(t * c + row + 1, win).astype(F32)
    dpool = wsum / cnt - a
    ya = jnp.dot(dpool.astype(BF16), poolw_ref[...], preferred_element_type=F32) * pscale_ref[...]
    prev_ref[...] = a[c - 16:]
    pool_ref[0] = a[c - 16:]

    vn = _layer_norm(sv, sgg_ref[...], sgb_ref[...])
    sg = jnp.dot(sgw_ref[...], vn.astype(BF16), preferred_element_type=F32)
    s_gate = _group_select(grp, [sg[g * c:(g + 1) * c] for g in range(N_HEADS)]) + sgbias_ref[...]
    yd = su * s_gate
    sgv_ref[0] = vn

    n_sub = c // HG_SUB
    logf = _log_forget(hf, loglb_ref[...], log1mlb_ref[...])
    kin = omlb_ref[...] * jax.nn.sigmoid(-hf)
    r16 = row & (HG_SUB - 1)
    bl = logf
    rv = logf
    for sh in (1, 2, 4, 8):
        bl = bl + jnp.where(r16 >= sh, pltpu.roll(bl, sh, 0), 0.0)
        rv = rv + jnp.where(r16 + sh < HG_SUB, pltpu.roll(rv, c - sh, 0), 0.0)
    sub_row = row >> 4
    qtb = (hq * jnp.exp(bl)).astype(BF16)
    kt = kin * jnp.exp(rv - logf)
    dec = jnp.exp(rv)
    vtb = hi.T.astype(BF16)
    bi0 = lax.broadcasted_iota(jnp.int32, (w, w), 0) >> 6
    bi1 = lax.broadcasted_iota(jnp.int32, (w, w), 1) >> 6
    blockmask = bi0 == bi1
    st = st_ref[...]
    seen = []
    for i in range(n_sub):
        seen.append(st.astype(BF16))
        km = jnp.where(sub_row == i, kt, 0.0).astype(BF16)
        u = jnp.dot(vtb, km, preferred_element_type=F32)
        st = dec[i * HG_SUB:i * HG_SUB + 1] * st + jnp.where(blockmask, u, 0.0)
    st_ref[...] = st
    st_out_ref[0] = st
    qx = jnp.concatenate([jnp.where(sub_row == i, qtb, jnp.zeros_like(qtb)) for i in range(n_sub)], axis=1)
    o = lax.dot_general(qx, jnp.concatenate(seen, axis=1), NT_DIMS,
                        preferred_element_type=F32)

    rt = {lo: lax.broadcasted_iota(jnp.int32, (HG_SUB - lo, w), 0) + lo for lo in (0, 8)}
    bl2 = bl * LOG2E
    xs = []
    for i in range(n_sub):
        rows = slice(i * HG_SUB, (i + 1) * HG_SUB)
        bli, qi, ki = bl2[rows], hq[rows], kin[rows]
        for s in range(HG_SUB):
            lo = (s // 8) * 8
            ratio = jnp.exp2(jnp.where(rt[lo] >= s, bli[lo:] - bli[s:s + 1], NEG))
            x_s = ratio * qi[lo:] * ki[s:s + 1]
            if lo:
                x_s = jnp.concatenate([jnp.zeros((lo, w), F32), x_s], axis=0)
            xs.append(x_s.astype(BF16))
    x = jnp.concatenate(xs, axis=0)
    r = jnp.dot(x, eones_ref[...], preferred_element_type=F32)
    o_intra = []
    for i in range(n_sub):
        vi = hi[i * HG_SUB:(i + 1) * HG_SUB]
        base = i * HG_SUB * HG_SUB
        oi = r[base:base + HG_SUB] * vi[0:1]
        for s in range(1, HG_SUB):
            oi = oi + r[base + s * HG_SUB:base + (s + 1) * HG_SUB] * vi[s:s + 1]
        o_intra.append(oi)
    o = o + jnp.concatenate(o_intra, axis=0)
    ms = [jnp.mean(jnp.square(o[:, g * D_HEAD:(g + 1) * D_HEAD]), axis=-1, keepdims=True)
          for g in range(N_HEADS)]
    yb = o * lax.rsqrt(_group_select(grp, ms) + EPS) * hgg_ref[...] * _silu(hg)

    y_ref[:, 0:w] = ya.astype(BF16)
    y_ref[:, w:2 * w] = yb.astype(BF16)
    y_ref[:, 2 * w:3 * w] = yd.astype(BF16)


def _mix_prompt(p_a, p_d, n_batch, lp, layer):
    r = p_a.shape[0]
    t = r // n_batch
    c = SG_CHUNK
    nt = t // c
    w = W_GROUP
    per_layer = lambda name: _layer_spec(lp[name].shape, layer)
    return pl.pallas_call(
        _mix_p_kernel,
        grid=(n_batch, nt),
        in_specs=[
            pl.BlockSpec((c, 5 * w), lambda b, i: (b * nt + i, 0)),
            pl.BlockSpec((c, 2 * w), lambda b, i: (b * nt + i, 0)),
            per_layer("pool_w"), per_layer("pool_scale"), per_layer("log_lb"), per_layer("log1m_lb"),
            per_layer("om_lb"), per_layer("hg_g"), per_layer("sg_g"), per_layer("sg_b"),
            per_layer("sg_w"), per_layer("sg_bias"), _const_spec((w, w)),
        ],
        out_specs=[
            pl.BlockSpec((c, 3 * w), lambda b, i: (b * nt + i, 0)),
            pl.BlockSpec((1, 16, w), lambda b, i: (b, 0, 0)),
            pl.BlockSpec((1, w, w), lambda b, i: (b, 0, 0)),
            pl.BlockSpec((1, c, w), lambda b, i: (b, 0, 0)),
        ],
        out_shape=[
            jax.ShapeDtypeStruct((r, 3 * w), BF16),
            jax.ShapeDtypeStruct((n_batch, 16, w), F32),
            jax.ShapeDtypeStruct((n_batch, w, w), F32),
            jax.ShapeDtypeStruct((n_batch, c, w), F32),
        ],
        scratch_shapes=[pltpu.VMEM((16, w), F32), pltpu.VMEM((w, w), F32)],
        compiler_params=_params("arbitrary", "arbitrary"),
        name="mix_prompt",
    )(p_a, p_d, lp["pool_w"], lp["pool_scale"], lp["log_lb"], lp["log1m_lb"], lp["om_lb"], lp["hg_g"],
      lp["sg_g"], lp["sg_b"], lp["sg_w"], lp["sg_bias"], lp["eones"])


def _attn_p_kernel(cfar_ref, lam_ref, qt_ref, k_ref, vt_ref, near_ref, subg_ref, o_ref,
                   va_ref, m_ref, acc_ref, s_ref, mb_ref, *, tq, tk, layer):
    h = pl.program_id(1)
    qi = pl.program_id(2)
    dh = D_HEAD

    @pl.when(qi == 0)
    def _():
        va_ref[0:dh, :] = vt_ref[0].astype(BF16)
        rows = lax.broadcasted_iota(jnp.int32, (V_AUG - dh, va_ref.shape[1]), 0)
        va_ref[dh:V_AUG, :] = jnp.where(rows == 0, 1.0, 0.0).astype(BF16)

    qt = qt_ref[0]
    sub = lax.broadcasted_iota(jnp.int32, qt.shape, 0)
    zero = jnp.zeros_like(qt)
    qm = (jnp.where(sub < DA_HALF, qt, zero), jnp.where(sub >= DA_HALF, qt, zero))
    m_ref[...] = jnp.full_like(m_ref, NEG)
    acc_ref[...] = jnp.zeros_like(acc_ref)

    def qk(ki, slot, near=None, q0=0):
        k0 = pl.multiple_of(ki * tk, tk)
        kblk = k_ref[0, pl.ds(k0, tk), :]
        for mi in range(2):
            s = jnp.dot(kblk, qm[mi][:, q0:], preferred_element_type=F32)
            if near is not None:
                s = s + near_ref[0, near, :, q0:]
            s_ref[slot, mi, :, q0:] = s
            mb_ref[slot, mi, :, q0:] = jnp.max(s, axis=0, keepdims=True)

    def softmax_pv(ki, slot, shift, q0=0):
        k0 = pl.multiple_of(ki * tk, tk)
        vblk = va_ref[:, pl.ds(k0, tk)]
        for mi in range(2):
            s = s_ref[slot, mi, :, q0:]
            m_blk = mb_ref[slot, mi, :, q0:]
            if shift is not None:
                m_blk = m_blk + shift
            m_old = m_ref[mi, :, q0:]
            m_new = jnp.maximum(m_old, m_blk)
            alpha = jnp.exp2(m_old - m_new)
            p = jnp.exp2(s - (m_new if shift is None else m_new - shift)).astype(BF16)
            acc_ref[mi, :, q0:] = alpha * acc_ref[mi, :, q0:] + jnp.dot(vblk, p, preferred_element_type=F32)
            m_ref[mi, :, q0:] = m_new

    cfar = cfar_ref[h]
    n_diag = tq // tk
    kb0 = qi * n_diag

    def near_tiles(first_tile, first_near, n, slot0):
        q0s = [max(first_near + j - 1, 0) * tk for j in range(n)]
        for j in range(n):
            if j + 1 < n:
                qk(first_tile + j + 1, (slot0 + j + 1) % 2, first_near + j + 1, q0s[j + 1])
            softmax_pv(first_tile + j, (slot0 + j) % 2, None, q0s[j])

    def far_pair(pi, carry):
        b = 2 * pi
        qk(b + 1, 1)
        softmax_pv(b, 0, cfar)
        qk(b + 2, 0)
        softmax_pv(b + 1, 1, cfar)
        return carry

    @pl.when(qi == 0)
    def _():
        qk(0, 0, 1)
        near_tiles(0, 1, n_diag, 0)

    @pl.when(qi >= 1)
    def _():
        n_far = kb0 - 1
        qk(0, 0)
        lax.fori_loop(0, (n_far - 1) // 2, far_pair, 0)

        def tail(rem):
            if rem == 2:
                qk(n_far - 1, 1)
                softmax_pv(n_far - 2, 0, cfar)
            last_slot = rem % 2
            qk(n_far, last_slot, 0)
            softmax_pv(n_far - 1, 1 - last_slot, cfar)
            near_tiles(n_far, 0, n_diag + 1, last_slot)

        if n_diag % 2 == 0:
            tail(1)
        else:
            pl.when(n_far % 2 == 1)(lambda: tail(1))
            pl.when(n_far % 2 == 0)(lambda: tail(2))

    a0 = acc_ref[0]
    a1 = acc_ref[1]
    o = a0[0:dh] / a0[dh:dh + 1] - lam_ref[layer] * (a1[0:dh] / a1[dh:dh + 1])
    ms = jnp.mean(o * o, axis=0, keepdims=True)
    o_ref[0] = (o * lax.rsqrt(ms + EPS) * subg_ref[...]).astype(BF16)


def _attn_tiles(t):
    tq = min(ATT_TQ, t)
    return tq, min(ATT_TK, tq // 2)


def _attn_prompt(qt, k_rows, vt, lp, layer, n_batch):
    t = qt.shape[2]
    tq, tk = _attn_tiles(t)
    nq = t // tq
    n_near = tq // tk + 1
    smem = pl.BlockSpec(memory_space=pltpu.SMEM)
    return pl.pallas_call(
        functools.partial(_attn_p_kernel, tq=tq, tk=tk, layer=layer),
        grid=(n_batch, N_HEADS, nq),
        in_specs=[
            smem, smem,
            pl.BlockSpec((1, D_HEAD, tq), lambda b, h, i: (b, h, i)),
            pl.BlockSpec((1, t, D_HEAD), lambda b, h, i: (h, b, 0)),
            pl.BlockSpec((1, D_HEAD, t), lambda b, h, i: (b, h, 0)),
            pl.BlockSpec((1, n_near, tk, tq), lambda b, h, i: (h, 0, 0, 0)),
            _layer_spec(lp["sub_g_col"].shape, layer),
        ],
        out_specs=pl.BlockSpec((1, D_HEAD, tq), lambda b, h, i: (b, h, i)),
        out_shape=jax.ShapeDtypeStruct((n_batch, W_GROUP, t), BF16),
        scratch_shapes=[
            pltpu.VMEM((V_AUG, t), BF16),
            pltpu.VMEM((2, 1, tq), F32), pltpu.VMEM((2, V_AUG, tq), F32),
            pltpu.VMEM((2, 2, tk, tq), F32), pltpu.VMEM((2, 2, 1, tq), F32),
        ],
        compiler_params=_params("arbitrary", "arbitrary", "arbitrary"),
        name="attn_prompt",
    )(lp["c_far"], lp["lam"], qt, k_rows, vt, lp["bias_near"], lp["sub_g_col"])


def _outproj_ffn_p_kernel(x_ref, y_ref, yct_ref, wab_ref, wc_ref, wd_ref, g1_ref, b1_ref,
                          wgu_ref, wdn_ref, g2_ref, b2_ref, o_ref, *, d_ff, bounds, alpha):
    w = W_GROUP

    def rows_of(rs):
        acc = jnp.dot(y_ref[rs, 0:2 * w], wab_ref[...], preferred_element_type=F32)
        acc = acc + jnp.dot(y_ref[rs, 2 * w:3 * w], wd_ref[...], preferred_element_type=F32)
        yc = yct_ref[0, :, rs].astype(F32).T.astype(BF16)
        acc = acc + jnp.dot(yc, wc_ref[...], preferred_element_type=F32)
        yield
        x2 = _layer_norm(alpha * x_ref[rs, :] + acc, g1_ref[...], b1_ref[...])

        def out(v):
            o_ref[rs, :] = v
        yield from _swiglu_ln_steps(x2, wgu_ref, wdn_ref, g2_ref[...], b2_ref[...], d_ff, bounds, alpha, out)

    _interleave(*[rows_of(rs) for rs in _row_halves(x_ref.shape[0])])


def _outproj_ffn_prompt(x, y_abd, y_ct, w_out, w_gu, w_dn, layer, ln_g, ln_b, alpha):
    r, d = x.shape
    w = W_GROUP
    t = y_ct.shape[2]
    tm = min(ROW_TILE, t)
    tpb = t // tm
    d_ff = w_dn.shape[1]
    return pl.pallas_call(
        functools.partial(_outproj_ffn_p_kernel, d_ff=d_ff, bounds=_ffn_bounds(d_ff), alpha=alpha),
        grid=(r // tm,),
        in_specs=[
            pl.BlockSpec((tm, d), lambda i: (i, 0)),
            pl.BlockSpec((tm, 3 * W_GROUP), lambda i: (i, 0)),
            pl.BlockSpec((1, W_GROUP, tm), lambda i: (i // tpb, 0, i % tpb)),
            _layer_spec(w_out.shape, layer, rows=2 * w, row_block=0),
            _layer_spec(w_out.shape, layer, rows=w, row_block=2),
            _layer_spec(w_out.shape, layer, rows=w, row_block=3),
            _layer_spec(ln_g.shape, 3 * layer + 1), _layer_spec(ln_b.shape, 3 * layer + 1),
            _layer_spec(w_gu.shape, layer, single=True),
            _layer_spec(w_dn.shape, layer, single=True),
            _layer_spec(ln_g.shape, 3 * layer + 2), _layer_spec(ln_b.shape, 3 * layer + 2),
        ],
        out_specs=pl.BlockSpec((tm, d), lambda i: (i, 0)),
        out_shape=jax.ShapeDtypeStruct((r, d), F32),
        compiler_params=_params("arbitrary"),
        name="outproj_ffn_prompt",
    )(x, y_abd, y_ct, w_out, w_out, w_out, ln_g, ln_b, w_gu, w_dn, ln_g, ln_b)


def _outproj_s_kernel(x_ref, ya_ref, ybt_ref, yct_ref, yd_ref, w_ref, g_ref, b_ref, o_ref, *, alpha):
    w = W_GROUP
    parts = (ya_ref[...], ybt_ref[...].T, yct_ref[...].T, yd_ref[...])
    acc = None
    for j, part in enumerate(parts):
        d = jnp.dot(part.astype(BF16), w_ref[j * w:(j + 1) * w, :], preferred_element_type=F32)
        acc = d if acc is None else acc + d
    o_ref[...] = _layer_norm(alpha * x_ref[...] + acc, g_ref[...], b_ref[...])


def _outproj_sample(x, y_a, y_bt, y_ct, y_d, w_out, layer, ln_g, ln_b, alpha):
    n, d = x.shape
    w = W_GROUP
    return pl.pallas_call(
        functools.partial(_outproj_s_kernel, alpha=alpha),
        grid=(1,),
        in_specs=[
            _const_spec((n, d)), _const_spec((n, w)), _const_spec((w, n)), _const_spec((w, n)),
            _const_spec((n, w)), _layer_spec(w_out.shape, layer),
            _layer_spec(ln_g.shape, 3 * layer + 1), _layer_spec(ln_b.shape, 3 * layer + 1),
        ],
        out_specs=_const_spec((n, d)),
        out_shape=jax.ShapeDtypeStruct((n, d), F32),
        compiler_params=_params("arbitrary"),
        name="outproj_sample",
    )(x, y_a, y_bt, y_ct, y_d, w_out, ln_g, ln_b)


def _mix_s_kernel(pa_ref, pd_ref, pool_ref, poolw_ref, pscale_ref, sgg_ref, sgb_ref, w00_ref, b0_ref,
                  ya_ref, yd_ref, newpool_ref, vn_ref, *, cnts):
    a = pa_ref[...]
    n, w = a.shape
    grp = lax.broadcasted_iota(jnp.int32, (n, w), 1) >> 6
    acc = a
    means = []
    for j in range(1, POOL_BUF + 1):
        acc = acc + pool_ref[POOL_BUF - j]
        if j + 1 in POOL_WINDOWS:
            means.append(acc / cnts[POOL_WINDOWS.index(j + 1)])
    dpool = _group_select(grp, means) - a
    ya_ref[...] = jnp.dot(dpool.astype(BF16), poolw_ref[...], preferred_element_type=F32) * pscale_ref[...]
    for j in range(POOL_BUF - 1):
        newpool_ref[j] = pool_ref[j + 1]
    newpool_ref[POOL_BUF - 1] = a

    vn = _layer_norm(pd_ref[:, w:2 * w], sgg_ref[...], sgb_ref[...])
    vn_ref[...] = vn
    yd_ref[...] = pd_ref[:, 0:w] * (w00_ref[...] * vn + b0_ref[...])


def _mix_sample(p, pool, lp, layer, past_len):
    n = p.shape[0]
    w = W_GROUP
    cnts = tuple(float(min(past_len + 1, win)) for win in POOL_WINDOWS)
    per_layer = lambda name: _layer_spec(lp[name].shape, layer)
    return pl.pallas_call(
        functools.partial(_mix_s_kernel, cnts=cnts),
        grid=(1,),
        in_specs=[
            pl.BlockSpec((n, w), lambda i: (0, 0)),
            pl.BlockSpec((n, 2 * w), lambda i: (0, 4)),
            _layer_spec(pool.shape, layer), per_layer("pool_w"), per_layer("pool_scale"), per_layer("sg_g"),
            per_layer("sg_b"), per_layer("sg_w00"), per_layer("sg_b0"),
        ],
        out_specs=[_const_spec((n, w)), _const_spec((n, w)), _const_spec(pool.shape[1:]), _const_spec((n, w))],
        out_shape=[
            jax.ShapeDtypeStruct((n, w), F32), jax.ShapeDtypeStruct((n, w), F32),
            jax.ShapeDtypeStruct(pool.shape[1:], F32), jax.ShapeDtypeStruct((n, w), F32),
        ],
        compiler_params=_params("arbitrary"),
        name="mix_sample",
    )(p, p, pool, lp["pool_w"], lp["pool_scale"], lp["sg_g"], lp["sg_b"], lp["sg_w00"], lp["sg_b0"])


def _hgrn_s_kernel(q_ref, f_ref, i_ref, g_ref, s_ref, loglb_ref, log1mlb_ref, omlb_ref, hgg_ref,
                   so_ref, yb_ref):
    z = f_ref[...]
    f = jnp.exp(_log_forget(z, loglb_ref[...], log1mlb_ref[...]))
    kin = omlb_ref[...] * jax.nn.sigmoid(-z)
    q = q_ref[...]
    v = i_ref[...]
    o = jnp.zeros_like(v)
    for d in range(D_HEAD):
        sn = f[d:d + 1] * s_ref[0, d] + kin[d:d + 1] * v
        so_ref[0, d] = sn
        o = o + q[d:d + 1] * sn
    ms = jnp.mean(o * o, axis=0, keepdims=True)
    yb_ref[...] = o * lax.rsqrt(ms + EPS) * hgg_ref[...] * _silu(g_ref[...])


def _hgrn_sample(pt, state, lp, layer):
    n = pt.shape[1]
    dh = D_HEAD
    blk = lambda off: pl.BlockSpec((dh, n), lambda h: (off * N_HEADS + h, 0))
    col = lambda: pl.BlockSpec((None, dh, 1), lambda h: (layer, h, 0))
    return pl.pallas_call(
        _hgrn_s_kernel,
        grid=(N_HEADS,),
        in_specs=[
            blk(1), blk(2), blk(3), blk(4),
            pl.BlockSpec((None, 1, dh, dh, n), lambda h: (layer, h, 0, 0, 0)),
            col(), col(), col(), _layer_spec(lp["hg_g_col"].shape, layer),
        ],
        out_specs=[
            pl.BlockSpec((1, dh, dh, n), lambda h: (h, 0, 0, 0)),
            pl.BlockSpec((dh, n), lambda h: (h, 0)),
        ],
        out_shape=[
            jax.ShapeDtypeStruct(state.shape[1:], F32),
            jax.ShapeDtypeStruct((W_GROUP, n), F32),
        ],
        compiler_params=_params("arbitrary"),
        name="hgrn_sample",
    )(pt, pt, pt, pt, state, lp["log_lb_col"], lp["log1m_lb_col"], lp["om_lb_col"], lp["hg_g_col"])


def _attn_s_one(n, k_pages, v_pages, lam, qt_ref, knt_ref, vnt_ref, bias_ref, bself_ref, n_pages, q_scale):
    w, ns = qt_ref.shape
    dh = D_HEAD
    sel = lax.broadcasted_iota(jnp.int32, (w, ns), 1) == n

    def column(ref):
        return jnp.sum(jnp.where(sel, ref[...], 0.0), axis=1, keepdims=True)

    qcol = column(qt_ref) * q_scale
    kcol = column(knt_ref)
    vcol = column(vnt_ref)

    n_grp = 2 * N_HEADS
    rows = ([], [])
    self_rows = ([], [])
    for h in range(N_HEADS):
        qh = qcol[h * dh:(h + 1) * dh]
        for j in range(n_pages):
            prod = k_pages[j][h] * qh
            rows[0].append(jnp.sum(prod[0:DA_HALF], axis=0, keepdims=True))
            rows[1].append(jnp.sum(prod[DA_HALF:dh], axis=0, keepdims=True))
        self_prod = qh * kcol[h * dh:(h + 1) * dh]
        for mi in range(2):
            s_self = (jnp.sum(self_prod[mi * DA_HALF:(mi + 1) * DA_HALF], axis=0, keepdims=True)
                      + bself_ref[h])
            self_rows[mi].append(jnp.broadcast_to(s_self, (n_pages, 1)))
    s = jnp.concatenate(rows[0] + rows[1], axis=0) + bias_ref[...]
    s_self = jnp.concatenate(self_rows[0] + self_rows[1], axis=0)

    def per_group(col, reduce):
        parts = [jnp.broadcast_to(reduce(col[g * n_pages:(g + 1) * n_pages], axis=0, keepdims=True),
                                  (n_pages, 1)) for g in range(n_grp)]
        return jnp.concatenate(parts, axis=0)

    m = jnp.maximum(per_group(jnp.max(s, axis=1, keepdims=True), jnp.max), s_self)
    p = jnp.exp(s - m)
    p_self = jnp.exp(s_self - m)
    inv_l = 1.0 / (per_group(jnp.sum(p, axis=1, keepdims=True), jnp.sum) + p_self)
    half = N_HEADS * n_pages
    a = (p * inv_l)[0:half] - lam * (p * inv_l)[half:2 * half]
    a_self = (p_self * inv_l)[0:half] - lam * (p_self * inv_l)[half:2 * half]
    weighted = []
    for h in range(N_HEADS):
        oh = v_pages[0][h] * a[h * n_pages:h * n_pages + 1]
        for j in range(1, n_pages):
            oh = oh + v_pages[j][h] * a[h * n_pages + j:h * n_pages + j + 1]
        weighted.append(oh)
    a_self_col = jnp.concatenate(
        [jnp.broadcast_to(a_self[h * n_pages:h * n_pages + 1], (dh, 1)) for h in range(N_HEADS)], axis=0)
    ocol = jnp.sum(jnp.concatenate(weighted, axis=0), axis=1, keepdims=True) + a_self_col * vcol
    return sel, ocol


def _attn_s_kernel(pt_ref, lam_ref, qt_ref, knt_ref, vnt_ref, bias_ref, bself_ref, subg_ref, ck_ref, cv_ref,
                   o_ref, ot_ref, kbuf_ref, vbuf_ref, sem_ref, *, n_pages, sps, layer, q_scale):
    dh = D_HEAD
    step = pl.program_id(0)
    n_steps = pl.num_programs(0)

    def page_copies(s, slot):
        copies = []
        for u in range(sps):
            for j in range(n_pages):
                page = pt_ref[s * sps + u, j]
                copies.append(pltpu.make_async_copy(ck_ref.at[layer, page], kbuf_ref.at[slot, u * n_pages + j],
                                                    sem_ref.at[0, slot]))
                copies.append(pltpu.make_async_copy(cv_ref.at[layer, page], vbuf_ref.at[slot, u * n_pages + j],
                                                    sem_ref.at[1, slot]))
        return copies

    @pl.when(step == 0)
    def _():
        for s in range(DEC_SLOTS - 1):
            @pl.when(s < n_steps)
            def _():
                for c in page_copies(s, s):
                    c.start()

    slot = step % DEC_SLOTS
    for c in page_copies(step, slot):
        c.wait()

    ot = ot_ref[...]
    for u in range(sps):
        k_pages = [kbuf_ref.at[slot, u * n_pages + j] for j in range(n_pages)]
        v_pages = [vbuf_ref.at[slot, u * n_pages + j] for j in range(n_pages)]
        sel, ocol = _attn_s_one(step * sps + u, k_pages, v_pages, lam_ref[layer],
                                qt_ref, knt_ref, vnt_ref, bias_ref, bself_ref, n_pages, q_scale)
        ot = jnp.where(sel, ocol, ot)
    ot_ref[...] = ot

    ahead = step + DEC_SLOTS - 1

    @pl.when(ahead < n_steps)
    def _():
        for c in page_copies(ahead, ahead % DEC_SLOTS):
            c.start()

    @pl.when(step == n_steps - 1)
    def _():
        for h in range(N_HEADS):
            oh = ot_ref[h * dh:(h + 1) * dh, :]
            ms = jnp.mean(oh * oh, axis=0, keepdims=True)
            o_ref[h * dh:(h + 1) * dh, :] = oh * lax.rsqrt(ms + EPS) * subg_ref[...]


def _attn_sample(pt, cache_kt, cache_vt, page_table, layer, lp):
    n = pt.shape[1]
    n_pages = page_table.shape[1]
    w = W_GROUP
    dh = D_HEAD
    blk = lambda off: pl.BlockSpec((w, n), lambda i, tbl: (off, 0))
    sps = DEC_SAMPLES_PER_STEP
    ring = (DEC_SLOTS, sps * n_pages, N_HEADS, dh, PAGE)
    grid_spec = pltpu.PrefetchScalarGridSpec(
        num_scalar_prefetch=1,
        grid=(n // sps,),
        in_specs=[
            pl.BlockSpec(memory_space=pltpu.SMEM),
            blk(5), blk(6), blk(7),
            pl.BlockSpec((2 * N_HEADS * n_pages, PAGE), lambda i, tbl: (0, 0)),
            pl.BlockSpec((N_HEADS, 1, 1), lambda i, tbl: (0, 0, 0)),
            pl.BlockSpec((None, dh, 1), lambda i, tbl: (layer, 0, 0)),
            pl.BlockSpec(memory_space=pl.ANY), pl.BlockSpec(memory_space=pl.ANY),
        ],
        out_specs=pl.BlockSpec((w, n), lambda i, tbl: (0, 0)),
        scratch_shapes=[
            pltpu.VMEM((w, n), F32), pltpu.VMEM(ring, F32), pltpu.VMEM(ring, F32),
            pltpu.SemaphoreType.DMA((2, DEC_SLOTS)),
        ],
    )
    return pl.pallas_call(
        functools.partial(_attn_s_kernel, n_pages=n_pages, sps=sps, layer=layer, q_scale=DA_HALF ** -0.5),
        grid_spec=grid_spec,
        out_shape=jax.ShapeDtypeStruct((w, n), F32),
        compiler_params=_params("arbitrary"),
        name="attn_sample",
    )(page_table, lp["lam"], pt, pt, pt, lp["bias_past"], lp["bias_self"], lp["sub_g_col"],
      cache_kt, cache_vt)


def _rel_bucket(dist):
    n = jnp.maximum(dist, 0)
    max_exact = REL_BUCKETS // 2
    large = max_exact + (jnp.log(jnp.maximum(n, 1).astype(F32) / max_exact)
                         / math.log(REL_MAX_DIST / max_exact) * (REL_BUCKETS - max_exact)).astype(jnp.int32)
    large = jnp.minimum(large, REL_BUCKETS - 1)
    return jnp.where(n < max_exact, n, large)


def _all_params(depth, tq, tk, past_len, n_pages, prm, lb_all):
    lam_init = jnp.asarray([0.8 - 0.6 * math.exp(-0.3 * l) for l in range(depth)], F32)
    lam = (jnp.exp(jnp.sum(prm["diff_lam_q1"] * prm["diff_lam_k1"], axis=-1))
           - jnp.exp(jnp.sum(prm["diff_lam_q2"] * prm["diff_lam_k2"], axis=-1)) + lam_init)
    rel_bias = prm["rel_bias"]

    def bias_of(dist):
        onehot = _rel_bucket(dist)[..., None] == jnp.arange(REL_BUCKETS, dtype=jnp.int32)
        table = rel_bias.T.reshape((N_HEADS,) + (1,) * dist.ndim + (REL_BUCKETS,))
        return jnp.sum(jnp.where(onehot[None], table, 0.0), axis=-1)

    blk = REL_MAX_DIST
    k0 = jnp.arange(blk, dtype=jnp.int32)[:, None]
    q0 = jnp.arange(blk, dtype=jnp.int32)[None, :]
    blocks = {}

    def block(e):
        e = max(min(e, 2), -1)
        if e not in blocks:
            dist = blk * e + q0 - k0
            blocks[e] = jnp.where((dist >= 0)[None], bias_of(dist), NEG) * LOG2E
        return blocks[e]

    tiles = []
    for r in range(-1, tq // tk):
        rows = [jnp.concatenate([block(q1 - k1 - r * (tk // blk)) for q1 in range(tq // blk)], axis=-1)
                for k1 in range(tk // blk)]
        tiles.append(jnp.concatenate(rows, axis=-2))
    bias_near = jnp.stack(tiles, axis=1)
    kpos = jnp.arange(n_pages * PAGE, dtype=jnp.int32)
    bias_past = jnp.tile(bias_of(past_len - kpos).reshape(N_HEADS * n_pages, PAGE), (2, 1))
    tril = jnp.tril(jnp.ones((SG_CHUNK, SG_CHUNK), F32))
    sub_g = prm["diff_subln_g"] * (1.0 - lam_init)[:, None]
    eye = jnp.eye(N_HEADS, dtype=F32)
    pool_w = jnp.einsum("gh,lgce->lgche", eye, prm["pool_w"]).reshape(depth, W_GROUP, W_GROUP)
    row = lambda a: a[:, None, :]
    col = lambda a: a[:, :, None]
    d_model = prm["ln_g"].shape[-1]
    return {
        "ln_g": prm["ln_g"].reshape(depth * 3, 1, d_model), "ln_b": prm["ln_b"].reshape(depth * 3, 1, d_model),
        "pool_w": pool_w.astype(BF16), "pool_scale": row(prm["pool_scale"]),
        "log_lb": row(jnp.log(lb_all)), "log1m_lb": row(jnp.log1p(-lb_all)), "om_lb": row(1.0 - lb_all),
        "log_lb_col": col(jnp.log(lb_all)), "log1m_lb_col": col(jnp.log1p(-lb_all)), "om_lb_col": col(1.0 - lb_all),
        "hg_g": row(jnp.tile(prm["hgrn_norm_g"], (1, N_HEADS))), "hg_g_col": col(prm["hgrn_norm_g"]),
        "sg_g": row(prm["sgu_ln_g"]), "sg_b": row(prm["sgu_ln_b"]),
        "sg_w": (prm["sgu_w"] * tril).reshape(depth, N_HEADS * SG_CHUNK, SG_CHUNK).astype(BF16),
        "sg_bias": jnp.repeat(jnp.swapaxes(prm["sgu_b"], 1, 2), D_HEAD, axis=2),
        "sg_w00": row(jnp.repeat(prm["sgu_w"][:, :, 0, 0], D_HEAD, axis=1)),
        "sg_b0": row(jnp.repeat(prm["sgu_b"][:, :, 0], D_HEAD, axis=1)),
        "eones": jnp.kron(eye, jnp.ones((D_HEAD, D_HEAD), F32)).astype(BF16),
        "lam": lam, "c_far": rel_bias[REL_BUCKETS - 1] * LOG2E,
        "bias_near": bias_near,
        "bias_past": bias_past, "bias_self": rel_bias[0].reshape(N_HEADS, 1, 1),
        "sub_g_col": col(sub_g),
    }


def kernel(x_prompt, x_sample, state_pool, state_hgrn, cache_k, cache_v, page_table, rel_bias, ln_g, ln_b,
           ffn1_w_gu, ffn1_w_dn, ffn2_w_gu, ffn2_w_dn, w_in, w_out, pool_w, pool_scale, hgrn_lb,
           hgrn_norm_g, diff_lam_q1, diff_lam_k1, diff_lam_q2, diff_lam_k2, diff_subln_g, sgu_ln_g,
           sgu_ln_b, sgu_w, sgu_b):
    prm = dict(rel_bias=rel_bias, ln_g=ln_g, ln_b=ln_b, ffn1_w_gu=ffn1_w_gu, ffn1_w_dn=ffn1_w_dn,
               ffn2_w_gu=ffn2_w_gu, ffn2_w_dn=ffn2_w_dn, w_in=w_in, w_out=w_out, pool_w=pool_w,
               pool_scale=pool_scale, hgrn_norm_g=hgrn_norm_g, diff_lam_q1=diff_lam_q1,
               diff_lam_k1=diff_lam_k1, diff_lam_q2=diff_lam_q2, diff_lam_k2=diff_lam_k2,
               diff_subln_g=diff_subln_g, sgu_ln_g=sgu_ln_g, sgu_ln_b=sgu_ln_b, sgu_w=sgu_w, sgu_b=sgu_b)
    depth = w_in.shape[0]
    nb, t, d = x_prompt.shape
    ns = x_sample.shape[0]
    n_pages = page_table.shape[1]
    past_len = n_pages * PAGE
    alpha = (2.0 * depth) ** 0.25
    w = W_GROUP
    tq, tk = _attn_tiles(t)
    assert x_sample.shape[1] == 1 and t % SG_CHUNK == 0 and t % tq == 0 and tk >= REL_MAX_DIST

    lb_cum = jnp.cumsum(jax.nn.softmax(hgrn_lb.astype(F32), axis=0), axis=0)
    lb_all = jnp.maximum(lb_cum - lb_cum[:1], 0.0)

    cache_kt = jnp.transpose(cache_k, (0, 1, 3, 4, 2))
    cache_vt = jnp.transpose(cache_v, (0, 1, 3, 4, 2))
    hgrn_t = jnp.transpose(state_hgrn, (0, 2, 3, 4, 1))
    pool_t = jnp.transpose(state_pool, (0, 2, 1, 3))

    wb = {k: prm[k].astype(BF16) for k in ("ffn1_w_gu", "ffn1_w_dn", "ffn2_w_gu", "ffn2_w_dn", "w_in", "w_out")}

    xp = x_prompt.reshape(nb * t, d)
    xs = x_sample.reshape(ns, d)
    lp = _all_params(depth, tq, tk, past_len, n_pages, prm, lb_all)
    ln_g, ln_b = lp["ln_g"], lp["ln_b"]
    outs = {k: [] for k in ("pool_p", "pool_s", "hgrn_p", "hgrn_s", "k_p", "k_s", "v_p", "v_s", "sgv_p", "sgv_s")}
    for l in range(depth):
        xp, p_a, p_d, k_rows, q_t, k_t, v_t = _ffn_inproj_prompt(
            xp, wb["ffn1_w_gu"], wb["ffn1_w_dn"], ln_g, ln_b, wb["w_in"], l, nb, alpha)
        y_abd, pool16, st, sgv = _mix_prompt(p_a, p_d, nb, lp, l)
        y_ct = _attn_prompt(q_t, k_rows, v_t, lp, l, nb)
        xp = _outproj_ffn_prompt(xp, y_abd, y_ct, wb["w_out"], wb["ffn2_w_gu"], wb["ffn2_w_dn"], l,
                                 ln_g, ln_b, alpha)
        outs["k_p"].append(jnp.transpose(k_t.reshape(nb, N_HEADS, D_HEAD, t), (0, 3, 1, 2)))
        outs["v_p"].append(jnp.transpose(v_t.reshape(nb, N_HEADS, D_HEAD, t), (0, 3, 1, 2)))
        outs["pool_p"].append(pool16[:, 1:])
        outs["hgrn_p"].append(jnp.stack(
            [jnp.swapaxes(st[:, h * D_HEAD:(h + 1) * D_HEAD, h * D_HEAD:(h + 1) * D_HEAD], 1, 2)
             for h in range(N_HEADS)], axis=1))
        outs["sgv_p"].append(sgv)

        xs = _ffn(xs, wb["ffn1_w_gu"], wb["ffn1_w_dn"], l, ln_g, ln_b, 0, alpha)
        ps, pst = _inproj_sample(xs, wb["w_in"], l)
        y_a, y_d, new_pool, vn = _mix_sample(ps, pool_t, lp, l, past_len)
        new_state, y_bt = _hgrn_sample(pst, hgrn_t, lp, l)
        y_ct = _attn_sample(pst, cache_kt, cache_vt, page_table, l, lp)
        xs = _outproj_sample(xs, y_a, y_bt, y_ct, y_d, wb["w_out"], l, ln_g, ln_b, alpha)
        xs = _ffn(xs, wb["ffn2_w_gu"], wb["ffn2_w_dn"], l, ln_g, ln_b, 2, alpha)
        outs["k_s"].append(jnp.transpose(pst[6 * w:7 * w].reshape(N_HEADS, D_HEAD, ns), (2, 0, 1))[:, None])
        outs["v_s"].append(jnp.transpose(pst[7 * w:8 * w].reshape(N_HEADS, D_HEAD, ns), (2, 0, 1))[:, None])
        outs["pool_s"].append(jnp.transpose(new_pool, (1, 0, 2)))
        outs["hgrn_s"].append(jnp.transpose(new_state, (3, 0, 1, 2)))
        outs["sgv_s"].append(vn[:, None])

    st = {k: jnp.stack(v, axis=0) for k, v in outs.items()}
    return (xp.reshape(nb, t, d), xs.reshape(ns, 1, d), st["pool_p"], st["pool_s"], st["hgrn_p"], st["hgrn_s"],
            st["k_p"], st["k_s"], st["v_p"], st["v_s"], st["sgv_p"], st["sgv_s"])
```
